```python
import jax, jax.numpy as jnp
from jax import lax
import numpy as np

D_MODEL = 1024
BATCH = 16
SEQ = 256
DEPTH = 4
DEC_BATCH = 4
DEC_SEQ = 2048
PAST_LEN = 512

GRID_W = 64
N_MIXERS = 3
N_A_LAYERS = (DEPTH + 2) // 3
N_B_LAYERS = (DEPTH + 1) // 3
N_MLA_LAYERS = DEPTH // 3
N_GQA_LAYERS = N_A_LAYERS + N_B_LAYERS
N_HEADS = 16
N_KV_HEADS = 4
GROUP = N_HEADS // N_KV_HEADS
HEAD_DIM = 64
Q_BLOCK = 128
WINDOW = 128
MLA_HEADS = 16
Q_RANK = 256
KV_RANK = 128
MLA_NOPE_DIM = 64
MLA_ROPE_DIM = 32
MLA_V_DIM = 64
MLA_SCALE = (MLA_NOPE_DIM + MLA_ROPE_DIM) ** -0.5
GQA_SCALE = HEAD_DIM ** -0.5
N_EXPERTS = 16
EXPERT_FF = 2048
CAPACITY_FACTOR = 2
ROPE_THETA = 10000.0
NORM_EPS = 1e-6
LN_EPS = 1e-5
ALPHA = (2 * DEPTH) ** 0.25
BETA = (8 * DEPTH) ** -0.25

kernel_name = 'hybrid_diffusion_trunk_step'


def layer_norm(x, g, b):
    xf = x.astype(jnp.float32)
    mu = jnp.mean(xf, -1, keepdims=True)
    var = jnp.mean(jnp.square(xf - mu), -1, keepdims=True)
    return ((xf - mu) * lax.rsqrt(var + LN_EPS) * g + b).astype(x.dtype)


def rms_norm(x, g):
    xf = x.astype(jnp.float32)
    return (xf * lax.rsqrt(jnp.mean(jnp.square(xf), -1, keepdims=True) + NORM_EPS) * g).astype(x.dtype)


def adaln(cond, w, b):
    m = jax.nn.silu(cond.reshape(-1, D_MODEL)) @ w + b
    m = m.reshape(m.shape[0], 1, 6, D_MODEL)
    return tuple(m[:, :, k] for k in range(6))


def modulate(x, shift, scale):
    return x * (1 + scale) + shift


def axial_rope(n_tokens, dim):
    n_rows = n_tokens // GRID_W
    rows = jnp.repeat(jnp.arange(n_rows, dtype=jnp.float32), GRID_W)
    cols = jnp.tile(jnp.arange(GRID_W, dtype=jnp.float32), n_rows)
    n_freq = dim // 4
    inv_freq = ROPE_THETA ** (-jnp.arange(n_freq, dtype=jnp.float32) / n_freq)
    ang = jnp.concatenate([rows[:, None] * inv_freq, cols[:, None] * inv_freq], -1)
    return jnp.cos(ang), jnp.sin(ang)


def apply_rope(x, cos, sin):
    xf = x.astype(jnp.float32)
    x1, x2 = jnp.split(xf, 2, axis=-1)
    return jnp.concatenate([x1 * cos - x2 * sin, x2 * cos + x1 * sin], -1).astype(x.dtype)


def split_heads_gqa(h, w_qkv):
    b, t, _ = h.shape
    q, k, v = jnp.split(h @ w_qkv, [N_HEADS * HEAD_DIM, (N_HEADS + N_KV_HEADS) * HEAD_DIM], axis=-1)
    q = q.reshape(b, t, N_KV_HEADS, GROUP, HEAD_DIM).transpose(0, 2, 3, 1, 4)
    k = k.reshape(b, t, N_KV_HEADS, HEAD_DIM).transpose(0, 2, 1, 3)
    v = v.reshape(b, t, N_KV_HEADS, HEAD_DIM).transpose(0, 2, 1, 3)
    return q, k, v


def merge_heads(o):
    b, hk, g, t, d = o.shape
    return o.transpose(0, 3, 1, 2, 4).reshape(b, t, hk * g * d)


def blocked_attention(q, k, v, scale, sink=None):
    b, hk, g, t, dk = q.shape
    nb = t // Q_BLOCK
    qb = jnp.moveaxis(q.reshape(b, hk, g, nb, Q_BLOCK, dk), 3, 0)

    def attend(qblk):
        s = jnp.einsum('bhgqd,bhsd->bhgqs', qblk, k, preferred_element_type=jnp.float32) * scale
        if sink is not None:
            s_sink = jnp.broadcast_to(sink.astype(jnp.float32)[None, :, :, None, None], s.shape[:-1] + (1,))
            p = jax.nn.softmax(jnp.concatenate([s, s_sink], -1), axis=-1)[..., :-1]
        else:
            p = jax.nn.softmax(s, axis=-1)
        return jnp.einsum('bhgqs,bhsd->bhgqd', p.astype(v.dtype), v)

    o = lax.map(attend, qb)
    return jnp.moveaxis(o, 0, 3).reshape(b, hk, g, t, v.shape[-1])


def banded_attention_with_context(q, k, v, k_ctx, v_ctx, sink, scale):
    b, hk, g, t, d = q.shape
    nb = t // Q_BLOCK
    span = Q_BLOCK + 2 * WINDOW
    s_ctx_len = k_ctx.shape[2]
    pad = ((0, 0), (0, 0), (WINDOW, WINDOW), (0, 0))
    kp = jnp.pad(k, pad)
    vp = jnp.pad(v, pad)
    qb = jnp.moveaxis(q.reshape(b, hk, g, nb, Q_BLOCK, d), 3, 0)
    sink_f = sink.astype(jnp.float32)[None, :, :, None, None]

    def attend(args):
        n, qblk = args
        start = n * Q_BLOCK
        kw = lax.dynamic_slice_in_dim(kp, start, span, axis=2)
        vw = lax.dynamic_slice_in_dim(vp, start, span, axis=2)
        qpos = start + jnp.arange(Q_BLOCK)
        kpos = start - WINDOW + jnp.arange(span)
        valid = (jnp.abs(kpos[None, :] - qpos[:, None]) <= WINDOW) & (kpos >= 0)[None, :] & (kpos < t)[None, :]
        s_w = jnp.einsum('bhgqd,bhkd->bhgqk', qblk, kw, preferred_element_type=jnp.float32) * scale
        s_w = jnp.where(valid, s_w, -jnp.inf)
        s_c = jnp.einsum('bhgqd,bhsd->bhgqs', qblk, k_ctx, preferred_element_type=jnp.float32) * scale
        s_sink = jnp.broadcast_to(sink_f, s_c.shape[:-1] + (1,))
        p = jax.nn.softmax(jnp.concatenate([s_w, s_c, s_sink], -1), axis=-1)
        p_w = p[..., :span].astype(v.dtype)
        p_c = p[..., span:span + s_ctx_len].astype(v.dtype)
        return jnp.einsum('bhgqk,bhkd->bhgqd', p_w, vw) + jnp.einsum('bhgqs,bhsd->bhgqd', p_c, v_ctx)

    o = lax.map(attend, (jnp.arange(nb), qb))
    return jnp.moveaxis(o, 0, 3).reshape(b, hk, g, t, d)


def mixer_a_context(h, w_qkv, q_norm, k_norm, w_o):
    q, k, v = split_heads_gqa(h, w_qkv)
    q, k = rms_norm(q, q_norm), rms_norm(k, k_norm)
    o = blocked_attention(q, k, v, GQA_SCALE)
    return merge_heads(o) @ w_o, k, v


def mixer_a_latent(h, k_ctx, v_ctx, w_qkv, q_norm, k_norm, w_o):
    q, k, v = split_heads_gqa(h, w_qkv)
    cos, sin = axial_rope(h.shape[1], HEAD_DIM)
    q = apply_rope(rms_norm(q, q_norm), cos, sin)
    k = apply_rope(rms_norm(k, k_norm), cos, sin)
    k_all = jnp.concatenate([k, k_ctx.astype(k.dtype)], axis=2)
    v_all = jnp.concatenate([v, v_ctx.astype(v.dtype)], axis=2)
    return merge_heads(blocked_attention(q, k_all, v_all, GQA_SCALE)) @ w_o


def mixer_b_context(h, w_qkv, sink, w_o):
    q, k, v = split_heads_gqa(h, w_qkv)
    o = blocked_attention(q, k, v, GQA_SCALE, sink=sink.reshape(N_KV_HEADS, GROUP))
    return merge_heads(o) @ w_o, k, v


def mixer_b_latent(h, k_ctx, v_ctx, w_qkv, sink, w_o):
    q, k, v = split_heads_gqa(h, w_qkv)
    cos, sin = axial_rope(h.shape[1], HEAD_DIM)
    q, k = apply_rope(q, cos, sin), apply_rope(k, cos, sin)
    o = banded_attention_with_context(q, k, v, k_ctx.astype(k.dtype), v_ctx.astype(v.dtype),
                                      sink.reshape(N_KV_HEADS, GROUP), GQA_SCALE)
    return merge_heads(o) @ w_o


def mla_project(h, w_down, q_norm, w_uq, kv_norm):
    b, t, _ = h.shape
    q_a, c_kv, k_pe = jnp.split(h @ w_down, [Q_RANK, Q_RANK + KV_RANK], axis=-1)
    q = (rms_norm(q_a, q_norm) @ w_uq).reshape(b, t, MLA_HEADS, MLA_NOPE_DIM + MLA_ROPE_DIM).transpose(0, 2, 1, 3)
    return q, rms_norm(c_kv, kv_norm), k_pe


def mla_keys_values(c_kv, k_pe, w_ukv):
    b, s, _ = c_kv.shape
    kv = (c_kv @ w_ukv).reshape(b, s, MLA_HEADS, MLA_NOPE_DIM + MLA_V_DIM).transpose(0, 2, 1, 3)
    k_nope, v = jnp.split(kv, [MLA_NOPE_DIM], axis=-1)
    k_rope = jnp.broadcast_to(k_pe[:, None], (b, MLA_HEADS, s, MLA_ROPE_DIM))
    return jnp.concatenate([k_nope, k_rope], -1), v


def mixer_c_context(h, w_down, q_norm, w_uq, kv_norm, w_ukv, w_o):
    q, c_kv, k_pe = mla_project(h, w_down, q_norm, w_uq, kv_norm)
    k, v = mla_keys_values(c_kv, k_pe, w_ukv)
    o = blocked_attention(q[:, :, None], k, v, MLA_SCALE)
    return merge_heads(o) @ w_o, c_kv, k_pe


def mixer_c_latent(h, ckv_ctx, kpe_ctx, w_down, q_norm, w_uq, kv_norm, w_ukv, w_o):
    q, c_kv, k_pe = mla_project(h, w_down, q_norm, w_uq, kv_norm)
    cos, sin = axial_rope(h.shape[1], MLA_ROPE_DIM)
    q = jnp.concatenate([q[..., :MLA_NOPE_DIM], apply_rope(q[..., MLA_NOPE_DIM:], cos, sin)], -1)
    k_pe = apply_rope(k_pe, cos, sin)
    c_all = jnp.concatenate([c_kv, ckv_ctx.astype(c_kv.dtype)], axis=1)
    kpe_all = jnp.concatenate([k_pe, kpe_ctx.astype(k_pe.dtype)], axis=1)
    k, v = mla_keys_values(c_all, kpe_all, w_ukv)
    return merge_heads(blocked_attention(q[:, :, None], k, v, MLA_SCALE)) @ w_o


def ec_moe(h, w_router, w_gate, w_up, w_down):
    b, t, d = h.shape
    xt = h.reshape(b * t, d)
    cap = CAPACITY_FACTOR * (b * t) // N_EXPERTS
    aff = jax.nn.softmax(jnp.einsum('nd,de->ne', xt, w_router, preferred_element_type=jnp.float32), axis=-1)
    gate, idx = lax.top_k(aff.T, cap)
    xe = jnp.take(xt, idx, axis=0)
    hid = jax.nn.silu(jnp.einsum('ecd,edf->ecf', xe, w_gate)) * jnp.einsum('ecd,edf->ecf', xe, w_up)
    ye = jnp.einsum('ecf,efd->ecd', hid, w_down) * gate[..., None].astype(h.dtype)
    out = jnp.zeros_like(xt).at[idx.reshape(-1)].add(ye.reshape(-1, d))
    return out.reshape(b, t, d)


def setup_inputs(seed: int = 0) -> dict:
    key = jax.random.key(seed)
    ks = iter(jax.random.split(key, 48))

    def nrm(shape, scale=1.0):
        return jax.random.normal(next(ks), shape, jnp.float32) * scale

    d = D_MODEL
    qkv_w = (N_HEADS + 2 * N_KV_HEADS) * HEAD_DIM
    attn_w = N_HEADS * HEAD_DIM
    return {
        'x_prompt': nrm((BATCH, SEQ, d)),
        'x_sample': nrm((DEC_BATCH, DEC_SEQ, d)),
        'cache_gqa_k': nrm((DEC_BATCH, N_GQA_LAYERS, N_KV_HEADS, PAST_LEN, HEAD_DIM)),
        'cache_gqa_v': nrm((DEC_BATCH, N_GQA_LAYERS, N_KV_HEADS, PAST_LEN, HEAD_DIM)),
        'cache_mla_ckv': nrm((DEC_BATCH, N_MLA_LAYERS, PAST_LEN, KV_RANK)),
        'cache_mla_kpe': nrm((DEC_BATCH, N_MLA_LAYERS, PAST_LEN, MLA_ROPE_DIM)),
        'c': nrm((DEC_BATCH, d)),
        'c_ctx': nrm((d,)),
        'ada_w': nrm((DEPTH, d, 6 * d), 0.5 * d ** -0.5),
        'ada_b': nrm((DEPTH, 6 * d), 0.02),
        'ln_g': 1.0 + nrm((DEPTH, 2, d), 0.02),
        'ln_b': nrm((DEPTH, 2, d), 0.02),
        'a_w_qkv': nrm((N_A_LAYERS, d, qkv_w), d ** -0.5),
        'a_q_norm': 1.0 + nrm((N_A_LAYERS, HEAD_DIM), 0.02),
        'a_k_norm': 1.0 + nrm((N_A_LAYERS, HEAD_DIM), 0.02),
        'a_w_o': nrm((N_A_LAYERS, attn_w, d), BETA * attn_w ** -0.5),
        'b_w_qkv': nrm((N_B_LAYERS, d, qkv_w), d ** -0.5),
        'b_sink': nrm((N_B_LAYERS, N_HEADS), 0.5),
        'b_w_o': nrm((N_B_LAYERS, attn_w, d), BETA * attn_w ** -0.5),
        'mla_w_down': nrm((N_MLA_LAYERS, d, Q_RANK + KV_RANK + MLA_ROPE_DIM), d ** -0.5),
        'mla_q_norm': 1.0 + nrm((N_MLA_LAYERS, Q_RANK), 0.02),
        'mla_w_uq': nrm((N_MLA_LAYERS, Q_RANK, MLA_HEADS * (MLA_NOPE_DIM + MLA_ROPE_DIM)), Q_RANK ** -0.5),
        'mla_kv_norm': 1.0 + nrm((N_MLA_LAYERS, KV_RANK), 0.02),
        'mla_w_ukv': nrm((N_MLA_LAYERS, KV_RANK, MLA_HEADS * (MLA_NOPE_DIM + MLA_V_DIM)), KV_RANK ** -0.5),
        'mla_w_o': nrm((N_MLA_LAYERS, MLA_HEADS * MLA_V_DIM, d), BETA * (MLA_HEADS * MLA_V_DIM) ** -0.5),
        'moe_w_router': nrm((DEPTH, d, N_EXPERTS), d ** -0.5),
        'moe_w_gate': nrm((DEPTH, N_EXPERTS, d, EXPERT_FF), d ** -0.5),
        'moe_w_up': nrm((DEPTH, N_EXPERTS, d, EXPERT_FF), d ** -0.5),
        'moe_w_down': nrm((DEPTH, N_EXPERTS, EXPERT_FF, d), BETA * EXPERT_FF ** -0.5),
    }


def reference(x_prompt, x_sample, cache_gqa_k, cache_gqa_v, cache_mla_ckv, cache_mla_kpe, c, c_ctx,
              ada_w, ada_b, ln_g, ln_b, a_w_qkv, a_q_norm, a_k_norm, a_w_o, b_w_qkv, b_sink, b_w_o,
              mla_w_down, mla_q_norm, mla_w_uq, mla_kv_norm, mla_w_ukv, mla_w_o,
              moe_w_router, moe_w_gate, moe_w_up, moe_w_down):
    xp = x_prompt
    xs = x_sample
    new_k, new_v, new_ckv, new_kpe = [], [], [], []
    gqa_slot = 0
    for i in range(DEPTH):
        kind, j = i % N_MIXERS, i // N_MIXERS
        mp = adaln(c_ctx, ada_w[i], ada_b[i])
        ms = adaln(c, ada_w[i], ada_b[i])
        hp = modulate(xp, mp[0], mp[1])
        hs = modulate(xs, ms[0], ms[1])
        if kind == 0:
            op, kc, vc = mixer_a_context(hp, a_w_qkv[j], a_q_norm[j], a_k_norm[j], a_w_o[j])
            os_ = mixer_a_latent(hs, cache_gqa_k[:, gqa_slot], cache_gqa_v[:, gqa_slot],
                                 a_w_qkv[j], a_q_norm[j], a_k_norm[j], a_w_o[j])
            new_k.append(kc)
            new_v.append(vc)
            gqa_slot += 1
        elif kind == 1:
            op, kc, vc = mixer_b_context(hp, b_w_qkv[j], b_sink[j], b_w_o[j])
            os_ = mixer_b_latent(hs, cache_gqa_k[:, gqa_slot], cache_gqa_v[:, gqa_slot],
                                 b_w_qkv[j], b_sink[j], b_w_o[j])
            new_k.append(kc)
            new_v.append(vc)
            gqa_slot += 1
        else:
            op, ckv, kpe = mixer_c_context(hp, mla_w_down[j], mla_q_norm[j], mla_w_uq[j],
                                           mla_kv_norm[j], mla_w_ukv[j], mla_w_o[j])
            os_ = mixer_c_latent(hs, cache_mla_ckv[:, j], cache_mla_kpe[:, j], mla_w_down[j], mla_q_norm[j],
                                 mla_w_uq[j], mla_kv_norm[j], mla_w_ukv[j], mla_w_o[j])
            new_ckv.append(ckv)
            new_kpe.append(kpe)
        xp = layer_norm(ALPHA * xp + mp[2] * op, ln_g[i, 0], ln_b[i, 0])
        xs = layer_norm(ALPHA * xs + ms[2] * os_, ln_g[i, 0], ln_b[i, 0])
        hp = modulate(xp, mp[3], mp[4])
        hs = modulate(xs, ms[3], ms[4])
        fp = ec_moe(hp, moe_w_router[i], moe_w_gate[i], moe_w_up[i], moe_w_down[i])
        fs = ec_moe(hs, moe_w_router[i], moe_w_gate[i], moe_w_up[i], moe_w_down[i])
        xp = layer_norm(ALPHA * xp + mp[5] * fp, ln_g[i, 1], ln_b[i, 1])
        xs = layer_norm(ALPHA * xs + ms[5] * fs, ln_g[i, 1], ln_b[i, 1])
    new_gqa_k = jnp.stack(new_k, axis=1)
    new_gqa_v = jnp.stack(new_v, axis=1)
    new_mla_ckv = jnp.stack(new_ckv, axis=1)
    new_mla_kpe = jnp.stack(new_kpe, axis=1)
    return (xp, xs, new_gqa_k, new_gqa_v, new_mla_ckv, new_mla_kpe)
```

```python
import functools
import math

import jax
import jax.numpy as jnp
from jax import lax
from jax.experimental import pallas as pl
from jax.experimental.pallas import tpu as pltpu

F32 = jnp.float32
BF16 = jnp.bfloat16

D_MODEL = 1024
BATCH = 16
SEQ = 256
DEPTH = 4
DEC_BATCH = 4
DEC_SEQ = 2048
PAST_LEN = 512
GRID_W = 64
N_HEADS = 16
N_KV_HEADS = 4
GROUP = N_HEADS // N_KV_HEADS
HEAD_DIM = 64
WINDOW = 128
MLA_HEADS = 16
Q_RANK = 256
KV_RANK = 128
MLA_NOPE_DIM = 64
MLA_ROPE_DIM = 32
MLA_V_DIM = 64
MLA_QK_DIM = MLA_NOPE_DIM + MLA_ROPE_DIM
MLA_SCALE = MLA_QK_DIM ** -0.5
GQA_SCALE = HEAD_DIM ** -0.5
N_EXPERTS = 16
EXPERT_FF = 2048
CAPACITY_FACTOR = 2
ROPE_THETA = 10000.0
NORM_EPS = 1e-6
LN_EPS = 1e-5
ALPHA = (2 * DEPTH) ** 0.25
LOG2E = math.log2(math.e)

LANES = 128
N_PROMPT = BATCH * SEQ
N_SAMPLE = DEC_BATCH * DEC_SEQ
N_TOK = N_PROMPT + N_SAMPLE
TM = 256
PROMPT_TILES = N_PROMPT // TM
TILES = N_TOK // TM
SAMPLE_TILES_PER_REQ = DEC_SEQ // TM
N_COND = 8
CAP_P = CAPACITY_FACTOR * N_PROMPT // N_EXPERTS
CAP_S = CAPACITY_FACTOR * N_SAMPLE // N_EXPERTS
CAP = CAP_P + CAP_S
FF_TILE = 512
KV_CHUNK = 512
MLA_TQ = 512

_NT = (((1,), (1,)), ((), ()))
_VMEM_LIMIT = 56 * 1024 * 1024


def _params(n_axes, vmem=_VMEM_LIMIT):
    return pltpu.CompilerParams(dimension_semantics=("arbitrary",) * n_axes, vmem_limit_bytes=vmem)


def _cond_of_tile(i):
    return jnp.where(i < PROMPT_TILES, 0, 1 + (i - PROMPT_TILES) // SAMPLE_TILES_PER_REQ)


def _pos_block_of_tile(i):
    return jnp.where(i < PROMPT_TILES, 0, 1 + (i - PROMPT_TILES) % SAMPLE_TILES_PER_REQ)


def _mod_spec(layer, k):
    return pl.BlockSpec((None, None, None, 1, D_MODEL), lambda i: (layer, k, _cond_of_tile(i), 0, 0))


def _row_spec(width):
    return pl.BlockSpec((TM, width), lambda i: (i, 0))


def _full_spec(shape):
    nd = len(shape)
    return pl.BlockSpec(shape, lambda *_: (0,) * nd)


def _split_bf16(a):
    hi = a.astype(BF16)
    lo = (a - hi.astype(F32)).astype(BF16)
    return hi, lo


def _dot3(a, b, dims=(((1,), (0,)), ((), ()))):
    a_hi, a_lo = _split_bf16(a)
    b_hi, b_lo = _split_bf16(b)
    dg = functools.partial(lax.dot_general, dimension_numbers=dims, preferred_element_type=F32)
    return dg(a_hi, b_hi) + dg(a_lo, b_hi) + dg(a_hi, b_lo)


def _layer_norm(y, g, b):
    mu = jnp.mean(y, axis=-1, keepdims=True)
    d = y - mu
    var = jnp.mean(d * d, axis=-1, keepdims=True)
    return d * lax.rsqrt(var + LN_EPS) * g + b


def _rope_lanes(t, cos, sin, half):
    lane = lax.broadcasted_iota(jnp.int32, t.shape, 1)
    first = (lane & (2 * half - 1)) < half
    partner = jnp.where(first, pltpu.roll(t, LANES - half, 1), pltpu.roll(t, half, 1))
    return t * cos + partner * sin


def _pair_rms(t, gain):
    sq = t * t
    lane = lax.broadcasted_iota(jnp.int32, t.shape, 1)
    lo = lane < HEAD_DIM
    s_lo = jnp.sum(jnp.where(lo, sq, 0.0), axis=-1, keepdims=True)
    s_hi = jnp.sum(jnp.where(lo, 0.0, sq), axis=-1, keepdims=True)
    ms = jnp.where(lo, s_lo, s_hi) * (1.0 / HEAD_DIM)
    return t * lax.rsqrt(ms + NORM_EPS) * gain


def _adaln_kernel(c_ref, w_ref, b_ref, o_ref):
    c = c_ref[...]
    a = c / (1.0 + jnp.exp(-c))
    o_ref[...] = _dot3(a, w_ref[...]) + b_ref[...]


def _adaln(cond, ada_w, ada_b):
    return pl.pallas_call(
        _adaln_kernel,
        out_shape=jax.ShapeDtypeStruct((DEPTH, 6, N_COND, D_MODEL), F32),
        grid=(DEPTH, 6),
        in_specs=[
            pl.BlockSpec((N_COND, D_MODEL), lambda l, k: (0, 0)),
            pl.BlockSpec((None, D_MODEL, D_MODEL), lambda l, k: (l, 0, k)),
            pl.BlockSpec((None, None, 1, D_MODEL), lambda l, k: (l, k, 0, 0)),
        ],
        out_specs=pl.BlockSpec((None, None, N_COND, D_MODEL), lambda l, k: (l, k, 0, 0)),
        compiler_params=_params(2),
        name="adaln",
    )(cond, ada_w, ada_b.reshape(DEPTH, 6, 1, D_MODEL))


QK_WIDTH = (N_HEADS + N_KV_HEADS) * HEAD_DIM
QKV_WIDTH = (N_HEADS + 2 * N_KV_HEADS) * HEAD_DIM
Q_WIDTH = N_HEADS * HEAD_DIM


def _gqa_proj_kernel(x_ref, sh_ref, sc_ref, w_ref, gain_ref, cos_ref, sin_ref,
                     q_ref, kh_ref, vh_ref, k32_ref, v32_ref, *, use_norm):
    h = (x_ref[...] * (1.0 + sc_ref[...]) + sh_ref[...]).astype(BF16)
    acc = jnp.dot(h, w_ref[...], preferred_element_type=F32)
    cos = cos_ref[...]
    sin = sin_ref[...]
    for j in range(QK_WIDTH // LANES):
        t = acc[:, j * LANES:(j + 1) * LANES]
        if use_norm:
            t = _pair_rms(t, gain_ref[:, j * LANES:(j + 1) * LANES])
        t = _rope_lanes(t, cos, sin, HEAD_DIM // 2)
        if j < Q_WIDTH // LANES:
            q_ref[:, j * LANES:(j + 1) * LANES] = (t * GQA_SCALE).astype(BF16)
        else:
            for half in range(2):
                kv_head = 2 * (j - Q_WIDTH // LANES) + half
                th = t[:, half * HEAD_DIM:(half + 1) * HEAD_DIM]
                kh_ref[kv_head] = th.astype(BF16)
                k32_ref[kv_head] = th
    for kv_head in range(N_KV_HEADS):
        lo = QK_WIDTH + kv_head * HEAD_DIM
        tv = acc[:, lo:lo + HEAD_DIM]
        vh_ref[kv_head] = tv.astype(BF16)
        v32_ref[kv_head] = tv


def _gqa_proj(x, mods5, layer, w_qkv, gain, cos_t, sin_t, use_norm):
    head_spec = pl.BlockSpec((N_KV_HEADS, TM, HEAD_DIM), lambda i: (0, i, 0))
    pos_spec = pl.BlockSpec((TM, LANES), lambda i: (_pos_block_of_tile(i), 0))
    return pl.pallas_call(
        functools.partial(_gqa_proj_kernel, use_norm=use_norm),
        out_shape=(
            jax.ShapeDtypeStruct((N_TOK, Q_WIDTH), BF16),
            jax.ShapeDtypeStruct((N_KV_HEADS, N_TOK, HEAD_DIM), BF16),
            jax.ShapeDtypeStruct((N_KV_HEADS, N_TOK, HEAD_DIM), BF16),
            jax.ShapeDtypeStruct((N_KV_HEADS, N_TOK, HEAD_DIM), F32),
            jax.ShapeDtypeStruct((N_KV_HEADS, N_TOK, HEAD_DIM), F32),
        ),
        grid=(TILES,),
        in_specs=[
            _row_spec(D_MODEL), _mod_spec(layer, 0), _mod_spec(layer, 1),
            _full_spec((D_MODEL, QKV_WIDTH)), _full_spec((1, QK_WIDTH)),
            pos_spec, pos_spec,
        ],
        out_specs=(_row_spec(Q_WIDTH), head_spec, head_spec, head_spec, head_spec),
        compiler_params=_params(1),
        name="gqa_proj",
    )(x, mods5, mods5, w_qkv, gain, cos_t, sin_t)


def _attend(qs, chunks, sink_col, c1, s_scr):
    m_rows = qs.shape[0]
    m = sink_col
    offs = []
    off = 0
    for k_fn, _, bias, width in chunks:
        s = lax.dot_general(qs, k_fn(), _NT, preferred_element_type=F32)
        if bias is not None:
            s = s + bias
        s_scr[0:m_rows, off:off + width] = s
        mc = jnp.max(s, axis=-1, keepdims=True)
        m = mc if m is None else jnp.maximum(m, mc)
        offs.append(off)
        off += width
    den = None if sink_col is None else jnp.exp2((sink_col - m) * c1)
    acc = None
    for (_, v_fn, _, width), o in zip(chunks, offs):
        p = jnp.exp2((s_scr[0:m_rows, o:o + width] - m) * c1)
        ps = jnp.sum(p, axis=-1, keepdims=True)
        den = ps if den is None else den + ps
        pv = jnp.dot(p.astype(BF16), v_fn(), preferred_element_type=F32)
        acc = pv if acc is None else acc + pv
    return acc / den


def _chunk_list(k_ref, v_ref, length, cast):
    out = []
    width = min(KV_CHUNK, length)
    for c0 in range(0, length, width):
        if cast:
            k_fn = lambda c0=c0: k_ref[c0:c0 + width, :].astype(BF16)
            v_fn = lambda c0=c0: v_ref[c0:c0 + width, :].astype(BF16)
        else:
            k_fn = lambda c0=c0: k_ref[c0:c0 + width, :]
            v_fn = lambda c0=c0: v_ref[c0:c0 + width, :]
        out.append((k_fn, v_fn, None, width))
    return out


def _gqa_attn_kernel(*refs, tq, n_keys, has_ctx, use_sink, window):
    refs = list(refs)
    q_ref, k_ref, v_ref = refs[:3]
    pos = 3
    if has_ctx:
        kc_ref, vc_ref = refs[pos:pos + 2]
        pos += 2
    if use_sink:
        sink_ref = refs[pos]
        pos += 1
    o_ref, s_scr = refs[pos:pos + 2]
    g = pl.program_id(1)
    qi = pl.program_id(2)

    qf = q_ref[...].astype(F32)
    qs = jnp.concatenate(
        [qf[:, j * HEAD_DIM:(j + 1) * HEAD_DIM] for j in range(GROUP)], axis=0).astype(BF16)

    if window:
        span = tq + 2 * WINDOW
        kstart = pl.multiple_of(jnp.clip(qi * tq - WINDOW, 0, n_keys - span), LANES)
        qpos = qi * tq + lax.broadcasted_iota(jnp.int32, (tq, span), 0)
        kpos = kstart + lax.broadcasted_iota(jnp.int32, (tq, span), 1)
        band = jnp.where(jnp.abs(kpos - qpos) <= WINDOW, 0.0, -jnp.inf).astype(F32)
        bias = jnp.concatenate([band] * GROUP, axis=0)
        chunks = [(lambda: k_ref[pl.ds(kstart, span), :], lambda: v_ref[pl.ds(kstart, span), :], bias, span)]
    else:
        chunks = _chunk_list(k_ref, v_ref, n_keys, cast=False)
    if has_ctx:
        chunks += _chunk_list(kc_ref, vc_ref, PAST_LEN, cast=True)

    sink_col = None
    if use_sink:
        sink_col = jnp.concatenate(
            [jnp.full((tq, 1), sink_ref[g * GROUP + j], F32) for j in range(GROUP)], axis=0)

    o = _attend(qs, chunks, sink_col, LOG2E, s_scr)
    o_ref[...] = jnp.concatenate([o[j * tq:(j + 1) * tq, :] for j in range(GROUP)], axis=-1).astype(BF16)


def _gqa_attention(q, kh, vh, ctx_k, ctx_v, slot, sink, *, latent, window):
    tq = TM
    if latent:
        n_b, n_keys, n_qt = DEC_BATCH, DEC_SEQ, DEC_SEQ // tq
        row0 = N_PROMPT // tq
        kv_blk0 = N_PROMPT // DEC_SEQ
    else:
        n_b, n_keys, n_qt = BATCH, SEQ, 1
        row0 = 0
        kv_blk0 = 0
    q_spec = pl.BlockSpec((tq, GROUP * HEAD_DIM), lambda b, g, t: (row0 + b * n_qt + t, g))
    kv_spec = pl.BlockSpec((None, n_keys, HEAD_DIM), lambda b, g, t: (g, kv_blk0 + b, 0))
    in_specs = [q_spec, kv_spec, kv_spec]
    args = [q, kh, vh]
    if latent:
        ctx_spec = pl.BlockSpec((None, None, None, PAST_LEN, HEAD_DIM), lambda b, g, t: (b, slot, g, 0, 0))
        in_specs += [ctx_spec, ctx_spec]
        args += [ctx_k, ctx_v]
    use_sink = sink is not None
    if use_sink:
        in_specs.append(pl.BlockSpec(memory_space=pltpu.SMEM))
        args.append(sink)
    n_cols = (tq + 2 * WINDOW if window else n_keys) + (PAST_LEN if latent else 0)
    return pl.pallas_call(
        functools.partial(_gqa_attn_kernel, tq=tq, n_keys=n_keys, has_ctx=latent,
                          use_sink=use_sink, window=window),
        out_shape=jax.ShapeDtypeStruct((n_b * n_keys, Q_WIDTH), BF16),
        grid=(n_b, N_KV_HEADS, n_qt),
        in_specs=in_specs,
        out_specs=pl.BlockSpec((tq, GROUP * HEAD_DIM), lambda b, g, t: (b * n_qt + t, g)),
        scratch_shapes=[pltpu.VMEM((GROUP * tq, n_cols), F32)],
        compiler_params=_params(3),
        name="gqa_attn_latent" if latent else "gqa_attn_context",
    )(*args)


MLA_DOWN_EXT = Q_RANK + KV_RANK + LANES
MLA_HEAD_LANES = LANES
MLA_ROPE_LO = MLA_NOPE_DIM


def _mla_proj_kernel(x_ref, sh_ref, sc_ref, wd_ref, qg_ref, wuq_ref, kvg_ref, cos_ref, sin_ref,
                     q_ref, ckv_ref, kpe_ref):
    h = (x_ref[...] * (1.0 + sc_ref[...]) + sh_ref[...]).astype(BF16)
    acc = jnp.dot(h, wd_ref[...], preferred_element_type=F32)
    cos = cos_ref[...]
    sin = sin_ref[...]
    qa = acc[:, :Q_RANK]
    qn = qa * lax.rsqrt(jnp.mean(qa * qa, axis=-1, keepdims=True) + NORM_EPS) * qg_ref[...]
    q = jnp.dot(qn.astype(BF16), wuq_ref[...], preferred_element_type=F32)
    for hd in range(MLA_HEADS):
        t = q[:, hd * LANES:(hd + 1) * LANES]
        q_ref[:, hd * LANES:(hd + 1) * LANES] = _rope_lanes(t, cos, sin, MLA_ROPE_DIM // 2).astype(BF16)
    ckv = acc[:, Q_RANK:Q_RANK + KV_RANK]
    ckv_ref[...] = ckv * lax.rsqrt(jnp.mean(ckv * ckv, axis=-1, keepdims=True) + NORM_EPS) * kvg_ref[...]
    kpe_ref[...] = _rope_lanes(acc[:, Q_RANK + KV_RANK:], cos, sin, MLA_ROPE_DIM // 2)


def _mla_proj(x, mods5, layer, wd_ext, q_gain, wuq_pad, kv_gain, cos_t, sin_t):
    pos_spec = pl.BlockSpec((TM, LANES), lambda i: (_pos_block_of_tile(i), 0))
    return pl.pallas_call(
        _mla_proj_kernel,
        out_shape=(
            jax.ShapeDtypeStruct((N_TOK, MLA_HEADS * LANES), BF16),
            jax.ShapeDtypeStruct((N_TOK, KV_RANK), F32),
            jax.ShapeDtypeStruct((N_TOK, LANES), F32),
        ),
        grid=(TILES,),
        in_specs=[
            _row_spec(D_MODEL), _mod_spec(layer, 0), _mod_spec(layer, 1),
            _full_spec((D_MODEL, MLA_DOWN_EXT)), _full_spec((1, Q_RANK)),
            _full_spec((Q_RANK, MLA_HEADS * LANES)), _full_spec((1, KV_RANK)),
            pos_spec, pos_spec,
        ],
        out_specs=(_row_spec(MLA_HEADS * LANES), _row_spec(KV_RANK), _row_spec(LANES)),
        compiler_params=_params(1),
        name="mla_proj",
    )(x, mods5, mods5, wd_ext, q_gain, wuq_pad, kv_gain, cos_t, sin_t)


def _mla_kv_kernel(c_ref, kpe_ref, wk_ref, wv_ref, k_ref, v_ref):
    c = c_ref[...].astype(BF16)
    kk = jnp.dot(c, wk_ref[...], preferred_element_type=F32)
    vv = jnp.dot(c, wv_ref[...], preferred_element_type=F32)
    kpe = kpe_ref[...]
    for hd in range(MLA_HEADS):
        k_ref[hd] = (kk[:, hd * LANES:(hd + 1) * LANES] + kpe).astype(BF16)
        v_ref[hd] = vv[:, hd * MLA_V_DIM:(hd + 1) * MLA_V_DIM].astype(BF16)


def _mla_kv(c_all, kpe_all, wk_pad, wv):
    n_rows = c_all.shape[0]
    return pl.pallas_call(
        _mla_kv_kernel,
        out_shape=(
            jax.ShapeDtypeStruct((MLA_HEADS, n_rows, LANES), BF16),
            jax.ShapeDtypeStruct((MLA_HEADS, n_rows, MLA_V_DIM), BF16),
        ),
        grid=(n_rows // TM,),
        in_specs=[
            _row_spec(KV_RANK), _row_spec(LANES),
            _full_spec((KV_RANK, MLA_HEADS * LANES)), _full_spec((KV_RANK, MLA_HEADS * MLA_V_DIM)),
        ],
        out_specs=(
            pl.BlockSpec((MLA_HEADS, TM, LANES), lambda i: (0, i, 0)),
            pl.BlockSpec((MLA_HEADS, TM, MLA_V_DIM), lambda i: (0, i, 0)),
        ),
        compiler_params=_params(1),
        name="mla_kv",
    )(c_all, kpe_all, wk_pad, wv)


def _mla_attn_kernel(*refs, tq, n_keys, has_ctx):
    refs = list(refs)
    q_ref, k_ref, v_ref = refs[:3]
    pos = 3
    if has_ctx:
        kc_ref, vc_ref = refs[pos:pos + 2]
        pos += 2
    o_ref, s_scr = refs[pos:pos + 2]
    outs = []
    for j in range(2):
        qs = q_ref[:, j * LANES:(j + 1) * LANES]
        chunks = _chunk_list(k_ref.at[j], v_ref.at[j], n_keys, cast=False)
        if has_ctx:
            chunks += _chunk_list(kc_ref.at[j], vc_ref.at[j], PAST_LEN, cast=False)
        outs.append(_attend(qs, chunks, None, MLA_SCALE * LOG2E, s_scr))
    o_ref[...] = jnp.concatenate(outs, axis=-1).astype(BF16)


def _mla_attention(q, k_all, v_all, *, latent):
    if latent:
        tq = MLA_TQ
        n_b, n_keys, n_qt = DEC_BATCH, DEC_SEQ, DEC_SEQ // tq
        row0 = N_PROMPT // tq
        kv_blk0 = N_PROMPT // DEC_SEQ
    else:
        tq = TM
        n_b, n_keys, n_qt = BATCH, SEQ, 1
        row0 = 0
        kv_blk0 = 0
    q_spec = pl.BlockSpec((tq, 2 * LANES), lambda b, hp, t: (row0 + b * n_qt + t, hp))
    k_spec = pl.BlockSpec((2, n_keys, LANES), lambda b, hp, t: (hp, kv_blk0 + b, 0))
    v_spec = pl.BlockSpec((2, n_keys, MLA_V_DIM), lambda b, hp, t: (hp, kv_blk0 + b, 0))
    in_specs = [q_spec, k_spec, v_spec]
    args = [q, k_all, v_all]
    if latent:
        ctx0 = N_TOK // PAST_LEN
        in_specs += [
            pl.BlockSpec((2, PAST_LEN, LANES), lambda b, hp, t: (hp, ctx0 + b, 0)),
            pl.BlockSpec((2, PAST_LEN, MLA_V_DIM), lambda b, hp, t: (hp, ctx0 + b, 0)),
        ]
        args += [k_all, v_all]
    n_cols = n_keys + (PAST_LEN if latent else 0)
    return pl.pallas_call(
        functools.partial(_mla_attn_kernel, tq=tq, n_keys=n_keys, has_ctx=latent),
        out_shape=jax.ShapeDtypeStruct((n_b * n_keys, MLA_HEADS * MLA_V_DIM), BF16),
        grid=(n_b, MLA_HEADS // 2, n_qt),
        in_specs=in_specs,
        out_specs=pl.BlockSpec((tq, 2 * MLA_V_DIM), lambda b, hp, t: (b * n_qt + t, hp)),
        scratch_shapes=[pltpu.VMEM((tq, n_cols), F32)],
        compiler_params=_params(3),
        name="mla_attn_latent" if latent else "mla_attn_context",
    )(*args)


def _post_attn_kernel(o_ref, x_ref, wo_ref, g1_ref, sh2_ref, sc2_ref, lng_ref, lnb_ref, wr_ref,
                      x1_ref, h2_ref, aff_ref):
    proj = jnp.dot(o_ref[...], wo_ref[...], preferred_element_type=F32)
    x1 = _layer_norm(ALPHA * x_ref[...] + g1_ref[...] * proj, lng_ref[...], lnb_ref[...])
    x1_ref[...] = x1
    h2 = x1 * (1.0 + sc2_ref[...]) + sh2_ref[...]
    h2_ref[...] = h2.astype(BF16)
    logits_t = _dot3(wr_ref[...], h2, _NT)
    e = jnp.exp(logits_t - jnp.max(logits_t, axis=0, keepdims=True))
    aff_ref[...] = e / jnp.sum(e, axis=0, keepdims=True)


def _post_attn(o, x, w_o, mods5, layer, ln_g, ln_b, w_router_t):
    return pl.pallas_call(
        _post_attn_kernel,
        out_shape=(
            jax.ShapeDtypeStruct((N_TOK, D_MODEL), F32),
            jax.ShapeDtypeStruct((N_TOK, D_MODEL), BF16),
            jax.ShapeDtypeStruct((N_EXPERTS, N_TOK), F32),
        ),
        grid=(TILES,),
        in_specs=[
            _row_spec(D_MODEL), _row_spec(D_MODEL), _full_spec((D_MODEL, D_MODEL)),
            _mod_spec(layer, 2), _mod_spec(layer, 3), _mod_spec(layer, 4),
            _full_spec((1, D_MODEL)), _full_spec((1, D_MODEL)), _full_spec((N_EXPERTS, D_MODEL)),
        ],
        out_specs=(_row_spec(D_MODEL), _row_spec(D_MODEL), pl.BlockSpec((N_EXPERTS, TM), lambda i: (0, i))),
        compiler_params=_params(1),
        name="post_attn",
    )(o, x, w_o, mods5, mods5, mods5, ln_g, ln_b, w_router_t)


FFN_ROWS = 512


def _ffn_kernel(x_ref, wg_ref, wu_ref, wd_ref, gate_ref, o_ref):
    f = pl.program_id(1)
    wg = wg_ref[...].astype(BF16)
    wu = wu_ref[...].astype(BF16)
    wd = wd_ref[...].astype(BF16)

    @pl.when(f == 0)
    def _():
        o_ref[...] = jnp.zeros_like(o_ref)

    for r0 in range(0, CAP, FFN_ROWS):
        x = x_ref[r0:r0 + FFN_ROWS, :]
        a = jnp.dot(x, wg, preferred_element_type=F32)
        u = jnp.dot(x, wu, preferred_element_type=F32)
        hid = (a / (1.0 + jnp.exp(-a)) * u).astype(BF16)
        o_ref[r0:r0 + FFN_ROWS, :] += jnp.dot(hid, wd, preferred_element_type=F32)

    @pl.when(f == EXPERT_FF // FF_TILE - 1)
    def _():
        o_ref[...] = o_ref[...] * gate_ref[...]


def _expert_ffn(xe, gate, w_gate, w_up, w_down, layer):
    return pl.pallas_call(
        _ffn_kernel,
        out_shape=jax.ShapeDtypeStruct((N_EXPERTS, CAP, D_MODEL), F32),
        grid=(N_EXPERTS, EXPERT_FF // FF_TILE),
        in_specs=[
            pl.BlockSpec((None, CAP, D_MODEL), lambda e, f: (e, 0, 0)),
            pl.BlockSpec((None, None, D_MODEL, FF_TILE), lambda e, f: (layer, e, 0, f)),
            pl.BlockSpec((None, None, D_MODEL, FF_TILE), lambda e, f: (layer, e, 0, f)),
            pl.BlockSpec((None, None, FF_TILE, D_MODEL), lambda e, f: (layer, e, f, 0)),
            pl.BlockSpec((None, CAP, 1), lambda e, f: (e, 0, 0)),
        ],
        out_specs=pl.BlockSpec((None, CAP, D_MODEL), lambda e, f: (e, 0, 0)),
        compiler_params=_params(2),
        name="expert_ffn",
    )(xe, w_gate, w_up, w_down, gate)


def _post_moe_kernel(x_ref, f_ref, g2_ref, lng_ref, lnb_ref, o_ref):
    o_ref[...] = _layer_norm(ALPHA * x_ref[...] + g2_ref[...] * f_ref[...], lng_ref[...], lnb_ref[...])


def _post_moe(x1, f, mods5, layer, ln_g, ln_b):
    return pl.pallas_call(
        _post_moe_kernel,
        out_shape=jax.ShapeDtypeStruct((N_TOK, D_MODEL), F32),
        grid=(TILES,),
        in_specs=[_row_spec(D_MODEL), _row_spec(D_MODEL), _mod_spec(layer, 5),
                  _full_spec((1, D_MODEL)), _full_spec((1, D_MODEL))],
        out_specs=_row_spec(D_MODEL),
        compiler_params=_params(1),
        name="post_moe",
    )(x1, f, mods5, ln_g, ln_b)


def _axial_tables(dim):
    n_rows = DEC_SEQ // GRID_W
    rows = jnp.repeat(jnp.arange(n_rows, dtype=F32), GRID_W)
    cols = jnp.tile(jnp.arange(GRID_W, dtype=F32), n_rows)
    n_freq = dim // 4
    inv_freq = ROPE_THETA ** (-jnp.arange(n_freq, dtype=F32) / n_freq)
    ang = jnp.concatenate([rows[:, None] * inv_freq, cols[:, None] * inv_freq], -1)
    cos, sin = jnp.cos(ang), jnp.sin(ang)
    return jnp.concatenate([cos, cos], -1), jnp.concatenate([-sin, sin], -1)


def _with_identity_rows(cos_l, sin_l):
    return (jnp.concatenate([jnp.ones((TM, LANES), F32), cos_l], 0),
            jnp.concatenate([jnp.zeros((TM, LANES), F32), sin_l], 0))


def _gqa_rope_tables():
    cos, sin = _axial_tables(HEAD_DIM)
    return _with_identity_rows(jnp.tile(cos, (1, 2)), jnp.tile(sin, (1, 2)))


def _mla_rope_tables():
    cos, sin = _axial_tables(MLA_ROPE_DIM)
    pad_hi = LANES - MLA_ROPE_LO - MLA_ROPE_DIM
    cos_l = jnp.concatenate([jnp.ones((DEC_SEQ, MLA_ROPE_LO), F32), cos, jnp.ones((DEC_SEQ, pad_hi), F32)], 1)
    sin_l = jnp.pad(sin, ((0, 0), (MLA_ROPE_LO, pad_hi)))
    return _with_identity_rows(cos_l, sin_l)


def _moe(h2, aff_t, w_gate, w_up, w_down, layer):
    xs, gates, idxs = [], [], []
    for lo, n, cap in ((0, N_PROMPT, CAP_P), (N_PROMPT, N_SAMPLE, CAP_S)):
        gate, idx = lax.top_k(aff_t[:, lo:lo + n], cap)
        xs.append(jnp.take(h2[lo:lo + n], idx, axis=0))
        gates.append(gate)
        idxs.append(idx)
    xe = jnp.concatenate(xs, axis=1)
    gate = jnp.concatenate(gates, axis=1)[..., None]
    ye = _expert_ffn(xe, gate, w_gate, w_up, w_down, layer)
    outs = []
    for (lo, n, cap), idx, c0 in zip(((0, N_PROMPT, CAP_P), (N_PROMPT, N_SAMPLE, CAP_S)), idxs, (0, CAP_P)):
        y = ye[:, c0:c0 + cap].reshape(-1, D_MODEL)
        outs.append(jnp.zeros((n, D_MODEL), F32).at[idx.reshape(-1)].add(y))
    return jnp.concatenate(outs, axis=0)


def kernel(x_prompt, x_sample, cache_gqa_k, cache_gqa_v, cache_mla_ckv, cache_mla_kpe, c, c_ctx,
           ada_w, ada_b, ln_g, ln_b, a_w_qkv, a_q_norm, a_k_norm, a_w_o, b_w_qkv, b_sink, b_w_o,
           mla_w_down, mla_q_norm, mla_w_uq, mla_kv_norm, mla_w_ukv, mla_w_o,
           moe_w_router, moe_w_gate, moe_w_up, moe_w_down):
    x = jnp.concatenate([x_prompt.reshape(N_PROMPT, D_MODEL), x_sample.reshape(N_SAMPLE, D_MODEL)], 0)
    cond = jnp.concatenate([c_ctx[None], c, jnp.zeros((N_COND - 1 - DEC_BATCH, D_MODEL), F32)], 0)
    mods5 = _adaln(cond, ada_w, ada_b).reshape(DEPTH, 6, N_COND, 1, D_MODEL)
    gqa_cos, gqa_sin = _gqa_rope_tables()
    mla_cos, mla_sin = _mla_rope_tables()

    new_k, new_v, new_ckv, new_kpe = [], [], [], []
    gqa_slot = 0
    for i in range(DEPTH):
        kind, j = i % 3, i // 3
        if kind in (0, 1):
            if kind == 0:
                w_qkv, w_o, sink = a_w_qkv[j], a_w_o[j], None
                gain = jnp.concatenate([jnp.tile(a_q_norm[j], N_HEADS), jnp.tile(a_k_norm[j], N_KV_HEADS)])[None]
            else:
                w_qkv, w_o, sink = b_w_qkv[j], b_w_o[j], b_sink[j]
                gain = jnp.ones((1, QK_WIDTH), F32)
            q, kh, vh, k32, v32 = _gqa_proj(x, mods5, i, w_qkv.astype(BF16), gain, gqa_cos, gqa_sin,
                                            use_norm=(kind == 0))
            o_p = _gqa_attention(q, kh, vh, None, None, gqa_slot, sink, latent=False, window=False)
            o_s = _gqa_attention(q, kh, vh, cache_gqa_k, cache_gqa_v, gqa_slot, sink,
                                 latent=True, window=(kind == 1))
            for store, arr in ((new_k, k32), (new_v, v32)):
                store.append(arr[:, :N_PROMPT].reshape(N_KV_HEADS, BATCH, SEQ, HEAD_DIM).transpose(1, 0, 2, 3))
            gqa_slot += 1
        else:
            w_down = mla_w_down[j]
            rope_lo = Q_RANK + KV_RANK
            wd_ext = jnp.concatenate([
                w_down[:, :rope_lo], jnp.zeros((D_MODEL, MLA_ROPE_LO), F32), w_down[:, rope_lo:],
                jnp.zeros((D_MODEL, LANES - MLA_ROPE_LO - MLA_ROPE_DIM), F32)], 1).astype(BF16)
            wuq_pad = jnp.pad(mla_w_uq[j].reshape(Q_RANK, MLA_HEADS, MLA_QK_DIM),
                              ((0, 0), (0, 0), (0, LANES - MLA_QK_DIM))).reshape(Q_RANK, MLA_HEADS * LANES)
            w_ukv = mla_w_ukv[j].reshape(KV_RANK, MLA_HEADS, MLA_NOPE_DIM + MLA_V_DIM)
            wk_pad = jnp.pad(w_ukv[:, :, :MLA_NOPE_DIM], ((0, 0), (0, 0), (0, LANES - MLA_NOPE_DIM)))
            wk_pad = wk_pad.reshape(KV_RANK, MLA_HEADS * LANES).astype(BF16)
            wv = w_ukv[:, :, MLA_NOPE_DIM:].reshape(KV_RANK, MLA_HEADS * MLA_V_DIM).astype(BF16)
            q, ckv, kpe = _mla_proj(x, mods5, i, wd_ext, mla_q_norm[j][None], wuq_pad.astype(BF16),
                                    mla_kv_norm[j][None], mla_cos, mla_sin)
            c_all = jnp.concatenate([ckv, cache_mla_ckv[:, j].reshape(DEC_BATCH * PAST_LEN, KV_RANK)], 0)
            kpe_ctx = jnp.pad(cache_mla_kpe[:, j].reshape(DEC_BATCH * PAST_LEN, MLA_ROPE_DIM),
                              ((0, 0), (MLA_ROPE_LO, LANES - MLA_ROPE_LO - MLA_ROPE_DIM)))
            kpe_all = jnp.concatenate([kpe, kpe_ctx], 0)
            k_all, v_all = _mla_kv(c_all, kpe_all, wk_pad, wv)
            o_p = _mla_attention(q, k_all, v_all, latent=False)
            o_s = _mla_attention(q, k_all, v_all, latent=True)
            w_o = mla_w_o[j]
            new_ckv.append(ckv[:N_PROMPT].reshape(BATCH, SEQ, KV_RANK))
            new_kpe.append(kpe[:N_PROMPT, MLA_ROPE_LO:MLA_ROPE_LO + MLA_ROPE_DIM].reshape(BATCH, SEQ, MLA_ROPE_DIM))
        o = jnp.concatenate([o_p, o_s], 0)
        x1, h2, aff_t = _post_attn(o, x, w_o.astype(BF16), mods5, i, ln_g[i, 0][None], ln_b[i, 0][None],
                                   moe_w_router[i].T)
        f = _moe(h2, aff_t, moe_w_gate, moe_w_up, moe_w_down, i)
        x = _post_moe(x1, f, mods5, i, ln_g[i, 1][None], ln_b[i, 1][None])

    y_prompt = x[:N_PROMPT].reshape(BATCH, SEQ, D_MODEL)
    y_sample = x[N_PROMPT:].reshape(DEC_BATCH, DEC_SEQ, D_MODEL)
    return (y_prompt, y_sample, jnp.stack(new_k, 1), jnp.stack(new_v, 1),
            jnp.stack(new_ckv, 1), jnp.stack(new_kpe, 1))
```

```python
import functools
import math

import jax
import jax.numpy as jnp
from jax import lax
from jax.experimental import pallas as pl
from jax.experimental.pallas import tpu as pltpu

F32 = jnp.float32
BF16 = jnp.bfloat16

D_MODEL = 1024
BATCH = 16
SEQ = 256
DEPTH = 4
DEC_BATCH = 4
DEC_SEQ = 2048
PAST_LEN = 512
GRID_W = 64
N_HEADS = 16
N_KV_HEADS = 4
GROUP = N_HEADS // N_KV_HEADS
HEAD_DIM = 64
WINDOW = 128
MLA_HEADS = 16
Q_RANK = 256
KV_RANK = 128
MLA_NOPE_DIM = 64
MLA_ROPE_DIM = 32
MLA_V_DIM = 64
MLA_QK_DIM = MLA_NOPE_DIM + MLA_ROPE_DIM
MLA_SCALE = MLA_QK_DIM ** -0.5
GQA_SCALE = HEAD_DIM ** -0.5
N_EXPERTS = 16
EXPERT_FF = 2048
CAPACITY_FACTOR = 2
ROPE_THETA = 10000.0
NORM_EPS = 1e-6
LN_EPS = 1e-5
ALPHA = (2 * DEPTH) ** 0.25
LOG2E = math.log2(math.e)

LANES = 128
N_PROMPT = BATCH * SEQ
N_SAMPLE = DEC_BATCH * DEC_SEQ
N_TOK = N_PROMPT + N_SAMPLE
TM = 256
PROMPT_TILES = N_PROMPT // TM
TILES = N_TOK // TM
SAMPLE_TILES_PER_REQ = DEC_SEQ // TM
N_COND = 8
CAP_P = CAPACITY_FACTOR * N_PROMPT // N_EXPERTS
CAP_S = CAPACITY_FACTOR * N_SAMPLE // N_EXPERTS
CAP = CAP_P + CAP_S
FF_TILE = 512
KV_CHUNK = 512
MLA_TQ = 512

_NT = (((1,), (1,)), ((), ()))
_VMEM_LIMIT = 56 * 1024 * 1024


def _params(n_axes, vmem=_VMEM_LIMIT):
    return pltpu.CompilerParams(dimension_semantics=("arbitrary",) * n_axes, vmem_limit_bytes=vmem)


def _cond_of_tile(i):
    return jnp.where(i < PROMPT_TILES, 0, 1 + (i - PROMPT_TILES) // SAMPLE_TILES_PER_REQ)


def _pos_block_of_tile(i):
    return jnp.where(i < PROMPT_TILES, 0, 1 + (i - PROMPT_TILES) % SAMPLE_TILES_PER_REQ)


def _mod_spec(layer, k):
    return pl.BlockSpec((None, None, None, 1, D_MODEL), lambda i: (layer, k, _cond_of_tile(i), 0, 0))


def _row_spec(width):
    return pl.BlockSpec((TM, width), lambda i: (i, 0))


def _full_spec(shape):
    nd = len(shape)
    return pl.BlockSpec(shape, lambda *_: (0,) * nd)


def _split_bf16(a):
    hi = a.astype(BF16)
    lo = (a - hi.astype(F32)).astype(BF16)
    return hi, lo


def _dot3(a, b, dims=(((1,), (0,)), ((), ()))):
    a_hi, a_lo = _split_bf16(a)
    b_hi, b_lo = _split_bf16(b)
    dg = functools.partial(lax.dot_general, dimension_numbers=dims, preferred_element_type=F32)
    return dg(a_hi, b_hi) + dg(a_lo, b_hi) + dg(a_hi, b_lo)


def _layer_norm(y, g, b):
    mu = jnp.mean(y, axis=-1, keepdims=True)
    d = y - mu
    var = jnp.mean(d * d, axis=-1, keepdims=True)
    return d * lax.rsqrt(var + LN_EPS) * g + b


def _rope_lanes(t, cos, sin, half):
    lane = lax.broadcasted_iota(jnp.int32, t.shape, 1)
    first = (lane & (2 * half - 1)) < half
    partner = jnp.where(first, pltpu.roll(t, LANES - half, 1), pltpu.roll(t, half, 1))
    return t * cos + partner * sin


def _pair_rms(t, gain):
    sq = t * t
    lane = lax.broadcasted_iota(jnp.int32, t.shape, 1)
    lo = lane < HEAD_DIM
    s_lo = jnp.sum(jnp.where(lo, sq, 0.0), axis=-1, keepdims=True)
    s_hi = jnp.sum(jnp.where(lo, 0.0, sq), axis=-1, keepdims=True)
    ms = jnp.where(lo, s_lo, s_hi) * (1.0 / HEAD_DIM)
    return t * lax.rsqrt(ms + NORM_EPS) * gain


def _adaln_kernel(c_ref, w_ref, b_ref, o_ref):
    c = c_ref[...]
    a = c / (1.0 + jnp.exp(-c))
    o_ref[...] = _dot3(a, w_ref[...]) + b_ref[...]


def _adaln(cond, ada_w, ada_b):
    return pl.pallas_call(
        _adaln_kernel,
        out_shape=jax.ShapeDtypeStruct((DEPTH, 6, N_COND, D_MODEL), F32),
        grid=(DEPTH, 6),
        in_specs=[
            pl.BlockSpec((N_COND, D_MODEL), lambda l, k: (0, 0)),
            pl.BlockSpec((None, D_MODEL, D_MODEL), lambda l, k: (l, 0, k)),
            pl.BlockSpec((None, None, 1, D_MODEL), lambda l, k: (l, k, 0, 0)),
        ],
        out_specs=pl.BlockSpec((None, None, N_COND, D_MODEL), lambda l, k: (l, k, 0, 0)),
        compiler_params=_params(2),
        name="adaln",
    )(cond, ada_w, ada_b.reshape(DEPTH, 6, 1, D_MODEL))


QK_WIDTH = (N_HEADS + N_KV_HEADS) * HEAD_DIM
QKV_WIDTH = (N_HEADS + 2 * N_KV_HEADS) * HEAD_DIM
Q_WIDTH = N_HEADS * HEAD_DIM


def _gqa_proj_kernel(x_ref, sh_ref, sc_ref, w_ref, gain_ref, cos_ref, sin_ref,
                     q_ref, kh_ref, vh_ref, k32_ref, v32_ref, *, use_norm):
    h = (x_ref[...] * (1.0 + sc_ref[...]) + sh_ref[...]).astype(BF16)
    acc = jnp.dot(h, w_ref[...], preferred_element_type=F32)
    cos = cos_ref[...]
    sin = sin_ref[...]
    for j in range(QK_WIDTH // LANES):
        t = acc[:, j * LANES:(j + 1) * LANES]
        if use_norm:
            t = _pair_rms(t, gain_ref[:, j * LANES:(j + 1) * LANES])
        t = _rope_lanes(t, cos, sin, HEAD_DIM // 2)
        if j < Q_WIDTH // LANES:
            q_ref[:, j * LANES:(j + 1) * LANES] = (t * GQA_SCALE).astype(BF16)
        else:
            for half in range(2):
                kv_head = 2 * (j - Q_WIDTH // LANES) + half
                th = t[:, half * HEAD_DIM:(half + 1) * HEAD_DIM]
                kh_ref[kv_head] = th.astype(BF16)
                k32_ref[kv_head] = th
    for kv_head in range(N_KV_HEADS):
        lo = QK_WIDTH + kv_head * HEAD_DIM
        tv = acc[:, lo:lo + HEAD_DIM]
        vh_ref[kv_head] = tv.astype(BF16)
        v32_ref[kv_head] = tv


def _gqa_proj(x, mods5, layer, w_qkv, gain, cos_t, sin_t, use_norm):
    head_spec = pl.BlockSpec((N_KV_HEADS, TM, HEAD_DIM), lambda i: (0, i, 0))
    pos_spec = pl.BlockSpec((TM, LANES), lambda i: (_pos_block_of_tile(i), 0))
    return pl.pallas_call(
        functools.partial(_gqa_proj_kernel, use_norm=use_norm),
        out_shape=(
            jax.ShapeDtypeStruct((N_TOK, Q_WIDTH), BF16),
            jax.ShapeDtypeStruct((N_KV_HEADS, N_TOK, HEAD_DIM), BF16),
            jax.ShapeDtypeStruct((N_KV_HEADS, N_TOK, HEAD_DIM), BF16),
            jax.ShapeDtypeStruct((N_KV_HEADS, N_TOK, HEAD_DIM), F32),
            jax.ShapeDtypeStruct((N_KV_HEADS, N_TOK, HEAD_DIM), F32),
        ),
        grid=(TILES,),
        in_specs=[
            _row_spec(D_MODEL), _mod_spec(layer, 0), _mod_spec(layer, 1),
            _full_spec((D_MODEL, QKV_WIDTH)), _full_spec((1, QK_WIDTH)),
            pos_spec, pos_spec,
        ],
        out_specs=(_row_spec(Q_WIDTH), head_spec, head_spec, head_spec, head_spec),
        compiler_params=_params(1),
        name="gqa_proj",
    )(x, mods5, mods5, w_qkv, gain, cos_t, sin_t)


def _attend(qs, chunks, sink_col, c1):
    m = den = acc = None
    for k_fn, v_fn, bias, _ in chunks:
        s = lax.dot_general(qs, k_fn(), _NT, preferred_element_type=F32)
        if bias is not None:
            s = s + bias
        mc = jnp.max(s, axis=-1, keepdims=True)
        m_new = mc if m is None else jnp.maximum(m, mc)
        p = jnp.exp2((s - m_new) * c1)
        ps = jnp.sum(p, axis=-1, keepdims=True)
        pv = jnp.dot(p.astype(BF16), v_fn(), preferred_element_type=F32)
        if m is None:
            den, acc = ps, pv
        else:
            alpha = jnp.exp2((m - m_new) * c1)
            den = den * alpha + ps
            acc = acc * alpha + pv
        m = m_new
    if sink_col is not None:
        den = den + jnp.exp2((sink_col - m) * c1)
    return acc / den


def _chunk_list(k_ref, v_ref, length, cast):
    out = []
    width = min(KV_CHUNK, length)
    for c0 in range(0, length, width):
        if cast:
            k_fn = lambda c0=c0: k_ref[c0:c0 + width, :].astype(BF16)
            v_fn = lambda c0=c0: v_ref[c0:c0 + width, :].astype(BF16)
        else:
            k_fn = lambda c0=c0: k_ref[c0:c0 + width, :]
            v_fn = lambda c0=c0: v_ref[c0:c0 + width, :]
        out.append((k_fn, v_fn, None, width))
    return out


def _gqa_attn_kernel(*refs, tq, n_keys, has_ctx, use_sink, window):
    refs = list(refs)
    q_ref, k_ref, v_ref = refs[:3]
    pos = 3
    if has_ctx:
        kc_ref, vc_ref = refs[pos:pos + 2]
        pos += 2
    if use_sink:
        sink_ref = refs[pos]
        pos += 1
    o_ref = refs[pos]
    g = pl.program_id(1)
    qi = pl.program_id(2)

    qf = q_ref[...].astype(F32)
    qs = jnp.concatenate(
        [qf[:, j * HEAD_DIM:(j + 1) * HEAD_DIM] for j in range(GROUP)], axis=0).astype(BF16)

    if window:
        span = tq + 2 * WINDOW
        kstart = pl.multiple_of(jnp.clip(qi * tq - WINDOW, 0, n_keys - span), LANES)
        qpos = qi * tq + lax.broadcasted_iota(jnp.int32, (tq, span), 0)
        kpos = kstart + lax.broadcasted_iota(jnp.int32, (tq, span), 1)
        band = jnp.where(jnp.abs(kpos - qpos) <= WINDOW, 0.0, -jnp.inf).astype(F32)
        bias = jnp.concatenate([band] * GROUP, axis=0)
        chunks = [(lambda: k_ref[pl.ds(kstart, span), :], lambda: v_ref[pl.ds(kstart, span), :], bias, span)]
    else:
        chunks = _chunk_list(k_ref, v_ref, n_keys, cast=False)
    if has_ctx:
        chunks += _chunk_list(kc_ref, vc_ref, PAST_LEN, cast=True)

    sink_col = None
    if use_sink:
        row = lax.broadcasted_iota(jnp.int32, (GROUP * tq, 1), 0)
        sink_col = jnp.full((GROUP * tq, 1), sink_ref[g * GROUP + GROUP - 1], F32)
        for j in reversed(range(GROUP - 1)):
            sink_col = jnp.where(row < (j + 1) * tq, sink_ref[g * GROUP + j], sink_col)

    o = _attend(qs, chunks, sink_col, LOG2E)
    o_ref[...] = jnp.concatenate([o[j * tq:(j + 1) * tq, :] for j in range(GROUP)], axis=-1).astype(BF16)


def _gqa_attention(q, kh, vh, ctx_k, ctx_v, slot, sink, *, latent, window):
    tq = TM
    if latent:
        n_b, n_keys, n_qt = DEC_BATCH, DEC_SEQ, DEC_SEQ // tq
        row0 = N_PROMPT // tq
        kv_blk0 = N_PROMPT // DEC_SEQ
    else:
        n_b, n_keys, n_qt = BATCH, SEQ, 1
        row0 = 0
        kv_blk0 = 0
    q_spec = pl.BlockSpec((tq, GROUP * HEAD_DIM), lambda b, g, t: (row0 + b * n_qt + t, g))
    kv_spec = pl.BlockSpec((None, n_keys, HEAD_DIM), lambda b, g, t: (g, kv_blk0 + b, 0))
    in_specs = [q_spec, kv_spec, kv_spec]
    args = [q, kh, vh]
    if latent:
        ctx_spec = pl.BlockSpec((None, None, None, PAST_LEN, HEAD_DIM), lambda b, g, t: (b, slot, g, 0, 0))
        in_specs += [ctx_spec, ctx_spec]
        args += [ctx_k, ctx_v]
    use_sink = sink is not None
    if use_sink:
        in_specs.append(pl.BlockSpec(memory_space=pltpu.SMEM))
        args.append(sink)
    return pl.pallas_call(
        functools.partial(_gqa_attn_kernel, tq=tq, n_keys=n_keys, has_ctx=latent,
                          use_sink=use_sink, window=window),
        out_shape=jax.ShapeDtypeStruct((n_b * n_keys, Q_WIDTH), BF16),
        grid=(n_b, N_KV_HEADS, n_qt),
        in_specs=in_specs,
        out_specs=pl.BlockSpec((tq, GROUP * HEAD_DIM), lambda b, g, t: (b * n_qt + t, g)),
        compiler_params=_params(3),
        name="gqa_attn_latent" if latent else "gqa_attn_context",
    )(*args)


MLA_DOWN_EXT = Q_RANK + KV_RANK + LANES
MLA_HEAD_LANES = LANES
MLA_ROPE_LO = MLA_NOPE_DIM


def _mla_proj_kernel(x_ref, sh_ref, sc_ref, wd_ref, qg_ref, wuq_ref, kvg_ref, cos_ref, sin_ref,
                     q_ref, ckv_ref, kpe_ref):
    h = (x_ref[...] * (1.0 + sc_ref[...]) + sh_ref[...]).astype(BF16)
    acc = jnp.dot(h, wd_ref[...], preferred_element_type=F32)
    cos = cos_ref[...]
    sin = sin_ref[...]
    qa = acc[:, :Q_RANK]
    qn = qa * lax.rsqrt(jnp.mean(qa * qa, axis=-1, keepdims=True) + NORM_EPS) * qg_ref[...]
    q = jnp.dot(qn.astype(BF16), wuq_ref[...], preferred_element_type=F32)
    for hd in range(MLA_HEADS):
        t = q[:, hd * LANES:(hd + 1) * LANES]
        q_ref[:, hd * LANES:(hd + 1) * LANES] = _rope_lanes(t, cos, sin, MLA_ROPE_DIM // 2).astype(BF16)
    ckv = acc[:, Q_RANK:Q_RANK + KV_RANK]
    ckv_ref[...] = ckv * lax.rsqrt(jnp.mean(ckv * ckv, axis=-1, keepdims=True) + NORM_EPS) * kvg_ref[...]
    kpe_ref[...] = _rope_lanes(acc[:, Q_RANK + KV_RANK:], cos, sin, MLA_ROPE_DIM // 2)


def _mla_proj(x, mods5, layer, wd_ext, q_gain, wuq_pad, kv_gain, cos_t, sin_t):
    pos_spec = pl.BlockSpec((TM, LANES), lambda i: (_pos_block_of_tile(i), 0))
    return pl.pallas_call(
        _mla_proj_kernel,
        out_shape=(
            jax.ShapeDtypeStruct((N_TOK, MLA_HEADS * LANES), BF16),
            jax.ShapeDtypeStruct((N_TOK, KV_RANK), F32),
            jax.ShapeDtypeStruct((N_TOK, LANES), F32),
        ),
        grid=(TILES,),
        in_specs=[
            _row_spec(D_MODEL), _mod_spec(layer, 0), _mod_spec(layer, 1),
            _full_spec((D_MODEL, MLA_DOWN_EXT)), _full_spec((1, Q_RANK)),
            _full_spec((Q_RANK, MLA_HEADS * LANES)), _full_spec((1, KV_RANK)),
            pos_spec, pos_spec,
        ],
        out_specs=(_row_spec(MLA_HEADS * LANES), _row_spec(KV_RANK), _row_spec(LANES)),
        compiler_params=_params(1),
        name="mla_proj",
    )(x, mods5, mods5, wd_ext, q_gain, wuq_pad, kv_gain, cos_t, sin_t)


def _mla_kv_kernel(c_ref, kpe_ref, wk_ref, wv_ref, k_ref, v_ref):
    c = c_ref[...].astype(BF16)
    kk = jnp.dot(c, wk_ref[...], preferred_element_type=F32)
    vv = jnp.dot(c, wv_ref[...], preferred_element_type=F32)
    kpe = kpe_ref[...]
    for hd in range(MLA_HEADS):
        k_ref[hd] = (kk[:, hd * LANES:(hd + 1) * LANES] + kpe).astype(BF16)
        v_ref[hd] = vv[:, hd * MLA_V_DIM:(hd + 1) * MLA_V_DIM].astype(BF16)


def _mla_kv(c_all, kpe_all, wk_pad, wv):
    n_rows = c_all.shape[0]
    return pl.pallas_call(
        _mla_kv_kernel,
        out_shape=(
            jax.ShapeDtypeStruct((MLA_HEADS, n_rows, LANES), BF16),
            jax.ShapeDtypeStruct((MLA_HEADS, n_rows, MLA_V_DIM), BF16),
        ),
        grid=(n_rows // TM,),
        in_specs=[
            _row_spec(KV_RANK), _row_spec(LANES),
            _full_spec((KV_RANK, MLA_HEADS * LANES)), _full_spec((KV_RANK, MLA_HEADS * MLA_V_DIM)),
        ],
        out_specs=(
            pl.BlockSpec((MLA_HEADS, TM, LANES), lambda i: (0, i, 0)),
            pl.BlockSpec((MLA_HEADS, TM, MLA_V_DIM), lambda i: (0, i, 0)),
        ),
        compiler_params=_params(1),
        name="mla_kv",
    )(c_all, kpe_all, wk_pad, wv)


def _mla_attn_kernel(*refs, tq, n_keys, has_ctx):
    refs = list(refs)
    q_ref, k_ref, v_ref = refs[:3]
    pos = 3
    if has_ctx:
        kc_ref, vc_ref = refs[pos:pos + 2]
        pos += 2
    o_ref = refs[pos]
    outs = []
    for j in range(2):
        qs = q_ref[:, j * LANES:(j + 1) * LANES]
        chunks = _chunk_list(k_ref.at[j], v_ref.at[j], n_keys, cast=False)
        if has_ctx:
            chunks += _chunk_list(kc_ref.at[j], vc_ref.at[j], PAST_LEN, cast=False)
        outs.append(_attend(qs, chunks, None, MLA_SCALE * LOG2E))
    o_ref[...] = jnp.concatenate(outs, axis=-1).astype(BF16)


def _mla_attention(q, k_all, v_all, *, latent):
    if latent:
        tq = MLA_TQ
        n_b, n_keys, n_qt = DEC_BATCH, DEC_SEQ, DEC_SEQ // tq
        row0 = N_PROMPT // tq
        kv_blk0 = N_PROMPT // DEC_SEQ
    else:
        tq = TM
        n_b, n_keys, n_qt = BATCH, SEQ, 1
        row0 = 0
        kv_blk0 = 0
    q_spec = pl.BlockSpec((tq, 2 * LANES), lambda b, hp, t: (row0 + b * n_qt + t, hp))
    k_spec = pl.BlockSpec((2, n_keys, LANES), lambda b, hp, t: (hp, kv_blk0 + b, 0))
    v_spec = pl.BlockSpec((2, n_keys, MLA_V_DIM), lambda b, hp, t: (hp, kv_blk0 + b, 0))
    in_specs = [q_spec, k_spec, v_spec]
    args = [q, k_all, v_all]
    if latent:
        ctx0 = N_TOK // PAST_LEN
        in_specs += [
            pl.BlockSpec((2, PAST_LEN, LANES), lambda b, hp, t: (hp, ctx0 + b, 0)),
            pl.BlockSpec((2, PAST_LEN, MLA_V_DIM), lambda b, hp, t: (hp, ctx0 + b, 0)),
        ]
        args += [k_all, v_all]
    return pl.pallas_call(
        functools.partial(_mla_attn_kernel, tq=tq, n_keys=n_keys, has_ctx=latent),
        out_shape=jax.ShapeDtypeStruct((n_b * n_keys, MLA_HEADS * MLA_V_DIM), BF16),
        grid=(n_b, MLA_HEADS // 2, n_qt),
        in_specs=in_specs,
        out_specs=pl.BlockSpec((tq, 2 * MLA_V_DIM), lambda b, hp, t: (b * n_qt + t, hp)),
        compiler_params=_params(3),
        name="mla_attn_latent" if latent else "mla_attn_context",
    )(*args)


def _post_attn_kernel(o_ref, x_ref, wo_ref, g1_ref, sh2_ref, sc2_ref, lng_ref, lnb_ref, wr_ref,
                      x1_ref, h2_ref, aff_ref):
    proj = jnp.dot(o_ref[...], wo_ref[...], preferred_element_type=F32)
    x1 = _layer_norm(ALPHA * x_ref[...] + g1_ref[...] * proj, lng_ref[...], lnb_ref[...])
    x1_ref[...] = x1
    h2 = x1 * (1.0 + sc2_ref[...]) + sh2_ref[...]
    h2_ref[...] = h2.astype(BF16)
    logits_t = _dot3(wr_ref[...], h2, _NT)
    e = jnp.exp(logits_t - jnp.max(logits_t, axis=0, keepdims=True))
    aff_ref[...] = e / jnp.sum(e, axis=0, keepdims=True)


def _post_attn(o, x, w_o, mods5, layer, ln_g, ln_b, w_router_t):
    return pl.pallas_call(
        _post_attn_kernel,
        out_shape=(
            jax.ShapeDtypeStruct((N_TOK, D_MODEL), F32),
            jax.ShapeDtypeStruct((N_TOK, D_MODEL), BF16),
            jax.ShapeDtypeStruct((N_EXPERTS, N_TOK), F32),
        ),
        grid=(TILES,),
        in_specs=[
            _row_spec(D_MODEL), _row_spec(D_MODEL), _full_spec((D_MODEL, D_MODEL)),
            _mod_spec(layer, 2), _mod_spec(layer, 3), _mod_spec(layer, 4),
            _full_spec((1, D_MODEL)), _full_spec((1, D_MODEL)), _full_spec((N_EXPERTS, D_MODEL)),
        ],
        out_specs=(_row_spec(D_MODEL), _row_spec(D_MODEL), pl.BlockSpec((N_EXPERTS, TM), lambda i: (0, i))),
        compiler_params=_params(1),
        name="post_attn",
    )(o, x, w_o, mods5, mods5, mods5, ln_g, ln_b, w_router_t)


FFN_ROWS = 512


def _ffn_kernel(x_ref, wg_ref, wu_ref, wd_ref, gate_ref, o_ref):
    f = pl.program_id(1)
    wg = wg_ref[...].astype(BF16)
    wu = wu_ref[...].astype(BF16)
    wd = wd_ref[...].astype(BF16)

    @pl.when(f == 0)
    def _():
        o_ref[...] = jnp.zeros_like(o_ref)

    for r0 in range(0, CAP, FFN_ROWS):
        x = x_ref[r0:r0 + FFN_ROWS, :]
        a = jnp.dot(x, wg, preferred_element_type=F32)
        u = jnp.dot(x, wu, preferred_element_type=F32)
        hid = (a / (1.0 + jnp.exp(-a)) * u).astype(BF16)
        o_ref[r0:r0 + FFN_ROWS, :] += jnp.dot(hid, wd, preferred_element_type=F32)

    @pl.when(f == EXPERT_FF // FF_TILE - 1)
    def _():
        o_ref[...] = o_ref[...] * gate_ref[...]


def _expert_ffn(xe, gate, w_gate, w_up, w_down, layer):
    return pl.pallas_call(
        _ffn_kernel,
        out_shape=jax.ShapeDtypeStruct((N_EXPERTS, CAP, D_MODEL), F32),
        grid=(N_EXPERTS, EXPERT_FF // FF_TILE),
        in_specs=[
            pl.BlockSpec((None, CAP, D_MODEL), lambda e, f: (e, 0, 0)),
            pl.BlockSpec((None, None, D_MODEL, FF_TILE), lambda e, f: (layer, e, 0, f)),
            pl.BlockSpec((None, None, D_MODEL, FF_TILE), lambda e, f: (layer, e, 0, f)),
            pl.BlockSpec((None, None, FF_TILE, D_MODEL), lambda e, f: (layer, e, f, 0)),
            pl.BlockSpec((None, CAP, 1), lambda e, f: (e, 0, 0)),
        ],
        out_specs=pl.BlockSpec((None, CAP, D_MODEL), lambda e, f: (e, 0, 0)),
        compiler_params=_params(2),
        name="expert_ffn",
    )(xe, w_gate, w_up, w_down, gate)


def _post_moe_kernel(x_ref, f_ref, g2_ref, lng_ref, lnb_ref, o_ref):
    o_ref[...] = _layer_norm(ALPHA * x_ref[...] + g2_ref[...] * f_ref[...], lng_ref[...], lnb_ref[...])


def _post_moe(x1, f, mods5, layer, ln_g, ln_b):
    return pl.pallas_call(
        _post_moe_kernel,
        out_shape=jax.ShapeDtypeStruct((N_TOK, D_MODEL), F32),
        grid=(TILES,),
        in_specs=[_row_spec(D_MODEL), _row_spec(D_MODEL), _mod_spec(layer, 5),
                  _full_spec((1, D_MODEL)), _full_spec((1, D_MODEL))],
        out_specs=_row_spec(D_MODEL),
        compiler_params=_params(1),
        name="post_moe",
    )(x1, f, mods5, ln_g, ln_b)


def _axial_tables(dim):
    n_rows = DEC_SEQ // GRID_W
    rows = jnp.repeat(jnp.arange(n_rows, dtype=F32), GRID_W)
    cols = jnp.tile(jnp.arange(GRID_W, dtype=F32), n_rows)
    n_freq = dim // 4
    inv_freq = ROPE_THETA ** (-jnp.arange(n_freq, dtype=F32) / n_freq)
    ang = jnp.concatenate([rows[:, None] * inv_freq, cols[:, None] * inv_freq], -1)
    cos, sin = jnp.cos(ang), jnp.sin(ang)
    return jnp.concatenate([cos, cos], -1), jnp.concatenate([-sin, sin], -1)


def _with_identity_rows(cos_l, sin_l):
    return (jnp.concatenate([jnp.ones((TM, LANES), F32), cos_l], 0),
            jnp.concatenate([jnp.zeros((TM, LANES), F32), sin_l], 0))


def _gqa_rope_tables():
    cos, sin = _axial_tables(HEAD_DIM)
    return _with_identity_rows(jnp.tile(cos, (1, 2)), jnp.tile(sin, (1, 2)))


def _mla_rope_tables():
    cos, sin = _axial_tables(MLA_ROPE_DIM)
    pad_hi = LANES - MLA_ROPE_LO - MLA_ROPE_DIM
    cos_l = jnp.concatenate([jnp.ones((DEC_SEQ, MLA_ROPE_LO), F32), cos, jnp.ones((DEC_SEQ, pad_hi), F32)], 1)
    sin_l = jnp.pad(sin, ((0, 0), (MLA_ROPE_LO, pad_hi)))
    return _with_identity_rows(cos_l, sin_l)


def _moe(h2, aff_t, w_gate, w_up, w_down, layer):
    xs, gates, idxs = [], [], []
    for lo, n, cap in ((0, N_PROMPT, CAP_P), (N_PROMPT, N_SAMPLE, CAP_S)):
        gate, idx = lax.top_k(aff_t[:, lo:lo + n], cap)
        xs.append(jnp.take(h2[lo:lo + n], idx, axis=0))
        gates.append(gate)
        idxs.append(idx)
    xe = jnp.concatenate(xs, axis=1)
    gate = jnp.concatenate(gates, axis=1)[..., None]
    ye = _expert_ffn(xe, gate, w_gate, w_up, w_down, layer)
    outs = []
    for (lo, n, cap), idx, c0 in zip(((0, N_PROMPT, CAP_P), (N_PROMPT, N_SAMPLE, CAP_S)), idxs, (0, CAP_P)):
        y = ye[:, c0:c0 + cap].reshape(-1, D_MODEL)
        outs.append(jnp.zeros((n, D_MODEL), F32).at[idx.reshape(-1)].add(y))
    return jnp.concatenate(outs, axis=0)


def kernel(x_prompt, x_sample, cache_gqa_k, cache_gqa_v, cache_mla_ckv, cache_mla_kpe, c, c_ctx,
           ada_w, ada_b, ln_g, ln_b, a_w_qkv, a_q_norm, a_k_norm, a_w_o, b_w_qkv, b_sink, b_w_o,
           mla_w_down, mla_q_norm, mla_w_uq, mla_kv_norm, mla_w_ukv, mla_w_o,
           moe_w_router, moe_w_gate, moe_w_up, moe_w_down):
    x = jnp.concatenate([x_prompt.reshape(N_PROMPT, D_MODEL), x_sample.reshape(N_SAMPLE, D_MODEL)], 0)
    cond = jnp.concatenate([c_ctx[None], c, jnp.zeros((N_COND - 1 - DEC_BATCH, D_MODEL), F32)], 0)
    mods5 = _adaln(cond, ada_w, ada_b).reshape(DEPTH, 6, N_COND, 1, D_MODEL)
    gqa_cos, gqa_sin = _gqa_rope_tables()
    mla_cos, mla_sin = _mla_rope_tables()

    new_k, new_v, new_ckv, new_kpe = [], [], [], []
    gqa_slot = 0
    for i in range(DEPTH):
        kind, j = i % 3, i // 3
        if kind in (0, 1):
            if kind == 0:
                w_qkv, w_o, sink = a_w_qkv[j], a_w_o[j], None
                gain = jnp.concatenate([jnp.tile(a_q_norm[j], N_HEADS), jnp.tile(a_k_norm[j], N_KV_HEADS)])[None]
            else:
                w_qkv, w_o, sink = b_w_qkv[j], b_w_o[j], b_sink[j]
                gain = jnp.ones((1, QK_WIDTH), F32)
            q, kh, vh, k32, v32 = _gqa_proj(x, mods5, i, w_qkv.astype(BF16), gain, gqa_cos, gqa_sin,
                                            use_norm=(kind == 0))
            o_p = _gqa_attention(q, kh, vh, None, None, gqa_slot, sink, latent=False, window=False)
            o_s = _gqa_attention(q, kh, vh, cache_gqa_k, cache_gqa_v, gqa_slot, sink,
                                 latent=True, window=(kind == 1))
            for store, arr in ((new_k, k32), (new_v, v32)):
                store.append(arr[:, :N_PROMPT].reshape(N_KV_HEADS, BATCH, SEQ, HEAD_DIM).transpose(1, 0, 2, 3))
            gqa_slot += 1
        else:
            w_down = mla_w_down[j]
            rope_lo = Q_RANK + KV_RANK
            wd_ext = jnp.concatenate([
                w_down[:, :rope_lo], jnp.zeros((D_MODEL, MLA_ROPE_LO), F32), w_down[:, rope_lo:],
                jnp.zeros((D_MODEL, LANES - MLA_ROPE_LO - MLA_ROPE_DIM), F32)], 1).astype(BF16)
            wuq_pad = jnp.pad(mla_w_uq[j].reshape(Q_RANK, MLA_HEADS, MLA_QK_DIM),
                              ((0, 0), (0, 0), (0, LANES - MLA_QK_DIM))).reshape(Q_RANK, MLA_HEADS * LANES)
            w_ukv = mla_w_ukv[j].reshape(KV_RANK, MLA_HEADS, MLA_NOPE_DIM + MLA_V_DIM)
            wk_pad = jnp.pad(w_ukv[:, :, :MLA_NOPE_DIM], ((0, 0), (0, 0), (0, LANES - MLA_NOPE_DIM)))
            wk_pad = wk_pad.reshape(KV_RANK, MLA_HEADS * LANES).astype(BF16)
            wv = w_ukv[:, :, MLA_NOPE_DIM:].reshape(KV_RANK, MLA_HEADS * MLA_V_DIM).astype(BF16)
            q, ckv, kpe = _mla_proj(x, mods5, i, wd_ext, mla_q_norm[j][None], wuq_pad.astype(BF16),
                                    mla_kv_norm[j][None], mla_cos, mla_sin)
            c_all = jnp.concatenate([ckv, cache_mla_ckv[:, j].reshape(DEC_BATCH * PAST_LEN, KV_RANK)], 0)
            kpe_ctx = jnp.pad(cache_mla_kpe[:, j].reshape(DEC_BATCH * PAST_LEN, MLA_ROPE_DIM),
                              ((0, 0), (MLA_ROPE_LO, LANES - MLA_ROPE_LO - MLA_ROPE_DIM)))
            kpe_all = jnp.concatenate([kpe, kpe_ctx], 0)
            k_all, v_all = _mla_kv(c_all, kpe_all, wk_pad, wv)
            o_p = _mla_attention(q, k_all, v_all, latent=False)
            o_s = _mla_attention(q, k_all, v_all, latent=True)
            w_o = mla_w_o[j]
            new_ckv.append(ckv[:N_PROMPT].reshape(BATCH, SEQ, KV_RANK))
            new_kpe.append(kpe[:N_PROMPT, MLA_ROPE_LO:MLA_ROPE_LO + MLA_ROPE_DIM].reshape(BATCH, SEQ, MLA_ROPE_DIM))
        o = jnp.concatenate([o_p, o_s], 0)
        x1, h2, aff_t = _post_attn(o, x, w_o.astype(BF16), mods5, i, ln_g[i, 0][None], ln_b[i, 0][None],
                                   moe_w_router[i].T)
        f = _moe(h2, aff_t, moe_w_gate, moe_w_up, moe_w_down, i)
        x = _post_moe(x1, f, mods5, i, ln_g[i, 1][None], ln_b[i, 1][None])

    y_prompt = x[:N_PROMPT].reshape(BATCH, SEQ, D_MODEL)
    y_sample = x[N_PROMPT:].reshape(DEC_BATCH, DEC_SEQ, D_MODEL)
    return (y_prompt, y_sample, jnp.stack(new_k, 1), jnp.stack(new_v, 1),
            jnp.stack(new_ckv, 1), jnp.stack(new_kpe, 1))
```

```python
import functools
import math

import jax
import jax.numpy as jnp
from jax import lax
from jax.experimental import pallas as pl
from jax.experimental.pallas import tpu as pltpu

F32 = jnp.float32
BF16 = jnp.bfloat16

D_MODEL = 1024
BATCH = 16
SEQ = 256
DEPTH = 4
DEC_BATCH = 4
DEC_SEQ = 2048
PAST_LEN = 512
GRID_W = 64
N_HEADS = 16
N_KV_HEADS = 4
GROUP = N_HEADS // N_KV_HEADS
HEAD_DIM = 64
WINDOW = 128
MLA_HEADS = 16
Q_RANK = 256
KV_RANK = 128
MLA_NOPE_DIM = 64
MLA_ROPE_DIM = 32
MLA_V_DIM = 64
MLA_QK_DIM = MLA_NOPE_DIM + MLA_ROPE_DIM
MLA_SCALE = MLA_QK_DIM ** -0.5
GQA_SCALE = HEAD_DIM ** -0.5
N_EXPERTS = 16
EXPERT_FF = 2048
CAPACITY_FACTOR = 2
ROPE_THETA = 10000.0
NORM_EPS = 1e-6
LN_EPS = 1e-5
ALPHA = (2 * DEPTH) ** 0.25
LOG2E = math.log2(math.e)

LANES = 128
N_PROMPT = BATCH * SEQ
N_SAMPLE = DEC_BATCH * DEC_SEQ
N_TOK = N_PROMPT + N_SAMPLE
TM = 256
PROMPT_TILES = N_PROMPT // TM
TILES = N_TOK // TM
SAMPLE_TILES_PER_REQ = DEC_SEQ // TM
N_COND = 8
CAP_P = CAPACITY_FACTOR * N_PROMPT // N_EXPERTS
CAP_S = CAPACITY_FACTOR * N_SAMPLE // N_EXPERTS
CAP = CAP_P + CAP_S
FF_TILE = 512
KV_CHUNK = 512
MLA_TQ = 512

_NT = (((1,), (1,)), ((), ()))
_VMEM_LIMIT = 56 * 1024 * 1024


def _params(n_axes, vmem=_VMEM_LIMIT):
    return pltpu.CompilerParams(dimension_semantics=("arbitrary",) * n_axes, vmem_limit_bytes=vmem)


def _cond_of_tile(i):
    return jnp.where(i < PROMPT_TILES, 0, 1 + (i - PROMPT_TILES) // SAMPLE_TILES_PER_REQ)


def _pos_block_of_tile(i):
    return jnp.where(i < PROMPT_TILES, 0, 1 + (i - PROMPT_TILES) % SAMPLE_TILES_PER_REQ)


def _mod_spec(layer, k):
    return pl.BlockSpec((None, None, None, 1, D_MODEL), lambda i: (layer, k, _cond_of_tile(i), 0, 0))


def _row_spec(width):
    return pl.BlockSpec((TM, width), lambda i: (i, 0))


def _full_spec(shape):
    nd = len(shape)
    return pl.BlockSpec(shape, lambda *_: (0,) * nd)


def _split_bf16(a):
    hi = a.astype(BF16)
    lo = (a - hi.astype(F32)).astype(BF16)
    return hi, lo


def _dot3(a, b, dims=(((1,), (0,)), ((), ()))):
    a_hi, a_lo = _split_bf16(a)
    b_hi, b_lo = _split_bf16(b)
    dg = functools.partial(lax.dot_general, dimension_numbers=dims, preferred_element_type=F32)
    return dg(a_hi, b_hi) + dg(a_lo, b_hi) + dg(a_hi, b_lo)


def _layer_norm(y, g, b):
    mu = jnp.mean(y, axis=-1, keepdims=True)
    d = y - mu
    var = jnp.mean(d * d, axis=-1, keepdims=True)
    return d * lax.rsqrt(var + LN_EPS) * g + b


def _rope_lanes(t, cos, sin, half):
    lane = lax.broadcasted_iota(jnp.int32, t.shape, 1)
    first = (lane & (2 * half - 1)) < half
    partner = jnp.where(first, pltpu.roll(t, LANES - half, 1), pltpu.roll(t, half, 1))
    return t * cos + partner * sin


def _pair_rms(t, gain):
    sq = t * t
    lane = lax.broadcasted_iota(jnp.int32, t.shape, 1)
    lo = lane < HEAD_DIM
    s_lo = jnp.sum(jnp.where(lo, sq, 0.0), axis=-1, keepdims=True)
    s_hi = jnp.sum(jnp.where(lo, 0.0, sq), axis=-1, keepdims=True)
    ms = jnp.where(lo, s_lo, s_hi) * (1.0 / HEAD_DIM)
    return t * lax.rsqrt(ms + NORM_EPS) * gain


def _adaln_kernel(c_ref, w_ref, b_ref, o_ref):
    c = c_ref[...]
    a = c / (1.0 + jnp.exp(-c))
    o_ref[...] = _dot3(a, w_ref[...]) + b_ref[...]


def _adaln(cond, ada_w, ada_b):
    return pl.pallas_call(
        _adaln_kernel,
        out_shape=jax.ShapeDtypeStruct((DEPTH, 6, N_COND, D_MODEL), F32),
        grid=(DEPTH, 6),
        in_specs=[
            pl.BlockSpec((N_COND, D_MODEL), lambda l, k: (0, 0)),
            pl.BlockSpec((None, D_MODEL, D_MODEL), lambda l, k: (l, 0, k)),
            pl.BlockSpec((None, None, 1, D_MODEL), lambda l, k: (l, k, 0, 0)),
        ],
        out_specs=pl.BlockSpec((None, None, N_COND, D_MODEL), lambda l, k: (l, k, 0, 0)),
        compiler_params=_params(2),
        name="adaln",
    )(cond, ada_w, ada_b.reshape(DEPTH, 6, 1, D_MODEL))


QK_WIDTH = (N_HEADS + N_KV_HEADS) * HEAD_DIM
QKV_WIDTH = (N_HEADS + 2 * N_KV_HEADS) * HEAD_DIM
Q_WIDTH = N_HEADS * HEAD_DIM


def _gqa_proj_kernel(x_ref, sh_ref, sc_ref, w_ref, gain_ref, cos_ref, sin_ref,
                     q_ref, kh_ref, vh_ref, k32_ref, v32_ref, *, use_norm):
    h = (x_ref[...] * (1.0 + sc_ref[...]) + sh_ref[...]).astype(BF16)
    acc = jnp.dot(h, w_ref[...], preferred_element_type=F32)
    cos = cos_ref[...]
    sin = sin_ref[...]
    for j in range(QK_WIDTH // LANES):
        t = acc[:, j * LANES:(j + 1) * LANES]
        if use_norm:
            t = _pair_rms(t, gain_ref[:, j * LANES:(j + 1) * LANES])
        t = _rope_lanes(t, cos, sin, HEAD_DIM // 2)
        if j < Q_WIDTH // LANES:
            q_ref[:, j * LANES:(j + 1) * LANES] = (t * GQA_SCALE).astype(BF16)
        else:
            for half in range(2):
                kv_head = 2 * (j - Q_WIDTH // LANES) + half
                th = t[:, half * HEAD_DIM:(half + 1) * HEAD_DIM]
                kh_ref[kv_head] = th.astype(BF16)
                k32_ref[kv_head] = th
    for kv_head in range(N_KV_HEADS):
        lo = QK_WIDTH + kv_head * HEAD_DIM
        tv = acc[:, lo:lo + HEAD_DIM]
        vh_ref[kv_head] = tv.astype(BF16)
        v32_ref[kv_head] = tv


def _gqa_proj(x, mods5, layer, w_qkv, gain, cos_t, sin_t, use_norm):
    head_spec = pl.BlockSpec((N_KV_HEADS, TM, HEAD_DIM), lambda i: (0, i, 0))
    pos_spec = pl.BlockSpec((TM, LANES), lambda i: (_pos_block_of_tile(i), 0))
    return pl.pallas_call(
        functools.partial(_gqa_proj_kernel, use_norm=use_norm),
        out_shape=(
            jax.ShapeDtypeStruct((N_TOK, Q_WIDTH), BF16),
            jax.ShapeDtypeStruct((N_KV_HEADS, N_TOK, HEAD_DIM), BF16),
            jax.ShapeDtypeStruct((N_KV_HEADS, N_TOK, HEAD_DIM), BF16),
            jax.ShapeDtypeStruct((N_KV_HEADS, N_TOK, HEAD_DIM), F32),
            jax.ShapeDtypeStruct((N_KV_HEADS, N_TOK, HEAD_DIM), F32),
        ),
        grid=(TILES,),
        in_specs=[
            _row_spec(D_MODEL), _mod_spec(layer, 0), _mod_spec(layer, 1),
            _full_spec((D_MODEL, QKV_WIDTH)), _full_spec((1, QK_WIDTH)),
            pos_spec, pos_spec,
        ],
        out_specs=(_row_spec(Q_WIDTH), head_spec, head_spec, head_spec, head_spec),
        compiler_params=_params(1),
        name="gqa_proj",
    )(x, mods5, mods5, w_qkv, gain, cos_t, sin_t)


def _attend(qs, chunks, sink_col, c1):
    m = den = acc = None
    for k_fn, v_fn, bias, _ in chunks:
        s = lax.dot_general(qs, k_fn(), _NT, preferred_element_type=F32)
        if bias is not None:
            s = s + bias
        mc = jnp.max(s, axis=-1, keepdims=True)
        m_new = mc if m is None else jnp.maximum(m, mc)
        p = jnp.exp2((s - m_new) * c1)
        ps = jnp.sum(p, axis=-1, keepdims=True)
        pv = jnp.dot(p.astype(BF16), v_fn(), preferred_element_type=F32)
        if m is None:
            den, acc = ps, pv
        else:
            alpha = jnp.exp2((m - m_new) * c1)
            den = den * alpha + ps
            acc = acc * alpha + pv
        m = m_new
    if sink_col is not None:
        den = den + jnp.exp2((sink_col - m) * c1)
    return acc / den


def _chunk_list(k_ref, v_ref, length, cast):
    out = []
    width = min(KV_CHUNK, length)
    for c0 in range(0, length, width):
        if cast:
            k_fn = lambda c0=c0: k_ref[c0:c0 + width, :].astype(BF16)
            v_fn = lambda c0=c0: v_ref[c0:c0 + width, :].astype(BF16)
        else:
            k_fn = lambda c0=c0: k_ref[c0:c0 + width, :]
            v_fn = lambda c0=c0: v_ref[c0:c0 + width, :]
        out.append((k_fn, v_fn, None, width))
    return out


def _gqa_attn_kernel(*refs, tq, n_keys, has_ctx, use_sink, window):
    refs = list(refs)
    q_ref, k_ref, v_ref = refs[:3]
    pos = 3
    if has_ctx:
        kc_ref, vc_ref = refs[pos:pos + 2]
        pos += 2
    if use_sink:
        sink_ref = refs[pos]
        pos += 1
    o_ref = refs[pos]
    g = pl.program_id(1)
    qi = pl.program_id(2)

    qf = q_ref[...].astype(F32)
    qs = jnp.concatenate(
        [qf[:, j * HEAD_DIM:(j + 1) * HEAD_DIM] for j in range(GROUP)], axis=0).astype(BF16)

    if window:
        span = tq + 2 * WINDOW
        kstart = pl.multiple_of(jnp.clip(qi * tq - WINDOW, 0, n_keys - span), LANES)
        qpos = qi * tq + lax.broadcasted_iota(jnp.int32, (tq, span), 0)
        kpos = kstart + lax.broadcasted_iota(jnp.int32, (tq, span), 1)
        band = jnp.where(jnp.abs(kpos - qpos) <= WINDOW, 0.0, -jnp.inf).astype(F32)
        bias = jnp.concatenate([band] * GROUP, axis=0)
        chunks = [(lambda: k_ref[pl.ds(kstart, span), :], lambda: v_ref[pl.ds(kstart, span), :], bias, span)]
    else:
        chunks = _chunk_list(k_ref, v_ref, n_keys, cast=False)
    if has_ctx:
        chunks += _chunk_list(kc_ref, vc_ref, PAST_LEN, cast=True)

    sink_col = None
    if use_sink:
        row = lax.broadcasted_iota(jnp.int32, (GROUP * tq, 1), 0)
        sink_col = jnp.full((GROUP * tq, 1), sink_ref[g * GROUP + GROUP - 1], F32)
        for j in reversed(range(GROUP - 1)):
            sink_col = jnp.where(row < (j + 1) * tq, sink_ref[g * GROUP + j], sink_col)

    o = _attend(qs, chunks, sink_col, LOG2E)
    o_ref[...] = jnp.concatenate([o[j * tq:(j + 1) * tq, :] for j in range(GROUP)], axis=-1).astype(BF16)


def _gqa_attention(q, kh, vh, ctx_k, ctx_v, slot, sink, *, latent, window):
    tq = TM
    if latent:
        n_b, n_keys, n_qt = DEC_BATCH, DEC_SEQ, DEC_SEQ // tq
        row0 = N_PROMPT // tq
        kv_blk0 = N_PROMPT // DEC_SEQ
    else:
        n_b, n_keys, n_qt = BATCH, SEQ, 1
        row0 = 0
        kv_blk0 = 0
    q_spec = pl.BlockSpec((tq, GROUP * HEAD_DIM), lambda b, g, t: (row0 + b * n_qt + t, g))
    kv_spec = pl.BlockSpec((None, n_keys, HEAD_DIM), lambda b, g, t: (g, kv_blk0 + b, 0))
    in_specs = [q_spec, kv_spec, kv_spec]
    args = [q, kh, vh]
    if latent:
        ctx_spec = pl.BlockSpec((None, None, None, PAST_LEN, HEAD_DIM), lambda b, g, t: (b, slot, g, 0, 0))
        in_specs += [ctx_spec, ctx_spec]
        args += [ctx_k, ctx_v]
    use_sink = sink is not None
    if use_sink:
        in_specs.append(pl.BlockSpec(memory_space=pltpu.SMEM))
        args.append(sink)
    return pl.pallas_call(
        functools.partial(_gqa_attn_kernel, tq=tq, n_keys=n_keys, has_ctx=latent,
                          use_sink=use_sink, window=window),
        out_shape=jax.ShapeDtypeStruct((n_b * n_keys, Q_WIDTH), BF16),
        grid=(n_b, N_KV_HEADS, n_qt),
        in_specs=in_specs,
        out_specs=pl.BlockSpec((tq, GROUP * HEAD_DIM), lambda b, g, t: (b * n_qt + t, g)),
        compiler_params=_params(3),
        name="gqa_attn_latent" if latent else "gqa_attn_context",
    )(*args)


MLA_DOWN_EXT = Q_RANK + KV_RANK + LANES
MLA_HEAD_LANES = LANES
MLA_ROPE_LO = MLA_NOPE_DIM


def _mla_proj_kernel(x_ref, sh_ref, sc_ref, wd_ref, qg_ref, wuq_ref, kvg_ref, cos_ref, sin_ref,
                     q_ref, ckv_ref, kpe_ref):
    h = (x_ref[...] * (1.0 + sc_ref[...]) + sh_ref[...]).astype(BF16)
    acc = jnp.dot(h, wd_ref[...], preferred_element_type=F32)
    cos = cos_ref[...]
    sin = sin_ref[...]
    qa = acc[:, :Q_RANK]
    qn = qa * lax.rsqrt(jnp.mean(qa * qa, axis=-1, keepdims=True) + NORM_EPS) * qg_ref[...]
    q = jnp.dot(qn.astype(BF16), wuq_ref[...], preferred_element_type=F32)
    for hd in range(MLA_HEADS):
        t = q[:, hd * LANES:(hd + 1) * LANES]
        q_ref[:, hd * LANES:(hd + 1) * LANES] = _rope_lanes(t, cos, sin, MLA_ROPE_DIM // 2).astype(BF16)
    ckv = acc[:, Q_RANK:Q_RANK + KV_RANK]
    ckv_ref[...] = ckv * lax.rsqrt(jnp.mean(ckv * ckv, axis=-1, keepdims=True) + NORM_EPS) * kvg_ref[...]
    kpe_ref[...] = _rope_lanes(acc[:, Q_RANK + KV_RANK:], cos, sin, MLA_ROPE_DIM // 2)


def _mla_proj(x, mods5, layer, wd_ext, q_gain, wuq_pad, kv_gain, cos_t, sin_t):
    pos_spec = pl.BlockSpec((TM, LANES), lambda i: (_pos_block_of_tile(i), 0))
    return pl.pallas_call(
        _mla_proj_kernel,
        out_shape=(
            jax.ShapeDtypeStruct((N_TOK, MLA_HEADS * LANES), BF16),
            jax.ShapeDtypeStruct((N_TOK, KV_RANK), F32),
            jax.ShapeDtypeStruct((N_TOK, LANES), F32),
        ),
        grid=(TILES,),
        in_specs=[
            _row_spec(D_MODEL), _mod_spec(layer, 0), _mod_spec(layer, 1),
            _full_spec((D_MODEL, MLA_DOWN_EXT)), _full_spec((1, Q_RANK)),
            _full_spec((Q_RANK, MLA_HEADS * LANES)), _full_spec((1, KV_RANK)),
            pos_spec, pos_spec,
        ],
        out_specs=(_row_spec(MLA_HEADS * LANES), _row_spec(KV_RANK), _row_spec(LANES)),
        compiler_params=_params(1),
        name="mla_proj",
    )(x, mods5, mods5, wd_ext, q_gain, wuq_pad, kv_gain, cos_t, sin_t)


def _mla_kv_kernel(c_ref, kpe_ref, wk_ref, wv_ref, k_ref, v_ref):
    c = c_ref[...].astype(BF16)
    kk = jnp.dot(c, wk_ref[...], preferred_element_type=F32)
    vv = jnp.dot(c, wv_ref[...], preferred_element_type=F32)
    kpe = kpe_ref[...]
    for hd in range(MLA_HEADS):
        k_ref[hd] = (kk[:, hd * LANES:(hd + 1) * LANES] + kpe).astype(BF16)
        v_ref[hd] = vv[:, hd * MLA_V_DIM:(hd + 1) * MLA_V_DIM].astype(BF16)


def _mla_kv(c_all, kpe_all, wk_pad, wv):
    n_rows = c_all.shape[0]
    return pl.pallas_call(
        _mla_kv_kernel,
        out_shape=(
            jax.ShapeDtypeStruct((MLA_HEADS, n_rows, LANES), BF16),
            jax.ShapeDtypeStruct((MLA_HEADS, n_rows, MLA_V_DIM), BF16),
        ),
        grid=(n_rows // TM,),
        in_specs=[
            _row_spec(KV_RANK), _row_spec(LANES),
            _full_spec((KV_RANK, MLA_HEADS * LANES)), _full_spec((KV_RANK, MLA_HEADS * MLA_V_DIM)),
        ],
        out_specs=(
            pl.BlockSpec((MLA_HEADS, TM, LANES), lambda i: (0, i, 0)),
            pl.BlockSpec((MLA_HEADS, TM, MLA_V_DIM), lambda i: (0, i, 0)),
        ),
        compiler_params=_params(1),
        name="mla_kv",
    )(c_all, kpe_all, wk_pad, wv)


def _mla_attn_kernel(*refs, tq, n_keys, has_ctx):
    refs = list(refs)
    q_ref, k_ref, v_ref = refs[:3]
    pos = 3
    if has_ctx:
        kc_ref, vc_ref = refs[pos:pos + 2]
        pos += 2
    o_ref = refs[pos]
    outs = []
    for j in range(2):
        qs = q_ref[:, j * LANES:(j + 1) * LANES]
        chunks = _chunk_list(k_ref.at[j], v_ref.at[j], n_keys, cast=False)
        if has_ctx:
            chunks += _chunk_list(kc_ref.at[j], vc_ref.at[j], PAST_LEN, cast=False)
        outs.append(_attend(qs, chunks, None, MLA_SCALE * LOG2E))
    o_ref[...] = jnp.concatenate(outs, axis=-1).astype(BF16)


def _mla_attention(q, k_all, v_all, *, latent):
    if latent:
        tq = MLA_TQ
        n_b, n_keys, n_qt = DEC_BATCH, DEC_SEQ, DEC_SEQ // tq
        row0 = N_PROMPT // tq
        kv_blk0 = N_PROMPT // DEC_SEQ
    else:
        tq = TM
        n_b, n_keys, n_qt = BATCH, SEQ, 1
        row0 = 0
        kv_blk0 = 0
    q_spec = pl.BlockSpec((tq, 2 * LANES), lambda b, hp, t: (row0 + b * n_qt + t, hp))
    k_spec = pl.BlockSpec((2, n_keys, LANES), lambda b, hp, t: (hp, kv_blk0 + b, 0))
    v_spec = pl.BlockSpec((2, n_keys, MLA_V_DIM), lambda b, hp, t: (hp, kv_blk0 + b, 0))
    in_specs = [q_spec, k_spec, v_spec]
    args = [q, k_all, v_all]
    if latent:
        ctx0 = N_TOK // PAST_LEN
        in_specs += [
            pl.BlockSpec((2, PAST_LEN, LANES), lambda b, hp, t: (hp, ctx0 + b, 0)),
            pl.BlockSpec((2, PAST_LEN, MLA_V_DIM), lambda b, hp, t: (hp, ctx0 + b, 0)),
        ]
        args += [k_all, v_all]
    return pl.pallas_call(
        functools.partial(_mla_attn_kernel, tq=tq, n_keys=n_keys, has_ctx=latent),
        out_shape=jax.ShapeDtypeStruct((n_b * n_keys, MLA_HEADS * MLA_V_DIM), BF16),
        grid=(n_b, MLA_HEADS // 2, n_qt),
        in_specs=in_specs,
        out_specs=pl.BlockSpec((tq, 2 * MLA_V_DIM), lambda b, hp, t: (b * n_qt + t, hp)),
        compiler_params=_params(3),
        name="mla_attn_latent" if latent else "mla_attn_context",
    )(*args)


def _post_attn_kernel(op_ref, os_ref, x_ref, wo_ref, g1_ref, sh2_ref, sc2_ref, lng_ref, lnb_ref, wr_ref,
                      x1_ref, h2_ref, aff_ref):
    o = jnp.where(pl.program_id(0) < PROMPT_TILES, op_ref[...], os_ref[...])
    proj = jnp.dot(o, wo_ref[...], preferred_element_type=F32)
    x1 = _layer_norm(ALPHA * x_ref[...] + g1_ref[...] * proj, lng_ref[...], lnb_ref[...])
    x1_ref[...] = x1
    h2 = x1 * (1.0 + sc2_ref[...]) + sh2_ref[...]
    h2_ref[...] = h2
    logits_t = _dot3(wr_ref[...], h2, _NT)
    e = jnp.exp(logits_t - jnp.max(logits_t, axis=0, keepdims=True))
    aff_ref[...] = e / jnp.sum(e, axis=0, keepdims=True)


def _post_attn(o_p, o_s, x, w_o, mods5, layer, ln_g, ln_b, w_router_t):
    op_spec = pl.BlockSpec((TM, D_MODEL), lambda i: (jnp.minimum(i, PROMPT_TILES - 1), 0))
    os_spec = pl.BlockSpec((TM, D_MODEL), lambda i: (jnp.maximum(i - PROMPT_TILES, 0), 0))
    return pl.pallas_call(
        _post_attn_kernel,
        out_shape=(
            jax.ShapeDtypeStruct((N_TOK, D_MODEL), F32),
            jax.ShapeDtypeStruct((N_TOK, D_MODEL), F32),
            jax.ShapeDtypeStruct((N_EXPERTS, N_TOK), F32),
        ),
        grid=(TILES,),
        in_specs=[
            op_spec, os_spec, _row_spec(D_MODEL), _full_spec((D_MODEL, D_MODEL)),
            _mod_spec(layer, 2), _mod_spec(layer, 3), _mod_spec(layer, 4),
            _full_spec((1, D_MODEL)), _full_spec((1, D_MODEL)), _full_spec((N_EXPERTS, D_MODEL)),
        ],
        out_specs=(_row_spec(D_MODEL), _row_spec(D_MODEL), pl.BlockSpec((N_EXPERTS, TM), lambda i: (0, i))),
        compiler_params=_params(1),
        name="post_attn",
    )(o_p, o_s, x, w_o, mods5, mods5, mods5, ln_g, ln_b, w_router_t)


FFN_ROWS = 512
FFN_STEPS = EXPERT_FF // FF_TILE
ROW_CHUNKS = CAP // FFN_ROWS
GATHER_PER_CHUNK = CAP // (FFN_STEPS * ROW_CHUNKS)


def _row_gather(idx_ref, h_hbm, xbuf, sem, expert, s, buf):
    tok = idx_ref[expert * CAP + s]
    return pltpu.make_async_copy(h_hbm.at[pl.ds(tok, 1), :], xbuf.at[buf, pl.ds(s, 1), :], sem.at[buf])


def _wait_rows(xbuf, sem, buf):
    pltpu.make_async_copy(xbuf.at[buf], xbuf.at[buf], sem.at[buf]).wait()


def _ffn_kernel(idx_ref, h_hbm, wg_ref, wu_ref, wd_ref, gate_ref, o_ref, xbuf, sem):
    e = pl.program_id(0)
    f = pl.program_id(1)
    buf = e % 2
    nxt = jnp.minimum(e + 1, N_EXPERTS - 1)

    @pl.when((e == 0) & (f == 0))
    def _():
        def issue(s, carry):
            _row_gather(idx_ref, h_hbm, xbuf, sem, 0, s, 0).start()
            return carry
        lax.fori_loop(0, CAP, issue, 0)

    @pl.when(f == 0)
    def _():
        _wait_rows(xbuf, sem, buf)
        o_ref[...] = jnp.zeros_like(o_ref)

    wg = wg_ref[...].astype(BF16)
    wu = wu_ref[...].astype(BF16)
    wd = wd_ref[...].astype(BF16)
    for c in range(ROW_CHUNKS):
        r0 = c * FFN_ROWS
        for j in range(GATHER_PER_CHUNK):
            s = (f * ROW_CHUNKS + c) * GATHER_PER_CHUNK + j
            _row_gather(idx_ref, h_hbm, xbuf, sem, nxt, s, 1 - buf).start()
        x = xbuf[buf, r0:r0 + FFN_ROWS, :].astype(BF16)
        a = jnp.dot(x, wg, preferred_element_type=F32)
        u = jnp.dot(x, wu, preferred_element_type=F32)
        hid = (a / (1.0 + jnp.exp(-a)) * u).astype(BF16)
        o_ref[r0:r0 + FFN_ROWS, :] += jnp.dot(hid, wd, preferred_element_type=F32)

    @pl.when(f == FFN_STEPS - 1)
    def _():
        o_ref[...] = o_ref[...] * gate_ref[...]

    @pl.when((e == N_EXPERTS - 1) & (f == FFN_STEPS - 1))
    def _():
        _wait_rows(xbuf, sem, 1 - buf)


def _expert_ffn(idx, h2, gate, w_gate, w_up, w_down, layer):
    grid_spec = pltpu.PrefetchScalarGridSpec(
        num_scalar_prefetch=1,
        grid=(N_EXPERTS, FFN_STEPS),
        in_specs=[
            pl.BlockSpec(memory_space=pl.ANY),
            pl.BlockSpec((None, None, D_MODEL, FF_TILE), lambda e, f, idx: (layer, e, 0, f)),
            pl.BlockSpec((None, None, D_MODEL, FF_TILE), lambda e, f, idx: (layer, e, 0, f)),
            pl.BlockSpec((None, None, FF_TILE, D_MODEL), lambda e, f, idx: (layer, e, f, 0)),
            pl.BlockSpec((None, CAP, 1), lambda e, f, idx: (e, 0, 0)),
        ],
        out_specs=pl.BlockSpec((None, CAP, D_MODEL), lambda e, f, idx: (e, 0, 0)),
        scratch_shapes=[pltpu.VMEM((2, CAP, D_MODEL), F32), pltpu.SemaphoreType.DMA((2,))],
    )
    return pl.pallas_call(
        _ffn_kernel,
        out_shape=jax.ShapeDtypeStruct((N_EXPERTS, CAP, D_MODEL), F32),
        grid_spec=grid_spec,
        compiler_params=_params(2),
        name="expert_ffn",
    )(idx, h2, w_gate, w_up, w_down, gate)


def _post_moe_kernel(x_ref, f_ref, g2_ref, lng_ref, lnb_ref, o_ref):
    o_ref[...] = _layer_norm(ALPHA * x_ref[...] + g2_ref[...] * f_ref[...], lng_ref[...], lnb_ref[...])


def _post_moe(x1, f, mods5, layer, ln_g, ln_b):
    return pl.pallas_call(
        _post_moe_kernel,
        out_shape=jax.ShapeDtypeStruct((N_TOK, D_MODEL), F32),
        grid=(TILES,),
        in_specs=[_row_spec(D_MODEL), _row_spec(D_MODEL), _mod_spec(layer, 5),
                  _full_spec((1, D_MODEL)), _full_spec((1, D_MODEL))],
        out_specs=_row_spec(D_MODEL),
        compiler_params=_params(1),
        name="post_moe",
    )(x1, f, mods5, ln_g, ln_b)


def _axial_tables(dim):
    n_rows = DEC_SEQ // GRID_W
    rows = jnp.repeat(jnp.arange(n_rows, dtype=F32), GRID_W)
    cols = jnp.tile(jnp.arange(GRID_W, dtype=F32), n_rows)
    n_freq = dim // 4
    inv_freq = ROPE_THETA ** (-jnp.arange(n_freq, dtype=F32) / n_freq)
    ang = jnp.concatenate([rows[:, None] * inv_freq, cols[:, None] * inv_freq], -1)
    cos, sin = jnp.cos(ang), jnp.sin(ang)
    return jnp.concatenate([cos, cos], -1), jnp.concatenate([-sin, sin], -1)


def _with_identity_rows(cos_l, sin_l):
    return (jnp.concatenate([jnp.ones((TM, LANES), F32), cos_l], 0),
            jnp.concatenate([jnp.zeros((TM, LANES), F32), sin_l], 0))


def _gqa_rope_tables():
    cos, sin = _axial_tables(HEAD_DIM)
    return _with_identity_rows(jnp.tile(cos, (1, 2)), jnp.tile(sin, (1, 2)))


def _mla_rope_tables():
    cos, sin = _axial_tables(MLA_ROPE_DIM)
    pad_hi = LANES - MLA_ROPE_LO - MLA_ROPE_DIM
    cos_l = jnp.concatenate([jnp.ones((DEC_SEQ, MLA_ROPE_LO), F32), cos, jnp.ones((DEC_SEQ, pad_hi), F32)], 1)
    sin_l = jnp.pad(sin, ((0, 0), (MLA_ROPE_LO, pad_hi)))
    return _with_identity_rows(cos_l, sin_l)


def _moe(h2, aff_t, w_gate, w_up, w_down, layer):
    gate_p, idx_p = lax.top_k(aff_t[:, :N_PROMPT], CAP_P)
    gate_s, idx_s = lax.top_k(aff_t[:, N_PROMPT:], CAP_S)
    idx = jnp.concatenate([idx_p, idx_s + N_PROMPT], axis=1).reshape(-1)
    gate = jnp.concatenate([gate_p, gate_s], axis=1)[..., None]
    ye = _expert_ffn(idx, h2, gate, w_gate, w_up, w_down, layer)
    return jnp.zeros((N_TOK, D_MODEL), F32).at[idx].add(ye.reshape(-1, D_MODEL))


def kernel(x_prompt, x_sample, cache_gqa_k, cache_gqa_v, cache_mla_ckv, cache_mla_kpe, c, c_ctx,
           ada_w, ada_b, ln_g, ln_b, a_w_qkv, a_q_norm, a_k_norm, a_w_o, b_w_qkv, b_sink, b_w_o,
           mla_w_down, mla_q_norm, mla_w_uq, mla_kv_norm, mla_w_ukv, mla_w_o,
           moe_w_router, moe_w_gate, moe_w_up, moe_w_down):
    x = jnp.concatenate([x_prompt.reshape(N_PROMPT, D_MODEL), x_sample.reshape(N_SAMPLE, D_MODEL)], 0)
    cond = jnp.concatenate([c_ctx[None], c, jnp.zeros((N_COND - 1 - DEC_BATCH, D_MODEL), F32)], 0)
    mods5 = _adaln(cond, ada_w, ada_b).reshape(DEPTH, 6, N_COND, 1, D_MODEL)
    gqa_cos, gqa_sin = _gqa_rope_tables()
    mla_cos, mla_sin = _mla_rope_tables()

    new_k, new_v, new_ckv, new_kpe = [], [], [], []
    gqa_slot = 0
    for i in range(DEPTH):
        kind, j = i % 3, i // 3
        if kind in (0, 1):
            if kind == 0:
                w_qkv, w_o, sink = a_w_qkv[j], a_w_o[j], None
                gain = jnp.concatenate([jnp.tile(a_q_norm[j], N_HEADS), jnp.tile(a_k_norm[j], N_KV_HEADS)])[None]
            else:
                w_qkv, w_o, sink = b_w_qkv[j], b_w_o[j], b_sink[j]
                gain = jnp.ones((1, QK_WIDTH), F32)
            q, kh, vh, k32, v32 = _gqa_proj(x, mods5, i, w_qkv.astype(BF16), gain, gqa_cos, gqa_sin,
                                            use_norm=(kind == 0))
            o_p = _gqa_attention(q, kh, vh, None, None, gqa_slot, sink, latent=False, window=False)
            o_s = _gqa_attention(q, kh, vh, cache_gqa_k, cache_gqa_v, gqa_slot, sink,
                                 latent=True, window=(kind == 1))
            for store, arr in ((new_k, k32), (new_v, v32)):
                store.append(arr[:, :N_PROMPT].reshape(N_KV_HEADS, BATCH, SEQ, HEAD_DIM).transpose(1, 0, 2, 3))
            gqa_slot += 1
        else:
            w_down = mla_w_down[j]
            rope_lo = Q_RANK + KV_RANK
            wd_ext = jnp.concatenate([
                w_down[:, :rope_lo], jnp.zeros((D_MODEL, MLA_ROPE_LO), F32), w_down[:, rope_lo:],
                jnp.zeros((D_MODEL, LANES - MLA_ROPE_LO - MLA_ROPE_DIM), F32)], 1).astype(BF16)
            wuq_pad = jnp.pad(mla_w_uq[j].reshape(Q_RANK, MLA_HEADS, MLA_QK_DIM),
                              ((0, 0), (0, 0), (0, LANES - MLA_QK_DIM))).reshape(Q_RANK, MLA_HEADS * LANES)
            w_ukv = mla_w_ukv[j].reshape(KV_RANK, MLA_HEADS, MLA_NOPE_DIM + MLA_V_DIM)
            wk_pad = jnp.pad(w_ukv[:, :, :MLA_NOPE_DIM], ((0, 0), (0, 0), (0, LANES - MLA_NOPE_DIM)))
            wk_pad = wk_pad.reshape(KV_RANK, MLA_HEADS * LANES).astype(BF16)
            wv = w_ukv[:, :, MLA_NOPE_DIM:].reshape(KV_RANK, MLA_HEADS * MLA_V_DIM).astype(BF16)
            q, ckv, kpe = _mla_proj(x, mods5, i, wd_ext, mla_q_norm[j][None], wuq_pad.astype(BF16),
                                    mla_kv_norm[j][None], mla_cos, mla_sin)
            c_all = jnp.concatenate([ckv, cache_mla_ckv[:, j].reshape(DEC_BATCH * PAST_LEN, KV_RANK)], 0)
            kpe_ctx = jnp.pad(cache_mla_kpe[:, j].reshape(DEC_BATCH * PAST_LEN, MLA_ROPE_DIM),
                              ((0, 0), (MLA_ROPE_LO, LANES - MLA_ROPE_LO - MLA_ROPE_DIM)))
            kpe_all = jnp.concatenate([kpe, kpe_ctx], 0)
            k_all, v_all = _mla_kv(c_all, kpe_all, wk_pad, wv)
            o_p = _mla_attention(q, k_all, v_all, latent=False)
            o_s = _mla_attention(q, k_all, v_all, latent=True)
            w_o = mla_w_o[j]
            new_ckv.append(ckv[:N_PROMPT].reshape(BATCH, SEQ, KV_RANK))
            new_kpe.append(kpe[:N_PROMPT, MLA_ROPE_LO:MLA_ROPE_LO + MLA_ROPE_DIM].reshape(BATCH, SEQ, MLA_ROPE_DIM))
        x1, h2, aff_t = _post_attn(o_p, o_s, x, w_o.astype(BF16), mods5, i, ln_g[i, 0][None], ln_b[i, 0][None],
                                   moe_w_router[i].T)
        f = _moe(h2, aff_t, moe_w_gate, moe_w_up, moe_w_down, i)
        x = _post_moe(x1, f, mods5, i, ln_g[i, 1][None], ln_b[i, 1][None])

    y_prompt = x[:N_PROMPT].reshape(BATCH, SEQ, D_MODEL)
    y_sample = x[N_PROMPT:].reshape(DEC_BATCH, DEC_SEQ, D_MODEL)
    return (y_prompt, y_sample, jnp.stack(new_k, 1), jnp.stack(new_v, 1),
            jnp.stack(new_ckv, 1), jnp.stack(new_kpe, 1))
```

```python
import functools
import math

import jax
import jax.numpy as jnp
from jax import lax
from jax.experimental import pallas as pl
from jax.experimental.pallas import tpu as pltpu

F32 = jnp.float32
BF16 = jnp.bfloat16

D_MODEL = 1024
BATCH = 16
SEQ = 256
DEPTH = 4
DEC_BATCH = 4
DEC_SEQ = 2048
PAST_LEN = 512
GRID_W = 64
N_HEADS = 16
N_KV_HEADS = 4
GROUP = N_HEADS // N_KV_HEADS
HEAD_DIM = 64
WINDOW = 128
MLA_HEADS = 16
Q_RANK = 256
KV_RANK = 128
MLA_NOPE_DIM = 64
MLA_ROPE_DIM = 32
MLA_V_DIM = 64
MLA_QK_DIM = MLA_NOPE_DIM + MLA_ROPE_DIM
MLA_SCALE = MLA_QK_DIM ** -0.5
GQA_SCALE = HEAD_DIM ** -0.5
N_EXPERTS = 16
EXPERT_FF = 2048
CAPACITY_FACTOR = 2
ROPE_THETA = 10000.0
NORM_EPS = 1e-6
LN_EPS = 1e-5
ALPHA = (2 * DEPTH) ** 0.25
LOG2E = math.log2(math.e)

LANES = 128
N_PROMPT = BATCH * SEQ
N_SAMPLE = DEC_BATCH * DEC_SEQ
N_TOK = N_PROMPT + N_SAMPLE
TM = 256
PROMPT_TILES = N_PROMPT // TM
TILES = N_TOK // TM
SAMPLE_TILES_PER_REQ = DEC_SEQ // TM
N_COND = 8
CAP_P = CAPACITY_FACTOR * N_PROMPT // N_EXPERTS
CAP_S = CAPACITY_FACTOR * N_SAMPLE // N_EXPERTS
CAP = CAP_P + CAP_S
FF_TILE = 512
KV_CHUNK = 512
MLA_TQ = 512

_NT = (((1,), (1,)), ((), ()))
_VMEM_LIMIT = 56 * 1024 * 1024


def _params(n_axes, vmem=_VMEM_LIMIT):
    return pltpu.CompilerParams(dimension_semantics=("arbitrary",) * n_axes, vmem_limit_bytes=vmem)


def _cond_of_tile(i):
    return jnp.where(i < PROMPT_TILES, 0, 1 + (i - PROMPT_TILES) // SAMPLE_TILES_PER_REQ)


def _pos_block_of_tile(i):
    return jnp.where(i < PROMPT_TILES, 0, 1 + (i - PROMPT_TILES) % SAMPLE_TILES_PER_REQ)


def _mod_spec(layer, k):
    return pl.BlockSpec((None, None, None, 1, D_MODEL), lambda i: (layer, k, _cond_of_tile(i), 0, 0))


def _row_spec(width):
    return pl.BlockSpec((TM, width), lambda i: (i, 0))


def _full_spec(shape):
    nd = len(shape)
    return pl.BlockSpec(shape, lambda *_: (0,) * nd)


TOK_TILE = 8


def _tok_tile_spec():
    return pl.BlockSpec((TM * TOK_TILE, LANES), lambda i: (i, 0))


def _tok_rows(t):
    return pl.ds(pl.multiple_of(t * TOK_TILE, TOK_TILE), TOK_TILE)


def _lane_block_rows(j, n_tokens, t0=0):
    return pl.ds(t0 * TOK_TILE + j, n_tokens, stride=TOK_TILE)


def _store_token_tiles(ref, val):
    for j in range(TOK_TILE):
        ref[_lane_block_rows(j, val.shape[0]), :] = val[:, j * LANES:(j + 1) * LANES]


def _load_token_tiles(ref, n_tokens, t0=0):
    return jnp.concatenate([ref[_lane_block_rows(j, n_tokens, t0), :] for j in range(TOK_TILE)], axis=-1)


def _split_bf16(a):
    hi = a.astype(BF16)
    lo = (a - hi.astype(F32)).astype(BF16)
    return hi, lo


def _dot3(a, b, dims=(((1,), (0,)), ((), ()))):
    a_hi, a_lo = _split_bf16(a)
    b_hi, b_lo = _split_bf16(b)
    dg = functools.partial(lax.dot_general, dimension_numbers=dims, preferred_element_type=F32)
    return dg(a_hi, b_hi) + dg(a_lo, b_hi) + dg(a_hi, b_lo)


def _layer_norm(y, g, b):
    mu = jnp.mean(y, axis=-1, keepdims=True)
    d = y - mu
    var = jnp.mean(d * d, axis=-1, keepdims=True)
    return d * lax.rsqrt(var + LN_EPS) * g + b


def _rope_lanes(t, cos, sin, half):
    lane = lax.broadcasted_iota(jnp.int32, t.shape, 1)
    first = (lane & (2 * half - 1)) < half
    partner = jnp.where(first, pltpu.roll(t, LANES - half, 1), pltpu.roll(t, half, 1))
    return t * cos + partner * sin


def _pair_rms(t, gain):
    sq = t * t
    lane = lax.broadcasted_iota(jnp.int32, t.shape, 1)
    lo = lane < HEAD_DIM
    s_lo = jnp.sum(jnp.where(lo, sq, 0.0), axis=-1, keepdims=True)
    s_hi = jnp.sum(jnp.where(lo, 0.0, sq), axis=-1, keepdims=True)
    ms = jnp.where(lo, s_lo, s_hi) * (1.0 / HEAD_DIM)
    return t * lax.rsqrt(ms + NORM_EPS) * gain


def _adaln_kernel(c_ref, w_ref, b_ref, o_ref):
    c = c_ref[...]
    a = c / (1.0 + jnp.exp(-c))
    o_ref[...] = _dot3(a, w_ref[...]) + b_ref[...]


def _adaln(cond, ada_w, ada_b):
    return pl.pallas_call(
        _adaln_kernel,
        out_shape=jax.ShapeDtypeStruct((DEPTH, 6, N_COND, D_MODEL), F32),
        grid=(DEPTH, 6),
        in_specs=[
            pl.BlockSpec((N_COND, D_MODEL), lambda l, k: (0, 0)),
            pl.BlockSpec((None, D_MODEL, D_MODEL), lambda l, k: (l, 0, k)),
            pl.BlockSpec((None, None, 1, D_MODEL), lambda l, k: (l, k, 0, 0)),
        ],
        out_specs=pl.BlockSpec((None, None, N_COND, D_MODEL), lambda l, k: (l, k, 0, 0)),
        compiler_params=_params(2),
        name="adaln",
    )(cond, ada_w, ada_b.reshape(DEPTH, 6, 1, D_MODEL))


QK_WIDTH = (N_HEADS + N_KV_HEADS) * HEAD_DIM
QKV_WIDTH = (N_HEADS + 2 * N_KV_HEADS) * HEAD_DIM
Q_WIDTH = N_HEADS * HEAD_DIM


def _gqa_proj_kernel(x_ref, sh_ref, sc_ref, w_ref, gain_ref, cos_ref, sin_ref,
                     q_ref, kh_ref, vh_ref, k32_ref, v32_ref, *, use_norm):
    h = (x_ref[...] * (1.0 + sc_ref[...]) + sh_ref[...]).astype(BF16)
    acc = jnp.dot(h, w_ref[...], preferred_element_type=F32)
    cos = cos_ref[...]
    sin = sin_ref[...]
    for j in range(QK_WIDTH // LANES):
        t = acc[:, j * LANES:(j + 1) * LANES]
        if use_norm:
            t = _pair_rms(t, gain_ref[:, j * LANES:(j + 1) * LANES])
        t = _rope_lanes(t, cos, sin, HEAD_DIM // 2)
        if j < Q_WIDTH // LANES:
            q_ref[:, j * LANES:(j + 1) * LANES] = (t * GQA_SCALE).astype(BF16)
        else:
            for half in range(2):
                kv_head = 2 * (j - Q_WIDTH // LANES) + half
                th = t[:, half * HEAD_DIM:(half + 1) * HEAD_DIM]
                kh_ref[kv_head] = th.astype(BF16)
                k32_ref[kv_head] = th
    for kv_head in range(N_KV_HEADS):
        lo = QK_WIDTH + kv_head * HEAD_DIM
        tv = acc[:, lo:lo + HEAD_DIM]
        vh_ref[kv_head] = tv.astype(BF16)
        v32_ref[kv_head] = tv


def _gqa_proj(x, mods5, layer, w_qkv, gain, cos_t, sin_t, use_norm):
    head_spec = pl.BlockSpec((N_KV_HEADS, TM, HEAD_DIM), lambda i: (0, i, 0))
    pos_spec = pl.BlockSpec((TM, LANES), lambda i: (_pos_block_of_tile(i), 0))
    return pl.pallas_call(
        functools.partial(_gqa_proj_kernel, use_norm=use_norm),
        out_shape=(
            jax.ShapeDtypeStruct((N_TOK, Q_WIDTH), BF16),
            jax.ShapeDtypeStruct((N_KV_HEADS, N_TOK, HEAD_DIM), BF16),
            jax.ShapeDtypeStruct((N_KV_HEADS, N_TOK, HEAD_DIM), BF16),
            jax.ShapeDtypeStruct((N_KV_HEADS, N_TOK, HEAD_DIM), F32),
            jax.ShapeDtypeStruct((N_KV_HEADS, N_TOK, HEAD_DIM), F32),
        ),
        grid=(TILES,),
        in_specs=[
            _row_spec(D_MODEL), _mod_spec(layer, 0), _mod_spec(layer, 1),
            _full_spec((D_MODEL, QKV_WIDTH)), _full_spec((1, QK_WIDTH)),
            pos_spec, pos_spec,
        ],
        out_specs=(_row_spec(Q_WIDTH), head_spec, head_spec, head_spec, head_spec),
        compiler_params=_params(1),
        name="gqa_proj",
    )(x, mods5, mods5, w_qkv, gain, cos_t, sin_t)


def _attend(qs, chunks, sink_col, c1):
    m = den = acc = None
    for k_fn, v_fn, bias, _ in chunks:
        s = lax.dot_general(qs, k_fn(), _NT, preferred_element_type=F32)
        if bias is not None:
            s = s + bias
        mc = jnp.max(s, axis=-1, keepdims=True)
        m_new = mc if m is None else jnp.maximum(m, mc)
        p = jnp.exp2((s - m_new) * c1)
        ps = jnp.sum(p, axis=-1, keepdims=True)
        pv = jnp.dot(p.astype(BF16), v_fn(), preferred_element_type=F32)
        if m is None:
            den, acc = ps, pv
        else:
            alpha = jnp.exp2((m - m_new) * c1)
            den = den * alpha + ps
            acc = acc * alpha + pv
        m = m_new
    if sink_col is not None:
        den = den + jnp.exp2((sink_col - m) * c1)
    return acc / den


def _chunk_list(k_ref, v_ref, length, cast):
    out = []
    width = min(KV_CHUNK, length)
    for c0 in range(0, length, width):
        if cast:
            k_fn = lambda c0=c0: k_ref[c0:c0 + width, :].astype(BF16)
            v_fn = lambda c0=c0: v_ref[c0:c0 + width, :].astype(BF16)
        else:
            k_fn = lambda c0=c0: k_ref[c0:c0 + width, :]
            v_fn = lambda c0=c0: v_ref[c0:c0 + width, :]
        out.append((k_fn, v_fn, None, width))
    return out


def _gqa_attn_kernel(*refs, tq, n_keys, has_ctx, use_sink, window):
    refs = list(refs)
    q_ref, k_ref, v_ref = refs[:3]
    pos = 3
    if has_ctx:
        kc_ref, vc_ref = refs[pos:pos + 2]
        pos += 2
    if use_sink:
        sink_ref = refs[pos]
        pos += 1
    o_ref = refs[pos]
    g = pl.program_id(1)
    qi = pl.program_id(2)

    qf = q_ref[...].astype(F32)
    qs = jnp.concatenate(
        [qf[:, j * HEAD_DIM:(j + 1) * HEAD_DIM] for j in range(GROUP)], axis=0).astype(BF16)

    if window:
        span = tq + 2 * WINDOW
        kstart = pl.multiple_of(jnp.clip(qi * tq - WINDOW, 0, n_keys - span), LANES)
        qpos = qi * tq + lax.broadcasted_iota(jnp.int32, (tq, span), 0)
        kpos = kstart + lax.broadcasted_iota(jnp.int32, (tq, span), 1)
        band = jnp.where(jnp.abs(kpos - qpos) <= WINDOW, 0.0, -jnp.inf).astype(F32)
        bias = jnp.concatenate([band] * GROUP, axis=0)
        chunks = [(lambda: k_ref[pl.ds(kstart, span), :], lambda: v_ref[pl.ds(kstart, span), :], bias, span)]
    else:
        chunks = _chunk_list(k_ref, v_ref, n_keys, cast=False)
    if has_ctx:
        chunks += _chunk_list(kc_ref, vc_ref, PAST_LEN, cast=True)

    sink_col = None
    if use_sink:
        row = lax.broadcasted_iota(jnp.int32, (GROUP * tq, 1), 0)
        sink_col = jnp.full((GROUP * tq, 1), sink_ref[g * GROUP + GROUP - 1], F32)
        for j in reversed(range(GROUP - 1)):
            sink_col = jnp.where(row < (j + 1) * tq, sink_ref[g * GROUP + j], sink_col)

    o = _attend(qs, chunks, sink_col, LOG2E)
    o_ref[...] = jnp.concatenate([o[j * tq:(j + 1) * tq, :] for j in range(GROUP)], axis=-1).astype(BF16)


def _gqa_attention(q, kh, vh, ctx_k, ctx_v, slot, sink, *, latent, window):
    tq = TM
    if latent:
        n_b, n_keys, n_qt = DEC_BATCH, DEC_SEQ, DEC_SEQ // tq
        row0 = N_PROMPT // tq
        kv_blk0 = N_PROMPT // DEC_SEQ
    else:
        n_b, n_keys, n_qt = BATCH, SEQ, 1
        row0 = 0
        kv_blk0 = 0
    q_spec = pl.BlockSpec((tq, GROUP * HEAD_DIM), lambda b, g, t: (row0 + b * n_qt + t, g))
    kv_spec = pl.BlockSpec((None, n_keys, HEAD_DIM), lambda b, g, t: (g, kv_blk0 + b, 0))
    in_specs = [q_spec, kv_spec, kv_spec]
    args = [q, kh, vh]
    if latent:
        ctx_spec = pl.BlockSpec((None, None, None, PAST_LEN, HEAD_DIM), lambda b, g, t: (b, slot, g, 0, 0))
        in_specs += [ctx_spec, ctx_spec]
        args += [ctx_k, ctx_v]
    use_sink = sink is not None
    if use_sink:
        in_specs.append(pl.BlockSpec(memory_space=pltpu.SMEM))
        args.append(sink)
    return pl.pallas_call(
        functools.partial(_gqa_attn_kernel, tq=tq, n_keys=n_keys, has_ctx=latent,
                          use_sink=use_sink, window=window),
        out_shape=jax.ShapeDtypeStruct((n_b * n_keys, Q_WIDTH), BF16),
        grid=(n_b, N_KV_HEADS, n_qt),
        in_specs=in_specs,
        out_specs=pl.BlockSpec((tq, GROUP * HEAD_DIM), lambda b, g, t: (b * n_qt + t, g)),
        compiler_params=_params(3),
        name="gqa_attn_latent" if latent else "gqa_attn_context",
    )(*args)


MLA_DOWN_EXT = Q_RANK + KV_RANK + LANES
MLA_HEAD_LANES = LANES
MLA_ROPE_LO = MLA_NOPE_DIM


def _mla_proj_kernel(x_ref, sh_ref, sc_ref, wd_ref, qg_ref, wuq_ref, kvg_ref, cos_ref, sin_ref,
                     q_ref, ckv_ref, kpe_ref):
    h = (x_ref[...] * (1.0 + sc_ref[...]) + sh_ref[...]).astype(BF16)
    acc = jnp.dot(h, wd_ref[...], preferred_element_type=F32)
    cos = cos_ref[...]
    sin = sin_ref[...]
    qa = acc[:, :Q_RANK]
    qn = qa * lax.rsqrt(jnp.mean(qa * qa, axis=-1, keepdims=True) + NORM_EPS) * qg_ref[...]
    q = jnp.dot(qn.astype(BF16), wuq_ref[...], preferred_element_type=F32)
    for hd in range(MLA_HEADS):
        t = q[:, hd * LANES:(hd + 1) * LANES]
        q_ref[:, hd * LANES:(hd + 1) * LANES] = _rope_lanes(t, cos, sin, MLA_ROPE_DIM // 2).astype(BF16)
    ckv = acc[:, Q_RANK:Q_RANK + KV_RANK]
    ckv_ref[...] = ckv * lax.rsqrt(jnp.mean(ckv * ckv, axis=-1, keepdims=True) + NORM_EPS) * kvg_ref[...]
    kpe_ref[...] = _rope_lanes(acc[:, Q_RANK + KV_RANK:], cos, sin, MLA_ROPE_DIM // 2)


def _mla_proj(x, mods5, layer, wd_ext, q_gain, wuq_pad, kv_gain, cos_t, sin_t):
    pos_spec = pl.BlockSpec((TM, LANES), lambda i: (_pos_block_of_tile(i), 0))
    return pl.pallas_call(
        _mla_proj_kernel,
        out_shape=(
            jax.ShapeDtypeStruct((N_TOK, MLA_HEADS * LANES), BF16),
            jax.ShapeDtypeStruct((N_TOK, KV_RANK), F32),
            jax.ShapeDtypeStruct((N_TOK, LANES), F32),
        ),
        grid=(TILES,),
        in_specs=[
            _row_spec(D_MODEL), _mod_spec(layer, 0), _mod_spec(layer, 1),
            _full_spec((D_MODEL, MLA_DOWN_EXT)), _full_spec((1, Q_RANK)),
            _full_spec((Q_RANK, MLA_HEADS * LANES)), _full_spec((1, KV_RANK)),
            pos_spec, pos_spec,
        ],
        out_specs=(_row_spec(MLA_HEADS * LANES), _row_spec(KV_RANK), _row_spec(LANES)),
        compiler_params=_params(1),
        name="mla_proj",
    )(x, mods5, mods5, wd_ext, q_gain, wuq_pad, kv_gain, cos_t, sin_t)


def _mla_kv_kernel(c_ref, kpe_ref, wk_ref, wv_ref, k_ref, v_ref):
    c = c_ref[...].astype(BF16)
    kk = jnp.dot(c, wk_ref[...], preferred_element_type=F32)
    vv = jnp.dot(c, wv_ref[...], preferred_element_type=F32)
    kpe = kpe_ref[...]
    for hd in range(MLA_HEADS):
        k_ref[hd] = (kk[:, hd * LANES:(hd + 1) * LANES] + kpe).astype(BF16)
        v_ref[hd] = vv[:, hd * MLA_V_DIM:(hd + 1) * MLA_V_DIM].astype(BF16)


def _mla_kv(c_all, kpe_all, wk_pad, wv):
    n_rows = c_all.shape[0]
    return pl.pallas_call(
        _mla_kv_kernel,
        out_shape=(
            jax.ShapeDtypeStruct((MLA_HEADS, n_rows, LANES), BF16),
            jax.ShapeDtypeStruct((MLA_HEADS, n_rows, MLA_V_DIM), BF16),
        ),
        grid=(n_rows // TM,),
        in_specs=[
            _row_spec(KV_RANK), _row_spec(LANES),
            _full_spec((KV_RANK, MLA_HEADS * LANES)), _full_spec((KV_RANK, MLA_HEADS * MLA_V_DIM)),
        ],
        out_specs=(
            pl.BlockSpec((MLA_HEADS, TM, LANES), lambda i: (0, i, 0)),
            pl.BlockSpec((MLA_HEADS, TM, MLA_V_DIM), lambda i: (0, i, 0)),
        ),
        compiler_params=_params(1),
        name="mla_kv",
    )(c_all, kpe_all, wk_pad, wv)


def _mla_attn_kernel(*refs, tq, n_keys, has_ctx):
    refs = list(refs)
    q_ref, k_ref, v_ref = refs[:3]
    pos = 3
    if has_ctx:
        kc_ref, vc_ref = refs[pos:pos + 2]
        pos += 2
    o_ref = refs[pos]
    outs = []
    for j in range(2):
        qs = q_ref[:, j * LANES:(j + 1) * LANES]
        chunks = _chunk_list(k_ref.at[j], v_ref.at[j], n_keys, cast=False)
        if has_ctx:
            chunks += _chunk_list(kc_ref.at[j], vc_ref.at[j], PAST_LEN, cast=False)
        outs.append(_attend(qs, chunks, None, MLA_SCALE * LOG2E))
    o_ref[...] = jnp.concatenate(outs, axis=-1).astype(BF16)


def _mla_attention(q, k_all, v_all, *, latent):
    if latent:
        tq = MLA_TQ
        n_b, n_keys, n_qt = DEC_BATCH, DEC_SEQ, DEC_SEQ // tq
        row0 = N_PROMPT // tq
        kv_blk0 = N_PROMPT // DEC_SEQ
    else:
        tq = TM
        n_b, n_keys, n_qt = BATCH, SEQ, 1
        row0 = 0
        kv_blk0 = 0
    q_spec = pl.BlockSpec((tq, 2 * LANES), lambda b, hp, t: (row0 + b * n_qt + t, hp))
    k_spec = pl.BlockSpec((2, n_keys, LANES), lambda b, hp, t: (hp, kv_blk0 + b, 0))
    v_spec = pl.BlockSpec((2, n_keys, MLA_V_DIM), lambda b, hp, t: (hp, kv_blk0 + b, 0))
    in_specs = [q_spec, k_spec, v_spec]
    args = [q, k_all, v_all]
    if latent:
        ctx0 = N_TOK // PAST_LEN
        in_specs += [
            pl.BlockSpec((2, PAST_LEN, LANES), lambda b, hp, t: (hp, ctx0 + b, 0)),
            pl.BlockSpec((2, PAST_LEN, MLA_V_DIM), lambda b, hp, t: (hp, ctx0 + b, 0)),
        ]
        args += [k_all, v_all]
    return pl.pallas_call(
        functools.partial(_mla_attn_kernel, tq=tq, n_keys=n_keys, has_ctx=latent),
        out_shape=jax.ShapeDtypeStruct((n_b * n_keys, MLA_HEADS * MLA_V_DIM), BF16),
        grid=(n_b, MLA_HEADS // 2, n_qt),
        in_specs=in_specs,
        out_specs=pl.BlockSpec((tq, 2 * MLA_V_DIM), lambda b, hp, t: (b * n_qt + t, hp)),
        compiler_params=_params(3),
        name="mla_attn_latent" if latent else "mla_attn_context",
    )(*args)


def _post_attn_kernel(op_ref, os_ref, x_ref, wo_ref, g1_ref, sh2_ref, sc2_ref, lng_ref, lnb_ref, wr_ref,
                      x1_ref, h2_ref, aff_ref):
    o = jnp.where(pl.program_id(0) < PROMPT_TILES, op_ref[...], os_ref[...])
    proj = jnp.dot(o, wo_ref[...], preferred_element_type=F32)
    x1 = _layer_norm(ALPHA * x_ref[...] + g1_ref[...] * proj, lng_ref[...], lnb_ref[...])
    x1_ref[...] = x1
    h2 = x1 * (1.0 + sc2_ref[...]) + sh2_ref[...]
    _store_token_tiles(h2_ref, h2)
    logits_t = _dot3(wr_ref[...], h2, _NT)
    e = jnp.exp(logits_t - jnp.max(logits_t, axis=0, keepdims=True))
    aff_ref[...] = e / jnp.sum(e, axis=0, keepdims=True)


def _post_attn(o_p, o_s, x, w_o, mods5, layer, ln_g, ln_b, w_router_t):
    op_spec = pl.BlockSpec((TM, D_MODEL), lambda i: (jnp.minimum(i, PROMPT_TILES - 1), 0))
    os_spec = pl.BlockSpec((TM, D_MODEL), lambda i: (jnp.maximum(i - PROMPT_TILES, 0), 0))
    return pl.pallas_call(
        _post_attn_kernel,
        out_shape=(
            jax.ShapeDtypeStruct((N_TOK, D_MODEL), F32),
            jax.ShapeDtypeStruct((N_TOK * TOK_TILE, LANES), F32),
            jax.ShapeDtypeStruct((N_EXPERTS, N_TOK), F32),
        ),
        grid=(TILES,),
        in_specs=[
            op_spec, os_spec, _row_spec(D_MODEL), _full_spec((D_MODEL, D_MODEL)),
            _mod_spec(layer, 2), _mod_spec(layer, 3), _mod_spec(layer, 4),
            _full_spec((1, D_MODEL)), _full_spec((1, D_MODEL)), _full_spec((N_EXPERTS, D_MODEL)),
        ],
        out_specs=(_row_spec(D_MODEL), _tok_tile_spec(), pl.BlockSpec((N_EXPERTS, TM), lambda i: (0, i))),
        compiler_params=_params(1),
        name="post_attn",
    )(o_p, o_s, x, w_o, mods5, mods5, mods5, ln_g, ln_b, w_router_t)


FFN_ROWS = 512
FFN_STEPS = EXPERT_FF // FF_TILE
ROW_CHUNKS = CAP // FFN_ROWS
GATHER_PER_CHUNK = CAP // (FFN_STEPS * ROW_CHUNKS)


def _token_copy(src, src_tok, dst, dst_tok, sem):
    return pltpu.make_async_copy(src.at[_tok_rows(src_tok), :], dst.at[_tok_rows(dst_tok), :], sem)


def _wait_buffer(buf_ref, sem):
    pltpu.make_async_copy(buf_ref, buf_ref, sem).wait()


def _ffn_kernel(idx_ref, h_hbm, f_in_hbm, wg_ref, wu_ref, wd_ref, gate_ref, f_hbm,
                xbuf, acc, fbuf, gsem, rsem, wsem):
    del f_in_hbm
    e = pl.program_id(0)
    f = pl.program_id(1)
    buf = e % 2
    nxt = jnp.minimum(e + 1, N_EXPERTS - 1)

    def gather(expert, s, b):
        return _token_copy(h_hbm, idx_ref[expert * CAP + s], xbuf.at[b], s, gsem.at[b])

    @pl.when((e == 0) & (f == 0))
    def _():
        def issue(s, carry):
            gather(0, s, 0).start()
            return carry
        lax.fori_loop(0, CAP, issue, 0, unroll=8)

    @pl.when(f == 0)
    def _():
        _wait_buffer(xbuf.at[buf], gsem.at[buf])
        acc[...] = jnp.zeros_like(acc)

    wg = wg_ref[...].astype(BF16)
    wu = wu_ref[...].astype(BF16)
    wd = wd_ref[...].astype(BF16)
    for c in range(ROW_CHUNKS):
        r0 = c * FFN_ROWS
        for j in range(GATHER_PER_CHUNK):
            gather(nxt, (f * ROW_CHUNKS + c) * GATHER_PER_CHUNK + j, 1 - buf).start()
        x = _load_token_tiles(xbuf.at[buf], FFN_ROWS, r0).astype(BF16)
        a = jnp.dot(x, wg, preferred_element_type=F32)
        u = jnp.dot(x, wu, preferred_element_type=F32)
        hid = (a / (1.0 + jnp.exp(-a)) * u).astype(BF16)
        acc[r0:r0 + FFN_ROWS, :] += jnp.dot(hid, wd, preferred_element_type=F32)

    @pl.when(f == FFN_STEPS - 1)
    def _():
        def rows_copy(k, b, to_hbm):
            def body(r, carry):
                tok = idx_ref[e * CAP + k * FFN_ROWS + r]
                if to_hbm:
                    _token_copy(fbuf.at[b], r, f_hbm, tok, wsem.at[b]).start()
                else:
                    _token_copy(f_hbm, tok, fbuf.at[b], r, rsem.at[b]).start()
                return carry
            lax.fori_loop(0, FFN_ROWS, body, 0, unroll=8)

        rows_copy(0, 0, False)
        for k in range(ROW_CHUNKS):
            b = k % 2
            if k + 1 < ROW_CHUNKS:
                if k >= 1:
                    _wait_buffer(fbuf.at[1 - b], wsem.at[1 - b])
                rows_copy(k + 1, 1 - b, False)
            _wait_buffer(fbuf.at[b], rsem.at[b])
            r0 = k * FFN_ROWS
            g = gate_ref[r0:r0 + FFN_ROWS, :]
            for j in range(TOK_TILE):
                rows = _lane_block_rows(j, FFN_ROWS)
                fbuf[b, rows, :] = fbuf[b, rows, :] + acc[r0:r0 + FFN_ROWS, j * LANES:(j + 1) * LANES] * g
            rows_copy(k, b, True)
        for b in range(2):
            _wait_buffer(fbuf.at[b], wsem.at[b])

    @pl.when((e == N_EXPERTS - 1) & (f == FFN_STEPS - 1))
    def _():
        _wait_buffer(xbuf.at[1 - buf], gsem.at[1 - buf])


def _expert_ffn(idx, h2, gate, w_gate, w_up, w_down, layer):
    grid_spec = pltpu.PrefetchScalarGridSpec(
        num_scalar_prefetch=1,
        grid=(N_EXPERTS, FFN_STEPS),
        in_specs=[
            pl.BlockSpec(memory_space=pl.ANY),
            pl.BlockSpec(memory_space=pl.ANY),
            pl.BlockSpec((None, None, D_MODEL, FF_TILE), lambda e, f, idx: (layer, e, 0, f)),
            pl.BlockSpec((None, None, D_MODEL, FF_TILE), lambda e, f, idx: (layer, e, 0, f)),
            pl.BlockSpec((None, None, FF_TILE, D_MODEL), lambda e, f, idx: (layer, e, f, 0)),
            pl.BlockSpec((None, CAP, 1), lambda e, f, idx: (e, 0, 0)),
        ],
        out_specs=pl.BlockSpec(memory_space=pl.ANY),
        scratch_shapes=[
            pltpu.VMEM((2, CAP * TOK_TILE, LANES), F32),
            pltpu.VMEM((CAP, D_MODEL), F32),
            pltpu.VMEM((2, FFN_ROWS * TOK_TILE, LANES), F32),
            pltpu.SemaphoreType.DMA((2,)), pltpu.SemaphoreType.DMA((2,)), pltpu.SemaphoreType.DMA((2,)),
        ],
    )
    f_zero = jnp.zeros((N_TOK * TOK_TILE, LANES), F32)
    return pl.pallas_call(
        _ffn_kernel,
        out_shape=jax.ShapeDtypeStruct((N_TOK * TOK_TILE, LANES), F32),
        grid_spec=grid_spec,
        input_output_aliases={2: 0},
        compiler_params=_params(2),
        name="expert_ffn",
    )(idx, h2, f_zero, w_gate, w_up, w_down, gate)


def _post_moe_kernel(x_ref, f_ref, g2_ref, lng_ref, lnb_ref, o_ref):
    f = _load_token_tiles(f_ref, TM)
    o_ref[...] = _layer_norm(ALPHA * x_ref[...] + g2_ref[...] * f, lng_ref[...], lnb_ref[...])


def _post_moe(x1, f, mods5, layer, ln_g, ln_b):
    return pl.pallas_call(
        _post_moe_kernel,
        out_shape=jax.ShapeDtypeStruct((N_TOK, D_MODEL), F32),
        grid=(TILES,),
        in_specs=[_row_spec(D_MODEL), _tok_tile_spec(), _mod_spec(layer, 5),
                  _full_spec((1, D_MODEL)), _full_spec((1, D_MODEL))],
        out_specs=_row_spec(D_MODEL),
        compiler_params=_params(1),
        name="post_moe",
    )(x1, f, mods5, ln_g, ln_b)


def _axial_tables(dim):
    n_rows = DEC_SEQ // GRID_W
    rows = jnp.repeat(jnp.arange(n_rows, dtype=F32), GRID_W)
    cols = jnp.tile(jnp.arange(GRID_W, dtype=F32), n_rows)
    n_freq = dim // 4
    inv_freq = ROPE_THETA ** (-jnp.arange(n_freq, dtype=F32) / n_freq)
    ang = jnp.concatenate([rows[:, None] * inv_freq, cols[:, None] * inv_freq], -1)
    cos, sin = jnp.cos(ang), jnp.sin(ang)
    return jnp.concatenate([cos, cos], -1), jnp.concatenate([-sin, sin], -1)


def _with_identity_rows(cos_l, sin_l):
    return (jnp.concatenate([jnp.ones((TM, LANES), F32), cos_l], 0),
            jnp.concatenate([jnp.zeros((TM, LANES), F32), sin_l], 0))


def _gqa_rope_tables():
    cos, sin = _axial_tables(HEAD_DIM)
    return _with_identity_rows(jnp.tile(cos, (1, 2)), jnp.tile(sin, (1, 2)))


def _mla_rope_tables():
    cos, sin = _axial_tables(MLA_ROPE_DIM)
    pad_hi = LANES - MLA_ROPE_LO - MLA_ROPE_DIM
    cos_l = jnp.concatenate([jnp.ones((DEC_SEQ, MLA_ROPE_LO), F32), cos, jnp.ones((DEC_SEQ, pad_hi), F32)], 1)
    sin_l = jnp.pad(sin, ((0, 0), (MLA_ROPE_LO, pad_hi)))
    return _with_identity_rows(cos_l, sin_l)


def _moe(h2, aff_t, w_gate, w_up, w_down, layer):
    gate_p, idx_p = lax.top_k(aff_t[:, :N_PROMPT], CAP_P)
    gate_s, idx_s = lax.top_k(aff_t[:, N_PROMPT:], CAP_S)
    idx = jnp.concatenate([idx_p, idx_s + N_PROMPT], axis=1).reshape(-1)
    gate = jnp.concatenate([gate_p, gate_s], axis=1)[..., None]
    return _expert_ffn(idx, h2, gate, w_gate, w_up, w_down, layer)


def kernel(x_prompt, x_sample, cache_gqa_k, cache_gqa_v, cache_mla_ckv, cache_mla_kpe, c, c_ctx,
           ada_w, ada_b, ln_g, ln_b, a_w_qkv, a_q_norm, a_k_norm, a_w_o, b_w_qkv, b_sink, b_w_o,
           mla_w_down, mla_q_norm, mla_w_uq, mla_kv_norm, mla_w_ukv, mla_w_o,
           moe_w_router, moe_w_gate, moe_w_up, moe_w_down):
    x = jnp.concatenate([x_prompt.reshape(N_PROMPT, D_MODEL), x_sample.reshape(N_SAMPLE, D_MODEL)], 0)
    cond = jnp.concatenate([c_ctx[None], c, jnp.zeros((N_COND - 1 - DEC_BATCH, D_MODEL), F32)], 0)
    mods5 = _adaln(cond, ada_w, ada_b).reshape(DEPTH, 6, N_COND, 1, D_MODEL)
    gqa_cos, gqa_sin = _gqa_rope_tables()
    mla_cos, mla_sin = _mla_rope_tables()

    new_k, new_v, new_ckv, new_kpe = [], [], [], []
    gqa_slot = 0
    for i in range(DEPTH):
        kind, j = i % 3, i // 3
        if kind in (0, 1):
            if kind == 0:
                w_qkv, w_o, sink = a_w_qkv[j], a_w_o[j], None
                gain = jnp.concatenate([jnp.tile(a_q_norm[j], N_HEADS), jnp.tile(a_k_norm[j], N_KV_HEADS)])[None]
            else:
                w_qkv, w_o, sink = b_w_qkv[j], b_w_o[j], b_sink[j]
                gain = jnp.ones((1, QK_WIDTH), F32)
            q, kh, vh, k32, v32 = _gqa_proj(x, mods5, i, w_qkv.astype(BF16), gain, gqa_cos, gqa_sin,
                                            use_norm=(kind == 0))
            o_p = _gqa_attention(q, kh, vh, None, None, gqa_slot, sink, latent=False, window=False)
            o_s = _gqa_attention(q, kh, vh, cache_gqa_k, cache_gqa_v, gqa_slot, sink,
                                 latent=True, window=(kind == 1))
            for store, arr in ((new_k, k32), (new_v, v32)):
                store.append(arr[:, :N_PROMPT].reshape(N_KV_HEADS, BATCH, SEQ, HEAD_DIM).transpose(1, 0, 2, 3))
            gqa_slot += 1
        else:
            w_down = mla_w_down[j]
            rope_lo = Q_RANK + KV_RANK
            wd_ext = jnp.concatenate([
                w_down[:, :rope_lo], jnp.zeros((D_MODEL, MLA_ROPE_LO), F32), w_down[:, rope_lo:],
                jnp.zeros((D_MODEL, LANES - MLA_ROPE_LO - MLA_ROPE_DIM), F32)], 1).astype(BF16)
            wuq_pad = jnp.pad(mla_w_uq[j].reshape(Q_RANK, MLA_HEADS, MLA_QK_DIM),
                              ((0, 0), (0, 0), (0, LANES - MLA_QK_DIM))).reshape(Q_RANK, MLA_HEADS * LANES)
            w_ukv = mla_w_ukv[j].reshape(KV_RANK, MLA_HEADS, MLA_NOPE_DIM + MLA_V_DIM)
            wk_pad = jnp.pad(w_ukv[:, :, :MLA_NOPE_DIM], ((0, 0), (0, 0), (0, LANES - MLA_NOPE_DIM)))
            wk_pad = wk_pad.reshape(KV_RANK, MLA_HEADS * LANES).astype(BF16)
            wv = w_ukv[:, :, MLA_NOPE_DIM:].reshape(KV_RANK, MLA_HEADS * MLA_V_DIM).astype(BF16)
            q, ckv, kpe = _mla_proj(x, mods5, i, wd_ext, mla_q_norm[j][None], wuq_pad.astype(BF16),
                                    mla_kv_norm[j][None], mla_cos, mla_sin)
            c_all = jnp.concatenate([ckv, cache_mla_ckv[:, j].reshape(DEC_BATCH * PAST_LEN, KV_RANK)], 0)
            kpe_ctx = jnp.pad(cache_mla_kpe[:, j].reshape(DEC_BATCH * PAST_LEN, MLA_ROPE_DIM),
                              ((0, 0), (MLA_ROPE_LO, LANES - MLA_ROPE_LO - MLA_ROPE_DIM)))
            kpe_all = jnp.concatenate([kpe, kpe_ctx], 0)
            k_all, v_all = _mla_kv(c_all, kpe_all, wk_pad, wv)
            o_p = _mla_attention(q, k_all, v_all, latent=False)
            o_s = _mla_attention(q, k_all, v_all, latent=True)
            w_o = mla_w_o[j]
            new_ckv.append(ckv[:N_PROMPT].reshape(BATCH, SEQ, KV_RANK))
            new_kpe.append(kpe[:N_PROMPT, MLA_ROPE_LO:MLA_ROPE_LO + MLA_ROPE_DIM].reshape(BATCH, SEQ, MLA_ROPE_DIM))
        x1, h2, aff_t = _post_attn(o_p, o_s, x, w_o.astype(BF16), mods5, i, ln_g[i, 0][None], ln_b[i, 0][None],
                                   moe_w_router[i].T)
        f = _moe(h2, aff_t, moe_w_gate, moe_w_up, moe_w_down, i)
        x = _post_moe(x1, f, mods5, i, ln_g[i, 1][None], ln_b[i, 1][None])

    y_prompt = x[:N_PROMPT].reshape(BATCH, SEQ, D_MODEL)
    y_sample = x[N_PROMPT:].reshape(DEC_BATCH, DEC_SEQ, D_MODEL)
    return (y_prompt, y_sample, jnp.stack(new_k, 1), jnp.stack(new_v, 1),
            jnp.stack(new_ckv, 1), jnp.stack(new_kpe, 1))
```

```python
import functools
import math

import jax
import jax.numpy as jnp
from jax import lax
from jax.experimental import pallas as pl
from jax.experimental.pallas import tpu as pltpu

F32 = jnp.float32
BF16 = jnp.bfloat16

D_MODEL = 1024
BATCH = 16
SEQ = 256
DEPTH = 4
DEC_BATCH = 4
DEC_SEQ = 2048
PAST_LEN = 512
GRID_W = 64
N_HEADS = 16
N_KV_HEADS = 4
GROUP = N_HEADS // N_KV_HEADS
HEAD_DIM = 64
WINDOW = 128
MLA_HEADS = 16
Q_RANK = 256
KV_RANK = 128
MLA_NOPE_DIM = 64
MLA_ROPE_DIM = 32
MLA_V_DIM = 64
MLA_QK_DIM = MLA_NOPE_DIM + MLA_ROPE_DIM
MLA_SCALE = MLA_QK_DIM ** -0.5
GQA_SCALE = HEAD_DIM ** -0.5
N_EXPERTS = 16
EXPERT_FF = 2048
CAPACITY_FACTOR = 2
ROPE_THETA = 10000.0
NORM_EPS = 1e-6
LN_EPS = 1e-5
ALPHA = (2 * DEPTH) ** 0.25
LOG2E = math.log2(math.e)

LANES = 128
N_PROMPT = BATCH * SEQ
N_SAMPLE = DEC_BATCH * DEC_SEQ
N_TOK = N_PROMPT + N_SAMPLE
TM = 256
PROMPT_TILES = N_PROMPT // TM
TILES = N_TOK // TM
SAMPLE_TILES_PER_REQ = DEC_SEQ // TM
N_COND = 8
CAP_P = CAPACITY_FACTOR * N_PROMPT // N_EXPERTS
CAP_S = CAPACITY_FACTOR * N_SAMPLE // N_EXPERTS
CAP = CAP_P + CAP_S
FF_TILE = 512
KV_CHUNK = 512
GQA_TQ = 512
MLA_TQ = 1024

_NT = (((1,), (1,)), ((), ()))
_VMEM_LIMIT = 56 * 1024 * 1024


def _params(n_axes, vmem=_VMEM_LIMIT):
    return pltpu.CompilerParams(dimension_semantics=("arbitrary",) * n_axes, vmem_limit_bytes=vmem)


def _cond_of_tile(i):
    return jnp.where(i < PROMPT_TILES, 0, 1 + (i - PROMPT_TILES) // SAMPLE_TILES_PER_REQ)


def _pos_block_of_tile(i):
    return jnp.where(i < PROMPT_TILES, 0, 1 + (i - PROMPT_TILES) % SAMPLE_TILES_PER_REQ)


def _mod_spec(layer, k):
    return pl.BlockSpec((None, None, None, 1, D_MODEL), lambda i: (layer, k, _cond_of_tile(i), 0, 0))


def _row_spec(width):
    return pl.BlockSpec((TM, width), lambda i: (i, 0))


def _full_spec(shape):
    nd = len(shape)
    return pl.BlockSpec(shape, lambda *_: (0,) * nd)


TOK_TILE = 8


def _tok_tile_spec():
    return pl.BlockSpec((TM * TOK_TILE, LANES), lambda i: (i, 0))


def _tok_rows(t):
    return pl.ds(pl.multiple_of(t * TOK_TILE, TOK_TILE), TOK_TILE)


def _lane_block_rows(j, n_tokens, t0=0):
    return pl.ds(t0 * TOK_TILE + j, n_tokens, stride=TOK_TILE)


def _store_token_tiles(ref, val):
    for j in range(TOK_TILE):
        ref[_lane_block_rows(j, val.shape[0]), :] = val[:, j * LANES:(j + 1) * LANES]


def _load_token_tiles(ref, n_tokens, t0=0):
    return jnp.concatenate([ref[_lane_block_rows(j, n_tokens, t0), :] for j in range(TOK_TILE)], axis=-1)


def _split_bf16(a):
    hi = a.astype(BF16)
    lo = (a - hi.astype(F32)).astype(BF16)
    return hi, lo


def _dot3(a, b, dims=(((1,), (0,)), ((), ()))):
    a_hi, a_lo = _split_bf16(a)
    b_hi, b_lo = _split_bf16(b)
    dg = functools.partial(lax.dot_general, dimension_numbers=dims, preferred_element_type=F32)
    return dg(a_hi, b_hi) + dg(a_lo, b_hi) + dg(a_hi, b_lo)


def _layer_norm(y, g, b):
    mu = jnp.mean(y, axis=-1, keepdims=True)
    d = y - mu
    var = jnp.mean(d * d, axis=-1, keepdims=True)
    return d * lax.rsqrt(var + LN_EPS) * g + b


def _rope_lanes(t, cos, sin, half):
    lane = lax.broadcasted_iota(jnp.int32, t.shape, 1)
    first = (lane & (2 * half - 1)) < half
    partner = jnp.where(first, pltpu.roll(t, LANES - half, 1), pltpu.roll(t, half, 1))
    return t * cos + partner * sin


def _pair_rms(t, gain):
    sq = t * t
    lane = lax.broadcasted_iota(jnp.int32, t.shape, 1)
    lo = lane < HEAD_DIM
    s_lo = jnp.sum(jnp.where(lo, sq, 0.0), axis=-1, keepdims=True)
    s_hi = jnp.sum(jnp.where(lo, 0.0, sq), axis=-1, keepdims=True)
    ms = jnp.where(lo, s_lo, s_hi) * (1.0 / HEAD_DIM)
    return t * lax.rsqrt(ms + NORM_EPS) * gain


def _adaln_kernel(c_ref, w_ref, b_ref, o_ref):
    c = c_ref[...]
    a = c / (1.0 + jnp.exp(-c))
    o_ref[...] = _dot3(a, w_ref[...]) + b_ref[...]


def _adaln(cond, ada_w, ada_b):
    return pl.pallas_call(
        _adaln_kernel,
        out_shape=jax.ShapeDtypeStruct((DEPTH, 6, N_COND, D_MODEL), F32),
        grid=(DEPTH, 6),
        in_specs=[
            pl.BlockSpec((N_COND, D_MODEL), lambda l, k: (0, 0)),
            pl.BlockSpec((None, D_MODEL, D_MODEL), lambda l, k: (l, 0, k)),
            pl.BlockSpec((None, None, 1, D_MODEL), lambda l, k: (l, k, 0, 0)),
        ],
        out_specs=pl.BlockSpec((None, None, N_COND, D_MODEL), lambda l, k: (l, k, 0, 0)),
        compiler_params=_params(2),
        name="adaln",
    )(cond, ada_w, ada_b.reshape(DEPTH, 6, 1, D_MODEL))


QK_WIDTH = (N_HEADS + N_KV_HEADS) * HEAD_DIM
QKV_WIDTH = (N_HEADS + 2 * N_KV_HEADS) * HEAD_DIM
Q_WIDTH = N_HEADS * HEAD_DIM


def _gqa_proj_kernel(x_ref, sh_ref, sc_ref, w_ref, gain_ref, cos_ref, sin_ref,
                     q_ref, kh_ref, vh_ref, k32_ref, v32_ref, *, use_norm):
    h = (x_ref[...] * (1.0 + sc_ref[...]) + sh_ref[...]).astype(BF16)
    acc = jnp.dot(h, w_ref[...], preferred_element_type=F32)
    cos = cos_ref[...]
    sin = sin_ref[...]
    for j in range(QK_WIDTH // LANES):
        t = acc[:, j * LANES:(j + 1) * LANES]
        if use_norm:
            t = _pair_rms(t, gain_ref[:, j * LANES:(j + 1) * LANES])
        t = _rope_lanes(t, cos, sin, HEAD_DIM // 2)
        if j < Q_WIDTH // LANES:
            q_ref[:, j * LANES:(j + 1) * LANES] = (t * GQA_SCALE).astype(BF16)
        else:
            for half in range(2):
                kv_head = 2 * (j - Q_WIDTH // LANES) + half
                th = t[:, half * HEAD_DIM:(half + 1) * HEAD_DIM]
                kh_ref[kv_head] = th.astype(BF16)
                k32_ref[kv_head] = th
    for kv_head in range(N_KV_HEADS):
        lo = QK_WIDTH + kv_head * HEAD_DIM
        tv = acc[:, lo:lo + HEAD_DIM]
        vh_ref[kv_head] = tv.astype(BF16)
        v32_ref[kv_head] = tv


def _gqa_proj(x, mods5, layer, w_qkv, gain, cos_t, sin_t, use_norm):
    head_spec = pl.BlockSpec((N_KV_HEADS, TM, HEAD_DIM), lambda i: (0, i, 0))
    pos_spec = pl.BlockSpec((TM, LANES), lambda i: (_pos_block_of_tile(i), 0))
    return pl.pallas_call(
        functools.partial(_gqa_proj_kernel, use_norm=use_norm),
        out_shape=(
            jax.ShapeDtypeStruct((N_TOK, Q_WIDTH), BF16),
            jax.ShapeDtypeStruct((N_KV_HEADS, N_TOK, HEAD_DIM), BF16),
            jax.ShapeDtypeStruct((N_KV_HEADS, N_TOK, HEAD_DIM), BF16),
            jax.ShapeDtypeStruct((N_KV_HEADS, N_TOK, HEAD_DIM), F32),
            jax.ShapeDtypeStruct((N_KV_HEADS, N_TOK, HEAD_DIM), F32),
        ),
        grid=(TILES,),
        in_specs=[
            _row_spec(D_MODEL), _mod_spec(layer, 0), _mod_spec(layer, 1),
            _full_spec((D_MODEL, QKV_WIDTH)), _full_spec((1, QK_WIDTH)),
            pos_spec, pos_spec,
        ],
        out_specs=(_row_spec(Q_WIDTH), head_spec, head_spec, head_spec, head_spec),
        compiler_params=_params(1),
        name="gqa_proj",
    )(x, mods5, mods5, w_qkv, gain, cos_t, sin_t)


def _attend(qs, chunks, sink_col, c1):
    m = den = acc = None
    for k_fn, v_fn, bias, _ in chunks:
        s = lax.dot_general(qs, k_fn(), _NT, preferred_element_type=F32)
        if bias is not None:
            s = s + bias
        mc = jnp.max(s, axis=-1, keepdims=True)
        m_new = mc if m is None else jnp.maximum(m, mc)
        p = jnp.exp2((s - m_new) * c1)
        ps = jnp.sum(p, axis=-1, keepdims=True)
        pv = jnp.dot(p.astype(BF16), v_fn(), preferred_element_type=F32)
        if m is None:
            den, acc = ps, pv
        else:
            alpha = jnp.exp2((m - m_new) * c1)
            den = den * alpha + ps
            acc = acc * alpha + pv
        m = m_new
    if sink_col is not None:
        den = den + jnp.exp2((sink_col - m) * c1)
    return acc / den


def _chunk_list(k_ref, v_ref, length, cast):
    out = []
    width = min(KV_CHUNK, length)
    for c0 in range(0, length, width):
        if cast:
            k_fn = lambda c0=c0: k_ref[c0:c0 + width, :].astype(BF16)
            v_fn = lambda c0=c0: v_ref[c0:c0 + width, :].astype(BF16)
        else:
            k_fn = lambda c0=c0: k_ref[c0:c0 + width, :]
            v_fn = lambda c0=c0: v_ref[c0:c0 + width, :]
        out.append((k_fn, v_fn, None, width))
    return out


def _gqa_attn_kernel(*refs, tq, n_keys, has_ctx, use_sink, window):
    refs = list(refs)
    q_ref, k_ref, v_ref = refs[:3]
    pos = 3
    if has_ctx:
        kc_ref, vc_ref = refs[pos:pos + 2]
        pos += 2
    if use_sink:
        sink_ref = refs[pos]
        pos += 1
    o_ref = refs[pos]
    g = pl.program_id(1)
    qi = pl.program_id(2)

    qf = q_ref[...].astype(F32)
    qs = jnp.concatenate(
        [qf[:, j * HEAD_DIM:(j + 1) * HEAD_DIM] for j in range(GROUP)], axis=0).astype(BF16)

    if window:
        span = tq + 2 * WINDOW
        kstart = pl.multiple_of(jnp.clip(qi * tq - WINDOW, 0, n_keys - span), LANES)
        qpos = qi * tq + lax.broadcasted_iota(jnp.int32, (tq, span), 0)
        kpos = kstart + lax.broadcasted_iota(jnp.int32, (tq, span), 1)
        band = jnp.where(jnp.abs(kpos - qpos) <= WINDOW, 0.0, -jnp.inf).astype(F32)
        bias = jnp.concatenate([band] * GROUP, axis=0)
        chunks = [(lambda: k_ref[pl.ds(kstart, span), :], lambda: v_ref[pl.ds(kstart, span), :], bias, span)]
    else:
        chunks = _chunk_list(k_ref, v_ref, n_keys, cast=False)
    if has_ctx:
        chunks += _chunk_list(kc_ref, vc_ref, PAST_LEN, cast=True)

    sink_col = None
    if use_sink:
        row = lax.broadcasted_iota(jnp.int32, (GROUP * tq, 1), 0)
        sink_col = jnp.full((GROUP * tq, 1), sink_ref[g * GROUP + GROUP - 1], F32)
        for j in reversed(range(GROUP - 1)):
            sink_col = jnp.where(row < (j + 1) * tq, sink_ref[g * GROUP + j], sink_col)

    o = _attend(qs, chunks, sink_col, LOG2E)
    o_ref[...] = jnp.concatenate([o[j * tq:(j + 1) * tq, :] for j in range(GROUP)], axis=-1).astype(BF16)


def _gqa_attention(q, kh, vh, ctx_k, ctx_v, slot, sink, *, latent, window):
    tq = GQA_TQ if latent and not window else TM
    if latent:
        n_b, n_keys, n_qt = DEC_BATCH, DEC_SEQ, DEC_SEQ // tq
        row0 = N_PROMPT // tq
        kv_blk0 = N_PROMPT // DEC_SEQ
    else:
        n_b, n_keys, n_qt = BATCH, SEQ, 1
        row0 = 0
        kv_blk0 = 0
    q_spec = pl.BlockSpec((tq, GROUP * HEAD_DIM), lambda b, g, t: (row0 + b * n_qt + t, g))
    kv_spec = pl.BlockSpec((None, n_keys, HEAD_DIM), lambda b, g, t: (g, kv_blk0 + b, 0))
    in_specs = [q_spec, kv_spec, kv_spec]
    args = [q, kh, vh]
    if latent:
        ctx_spec = pl.BlockSpec((None, None, None, PAST_LEN, HEAD_DIM), lambda b, g, t: (b, slot, g, 0, 0))
        in_specs += [ctx_spec, ctx_spec]
        args += [ctx_k, ctx_v]
    use_sink = sink is not None
    if use_sink:
        in_specs.append(pl.BlockSpec(memory_space=pltpu.SMEM))
        args.append(sink)
    return pl.pallas_call(
        functools.partial(_gqa_attn_kernel, tq=tq, n_keys=n_keys, has_ctx=latent,
                          use_sink=use_sink, window=window),
        out_shape=jax.ShapeDtypeStruct((n_b * n_keys, Q_WIDTH), BF16),
        grid=(n_b, N_KV_HEADS, n_qt),
        in_specs=in_specs,
        out_specs=pl.BlockSpec((tq, GROUP * HEAD_DIM), lambda b, g, t: (b * n_qt + t, g)),
        compiler_params=_params(3),
        name="gqa_attn_latent" if latent else "gqa_attn_context",
    )(*args)


MLA_DOWN_EXT = Q_RANK + KV_RANK + LANES
MLA_HEAD_LANES = LANES
MLA_ROPE_LO = MLA_NOPE_DIM


def _mla_proj_kernel(x_ref, sh_ref, sc_ref, wd_ref, qg_ref, wuq_ref, kvg_ref, cos_ref, sin_ref,
                     q_ref, ckv_ref, kpe_ref):
    h = (x_ref[...] * (1.0 + sc_ref[...]) + sh_ref[...]).astype(BF16)
    acc = jnp.dot(h, wd_ref[...], preferred_element_type=F32)
    cos = cos_ref[...]
    sin = sin_ref[...]
    qa = acc[:, :Q_RANK]
    qn = qa * lax.rsqrt(jnp.mean(qa * qa, axis=-1, keepdims=True) + NORM_EPS) * qg_ref[...]
    q = jnp.dot(qn.astype(BF16), wuq_ref[...], preferred_element_type=F32)
    for hd in range(MLA_HEADS):
        t = q[:, hd * LANES:(hd + 1) * LANES]
        q_ref[:, hd * LANES:(hd + 1) * LANES] = _rope_lanes(t, cos, sin, MLA_ROPE_DIM // 2).astype(BF16)
    ckv = acc[:, Q_RANK:Q_RANK + KV_RANK]
    ckv_ref[...] = ckv * lax.rsqrt(jnp.mean(ckv * ckv, axis=-1, keepdims=True) + NORM_EPS) * kvg_ref[...]
    kpe_ref[...] = _rope_lanes(acc[:, Q_RANK + KV_RANK:], cos, sin, MLA_ROPE_DIM // 2)


def _mla_proj(x, mods5, layer, wd_ext, q_gain, wuq_pad, kv_gain, cos_t, sin_t):
    pos_spec = pl.BlockSpec((TM, LANES), lambda i: (_pos_block_of_tile(i), 0))
    return pl.pallas_call(
        _mla_proj_kernel,
        out_shape=(
            jax.ShapeDtypeStruct((N_TOK, MLA_HEADS * LANES), BF16),
            jax.ShapeDtypeStruct((N_TOK, KV_RANK), F32),
            jax.ShapeDtypeStruct((N_TOK, LANES), F32),
        ),
        grid=(TILES,),
        in_specs=[
            _row_spec(D_MODEL), _mod_spec(layer, 0), _mod_spec(layer, 1),
            _full_spec((D_MODEL, MLA_DOWN_EXT)), _full_spec((1, Q_RANK)),
            _full_spec((Q_RANK, MLA_HEADS * LANES)), _full_spec((1, KV_RANK)),
            pos_spec, pos_spec,
        ],
        out_specs=(_row_spec(MLA_HEADS * LANES), _row_spec(KV_RANK), _row_spec(LANES)),
        compiler_params=_params(1),
        name="mla_proj",
    )(x, mods5, mods5, wd_ext, q_gain, wuq_pad, kv_gain, cos_t, sin_t)


def _mla_kv_kernel(c_ref, kpe_ref, wk_ref, wv_ref, k_ref, v_ref):
    c = c_ref[...].astype(BF16)
    kk = jnp.dot(c, wk_ref[...], preferred_element_type=F32)
    vv = jnp.dot(c, wv_ref[...], preferred_element_type=F32)
    kpe = kpe_ref[...]
    for hd in range(MLA_HEADS):
        k_ref[hd] = (kk[:, hd * LANES:(hd + 1) * LANES] + kpe).astype(BF16)
        v_ref[hd] = vv[:, hd * MLA_V_DIM:(hd + 1) * MLA_V_DIM].astype(BF16)


def _mla_kv(c_all, kpe_all, wk_pad, wv):
    n_rows = c_all.shape[0]
    return pl.pallas_call(
        _mla_kv_kernel,
        out_shape=(
            jax.ShapeDtypeStruct((MLA_HEADS, n_rows, LANES), BF16),
            jax.ShapeDtypeStruct((MLA_HEADS, n_rows, MLA_V_DIM), BF16),
        ),
        grid=(n_rows // TM,),
        in_specs=[
            _row_spec(KV_RANK), _row_spec(LANES),
            _full_spec((KV_RANK, MLA_HEADS * LANES)), _full_spec((KV_RANK, MLA_HEADS * MLA_V_DIM)),
        ],
        out_specs=(
            pl.BlockSpec((MLA_HEADS, TM, LANES), lambda i: (0, i, 0)),
            pl.BlockSpec((MLA_HEADS, TM, MLA_V_DIM), lambda i: (0, i, 0)),
        ),
        compiler_params=_params(1),
        name="mla_kv",
    )(c_all, kpe_all, wk_pad, wv)


def _mla_attn_kernel(*refs, tq, n_keys, has_ctx):
    refs = list(refs)
    q_ref, k_ref, v_ref = refs[:3]
    pos = 3
    if has_ctx:
        kc_ref, vc_ref = refs[pos:pos + 2]
        pos += 2
    o_ref = refs[pos]
    outs = []
    for j in range(2):
        qs = q_ref[:, j * LANES:(j + 1) * LANES]
        chunks = _chunk_list(k_ref.at[j], v_ref.at[j], n_keys, cast=False)
        if has_ctx:
            chunks += _chunk_list(kc_ref.at[j], vc_ref.at[j], PAST_LEN, cast=False)
        outs.append(_attend(qs, chunks, None, MLA_SCALE * LOG2E))
    o_ref[...] = jnp.concatenate(outs, axis=-1).astype(BF16)


def _mla_attention(q, k_all, v_all, *, latent):
    if latent:
        tq = MLA_TQ
        n_b, n_keys, n_qt = DEC_BATCH, DEC_SEQ, DEC_SEQ // tq
        row0 = N_PROMPT // tq
        kv_blk0 = N_PROMPT // DEC_SEQ
    else:
        tq = TM
        n_b, n_keys, n_qt = BATCH, SEQ, 1
        row0 = 0
        kv_blk0 = 0
    q_spec = pl.BlockSpec((tq, 2 * LANES), lambda b, hp, t: (row0 + b * n_qt + t, hp))
    k_spec = pl.BlockSpec((2, n_keys, LANES), lambda b, hp, t: (hp, kv_blk0 + b, 0))
    v_spec = pl.BlockSpec((2, n_keys, MLA_V_DIM), lambda b, hp, t: (hp, kv_blk0 + b, 0))
    in_specs = [q_spec, k_spec, v_spec]
    args = [q, k_all, v_all]
    if latent:
        ctx0 = N_TOK // PAST_LEN
        in_specs += [
            pl.BlockSpec((2, PAST_LEN, LANES), lambda b, hp, t: (hp, ctx0 + b, 0)),
            pl.BlockSpec((2, PAST_LEN, MLA_V_DIM), lambda b, hp, t: (hp, ctx0 + b, 0)),
        ]
        args += [k_all, v_all]
    return pl.pallas_call(
        functools.partial(_mla_attn_kernel, tq=tq, n_keys=n_keys, has_ctx=latent),
        out_shape=jax.ShapeDtypeStruct((n_b * n_keys, MLA_HEADS * MLA_V_DIM), BF16),
        grid=(n_b, MLA_HEADS // 2, n_qt),
        in_specs=in_specs,
        out_specs=pl.BlockSpec((tq, 2 * MLA_V_DIM), lambda b, hp, t: (b * n_qt + t, hp)),
        compiler_params=_params(3),
        name="mla_attn_latent" if latent else "mla_attn_context",
    )(*args)


def _post_attn_kernel(op_ref, os_ref, x_ref, wo_ref, g1_ref, sh2_ref, sc2_ref, lng_ref, lnb_ref, wr_ref,
                      x1_ref, h2_ref, aff_ref):
    o = jnp.where(pl.program_id(0) < PROMPT_TILES, op_ref[...], os_ref[...])
    proj = jnp.dot(o, wo_ref[...], preferred_element_type=F32)
    x1 = _layer_norm(ALPHA * x_ref[...] + g1_ref[...] * proj, lng_ref[...], lnb_ref[...])
    x1_ref[...] = x1
    h2 = x1 * (1.0 + sc2_ref[...]) + sh2_ref[...]
    _store_token_tiles(h2_ref, h2)
    logits_t = _dot3(wr_ref[...], h2, _NT)
    e = jnp.exp(logits_t - jnp.max(logits_t, axis=0, keepdims=True))
    aff_ref[...] = e / jnp.sum(e, axis=0, keepdims=True)


def _post_attn(o_p, o_s, x, w_o, mods5, layer, ln_g, ln_b, w_router_t):
    op_spec = pl.BlockSpec((TM, D_MODEL), lambda i: (jnp.minimum(i, PROMPT_TILES - 1), 0))
    os_spec = pl.BlockSpec((TM, D_MODEL), lambda i: (jnp.maximum(i - PROMPT_TILES, 0), 0))
    return pl.pallas_call(
        _post_attn_kernel,
        out_shape=(
            jax.ShapeDtypeStruct((N_TOK, D_MODEL), F32),
            jax.ShapeDtypeStruct((N_TOK * TOK_TILE, LANES), F32),
            jax.ShapeDtypeStruct((N_EXPERTS, N_TOK), F32),
        ),
        grid=(TILES,),
        in_specs=[
            op_spec, os_spec, _row_spec(D_MODEL), _full_spec((D_MODEL, D_MODEL)),
            _mod_spec(layer, 2), _mod_spec(layer, 3), _mod_spec(layer, 4),
            _full_spec((1, D_MODEL)), _full_spec((1, D_MODEL)), _full_spec((N_EXPERTS, D_MODEL)),
        ],
        out_specs=(_row_spec(D_MODEL), _tok_tile_spec(), pl.BlockSpec((N_EXPERTS, TM), lambda i: (0, i))),
        compiler_params=_params(1),
        name="post_attn",
    )(o_p, o_s, x, w_o, mods5, mods5, mods5, ln_g, ln_b, w_router_t)


FFN_ROWS = 512
FFN_STEPS = EXPERT_FF // FF_TILE
ROW_CHUNKS = CAP // FFN_ROWS
GATHER_PER_CHUNK = CAP // (FFN_STEPS * ROW_CHUNKS)


N_DMA_THREADS = 2


def _token_copy(src, src_tok, dst, dst_tok, sem):
    return pltpu.make_async_copy(src.at[_tok_rows(src_tok), :], dst.at[_tok_rows(dst_tok), :], sem)


def _wait_buffer(buf_ref, sem):
    pltpu.make_async_copy(buf_ref, buf_ref, sem).wait()


def _ffn_kernel(idx_ref, h_hbm, f_in_hbm, wg_ref, wu_ref, wd_ref, gate_ref, f_hbm,
                xbuf, acc, fbuf, gsem, rsem, wsem):
    del f_in_hbm
    e = pl.program_id(0)
    f = pl.program_id(1)
    buf = e % 2
    nxt = jnp.minimum(e + 1, N_EXPERTS - 1)

    def gather(expert, s, b):
        return _token_copy(h_hbm, idx_ref[expert * CAP + s], xbuf.at[b], s, gsem.at[b])

    @pl.when((e == 0) & (f == 0))
    def _():
        def issue(s, carry):
            gather(0, s, 0).start()
            return carry
        lax.fori_loop(0, CAP, issue, 0, unroll=8)

    @pl.when(f == 0)
    def _():
        _wait_buffer(xbuf.at[buf], gsem.at[buf])
        acc[...] = jnp.zeros_like(acc)

    def rows_copy(r_lo, n_rows, to_hbm):
        def body(i, carry):
            for thread in range(N_DMA_THREADS):
                r = r_lo + i * N_DMA_THREADS + thread
                tok = idx_ref[e * CAP + r]
                if to_hbm:
                    _token_copy(fbuf, r, f_hbm, tok, wsem.at[0]).start(priority=thread)
                else:
                    _token_copy(f_hbm, tok, fbuf, r, rsem.at[0]).start(priority=thread)
            return carry
        lax.fori_loop(0, n_rows // N_DMA_THREADS, body, 0, unroll=4)

    @pl.when(f == FFN_STEPS - 2)
    def _():
        rows_copy(0, CAP, False)

    wg = wg_ref[...].astype(BF16)
    wu = wu_ref[...].astype(BF16)
    wd = wd_ref[...].astype(BF16)
    for c in range(ROW_CHUNKS):
        r0 = c * FFN_ROWS
        for j in range(GATHER_PER_CHUNK):
            gather(nxt, (f * ROW_CHUNKS + c) * GATHER_PER_CHUNK + j, 1 - buf).start(priority=j % N_DMA_THREADS)
        x = _load_token_tiles(xbuf.at[buf], FFN_ROWS, r0).astype(BF16)
        a = jnp.dot(x, wg, preferred_element_type=F32)
        u = jnp.dot(x, wu, preferred_element_type=F32)
        hid = (a / (1.0 + jnp.exp(-a)) * u).astype(BF16)
        acc[r0:r0 + FFN_ROWS, :] += jnp.dot(hid, wd, preferred_element_type=F32)

    @pl.when(f == FFN_STEPS - 1)
    def _():
        _wait_buffer(fbuf, rsem.at[0])
        for k in range(ROW_CHUNKS):
            r0 = k * FFN_ROWS
            g = gate_ref[r0:r0 + FFN_ROWS, :]
            for j in range(TOK_TILE):
                rows = _lane_block_rows(j, FFN_ROWS, r0)
                fbuf[rows, :] = fbuf[rows, :] + acc[r0:r0 + FFN_ROWS, j * LANES:(j + 1) * LANES] * g
            rows_copy(r0, FFN_ROWS, True)
        _wait_buffer(fbuf, wsem.at[0])

    @pl.when((e == N_EXPERTS - 1) & (f == FFN_STEPS - 1))
    def _():
        _wait_buffer(xbuf.at[1 - buf], gsem.at[1 - buf])


def _expert_ffn(idx, h2, gate, w_gate, w_up, w_down, layer):
    grid_spec = pltpu.PrefetchScalarGridSpec(
        num_scalar_prefetch=1,
        grid=(N_EXPERTS, FFN_STEPS),
        in_specs=[
            pl.BlockSpec(memory_space=pl.ANY),
            pl.BlockSpec(memory_space=pl.ANY),
            pl.BlockSpec((None, None, D_MODEL, FF_TILE), lambda e, f, idx: (layer, e, 0, f)),
            pl.BlockSpec((None, None, D_MODEL, FF_TILE), lambda e, f, idx: (layer, e, 0, f)),
            pl.BlockSpec((None, None, FF_TILE, D_MODEL), lambda e, f, idx: (layer, e, f, 0)),
            pl.BlockSpec((None, CAP, 1), lambda e, f, idx: (e, 0, 0)),
        ],
        out_specs=pl.BlockSpec(memory_space=pl.ANY),
        scratch_shapes=[
            pltpu.VMEM((2, CAP * TOK_TILE, LANES), F32),
            pltpu.VMEM((CAP, D_MODEL), F32),
            pltpu.VMEM((CAP * TOK_TILE, LANES), F32),
            pltpu.SemaphoreType.DMA((2,)), pltpu.SemaphoreType.DMA((1,)), pltpu.SemaphoreType.DMA((1,)),
        ],
    )
    f_zero = jnp.zeros((N_TOK * TOK_TILE, LANES), F32)
    return pl.pallas_call(
        _ffn_kernel,
        out_shape=jax.ShapeDtypeStruct((N_TOK * TOK_TILE, LANES), F32),
        grid_spec=grid_spec,
        input_output_aliases={2: 0},
        compiler_params=_params(2),
        name="expert_ffn",
    )(idx, h2, f_zero, w_gate, w_up, w_down, gate)


def _post_moe_kernel(x_ref, f_ref, g2_ref, lng_ref, lnb_ref, o_ref):
    f = _load_token_tiles(f_ref, TM)
    o_ref[...] = _layer_norm(ALPHA * x_ref[...] + g2_ref[...] * f, lng_ref[...], lnb_ref[...])


def _post_moe(x1, f, mods5, layer, ln_g, ln_b):
    return pl.pallas_call(
        _post_moe_kernel,
        out_shape=jax.ShapeDtypeStruct((N_TOK, D_MODEL), F32),
        grid=(TILES,),
        in_specs=[_row_spec(D_MODEL), _tok_tile_spec(), _mod_spec(layer, 5),
                  _full_spec((1, D_MODEL)), _full_spec((1, D_MODEL))],
        out_specs=_row_spec(D_MODEL),
        compiler_params=_params(1),
        name="post_moe",
    )(x1, f, mods5, ln_g, ln_b)


def _axial_tables(dim):
    n_rows = DEC_SEQ // GRID_W
    rows = jnp.repeat(jnp.arange(n_rows, dtype=F32), GRID_W)
    cols = jnp.tile(jnp.arange(GRID_W, dtype=F32), n_rows)
    n_freq = dim // 4
    inv_freq = ROPE_THETA ** (-jnp.arange(n_freq, dtype=F32) / n_freq)
    ang = jnp.concatenate([rows[:, None] * inv_freq, cols[:, None] * inv_freq], -1)
    cos, sin = jnp.cos(ang), jnp.sin(ang)
    return jnp.concatenate([cos, cos], -1), jnp.concatenate([-sin, sin], -1)


def _with_identity_rows(cos_l, sin_l):
    return (jnp.concatenate([jnp.ones((TM, LANES), F32), cos_l], 0),
            jnp.concatenate([jnp.zeros((TM, LANES), F32), sin_l], 0))


def _gqa_rope_tables():
    cos, sin = _axial_tables(HEAD_DIM)
    return _with_identity_rows(jnp.tile(cos, (1, 2)), jnp.tile(sin, (1, 2)))


def _mla_rope_tables():
    cos, sin = _axial_tables(MLA_ROPE_DIM)
    pad_hi = LANES - MLA_ROPE_LO - MLA_ROPE_DIM
    cos_l = jnp.concatenate([jnp.ones((DEC_SEQ, MLA_ROPE_LO), F32), cos, jnp.ones((DEC_SEQ, pad_hi), F32)], 1)
    sin_l = jnp.pad(sin, ((0, 0), (MLA_ROPE_LO, pad_hi)))
    return _with_identity_rows(cos_l, sin_l)


def _moe(h2, aff_t, w_gate, w_up, w_down, layer):
    gate_p, idx_p = lax.top_k(aff_t[:, :N_PROMPT], CAP_P)
    gate_s, idx_s = lax.top_k(aff_t[:, N_PROMPT:], CAP_S)
    idx = jnp.concatenate([idx_p, idx_s + N_PROMPT], axis=1).reshape(-1)
    gate = jnp.concatenate([gate_p, gate_s], axis=1)[..., None]
    return _expert_ffn(idx, h2, gate, w_gate, w_up, w_down, layer)


def kernel(x_prompt, x_sample, cache_gqa_k, cache_gqa_v, cache_mla_ckv, cache_mla_kpe, c, c_ctx,
           ada_w, ada_b, ln_g, ln_b, a_w_qkv, a_q_norm, a_k_norm, a_w_o, b_w_qkv, b_sink, b_w_o,
           mla_w_down, mla_q_norm, mla_w_uq, mla_kv_norm, mla_w_ukv, mla_w_o,
           moe_w_router, moe_w_gate, moe_w_up, moe_w_down):
    x = jnp.concatenate([x_prompt.reshape(N_PROMPT, D_MODEL), x_sample.reshape(N_SAMPLE, D_MODEL)], 0)
    cond = jnp.concatenate([c_ctx[None], c, jnp.zeros((N_COND - 1 - DEC_BATCH, D_MODEL), F32)], 0)
    mods5 = _adaln(cond, ada_w, ada_b).reshape(DEPTH, 6, N_COND, 1, D_MODEL)
    gqa_cos, gqa_sin = _gqa_rope_tables()
    mla_cos, mla_sin = _mla_rope_tables()

    new_k, new_v, new_ckv, new_kpe = [], [], [], []
    gqa_slot = 0
    for i in range(DEPTH):
        kind, j = i % 3, i // 3
        if kind in (0, 1):
            if kind == 0:
                w_qkv, w_o, sink = a_w_qkv[j], a_w_o[j], None
                gain = jnp.concatenate([jnp.tile(a_q_norm[j], N_HEADS), jnp.tile(a_k_norm[j], N_KV_HEADS)])[None]
            else:
                w_qkv, w_o, sink = b_w_qkv[j], b_w_o[j], b_sink[j]
                gain = jnp.ones((1, QK_WIDTH), F32)
            q, kh, vh, k32, v32 = _gqa_proj(x, mods5, i, w_qkv.astype(BF16), gain, gqa_cos, gqa_sin,
                                            use_norm=(kind == 0))
            o_p = _gqa_attention(q, kh, vh, None, None, gqa_slot, sink, latent=False, window=False)
            o_s = _gqa_attention(q, kh, vh, cache_gqa_k, cache_gqa_v, gqa_slot, sink,
                                 latent=True, window=(kind == 1))
            for store, arr in ((new_k, k32), (new_v, v32)):
                store.append(arr[:, :N_PROMPT].reshape(N_KV_HEADS, BATCH, SEQ, HEAD_DIM).transpose(1, 0, 2, 3))
            gqa_slot += 1
        else:
            w_down = mla_w_down[j]
            rope_lo = Q_RANK + KV_RANK
            wd_ext = jnp.concatenate([
                w_down[:, :rope_lo], jnp.zeros((D_MODEL, MLA_ROPE_LO), F32), w_down[:, rope_lo:],
                jnp.zeros((D_MODEL, LANES - MLA_ROPE_LO - MLA_ROPE_DIM), F32)], 1).astype(BF16)
            wuq_pad = jnp.pad(mla_w_uq[j].reshape(Q_RANK, MLA_HEADS, MLA_QK_DIM),
                              ((0, 0), (0, 0), (0, LANES - MLA_QK_DIM))).reshape(Q_RANK, MLA_HEADS * LANES)
            w_ukv = mla_w_ukv[j].reshape(KV_RANK, MLA_HEADS, MLA_NOPE_DIM + MLA_V_DIM)
            wk_pad = jnp.pad(w_ukv[:, :, :MLA_NOPE_DIM], ((0, 0), (0, 0), (0, LANES - MLA_NOPE_DIM)))
            wk_pad = wk_pad.reshape(KV_RANK, MLA_HEADS * LANES).astype(BF16)
            wv = w_ukv[:, :, MLA_NOPE_DIM:].reshape(KV_RANK, MLA_HEADS * MLA_V_DIM).astype(BF16)
            q, ckv, kpe = _mla_proj(x, mods5, i, wd_ext, mla_q_norm[j][None], wuq_pad.astype(BF16),
                                    mla_kv_norm[j][None], mla_cos, mla_sin)
            c_all = jnp.concatenate([ckv, cache_mla_ckv[:, j].reshape(DEC_BATCH * PAST_LEN, KV_RANK)], 0)
            kpe_ctx = jnp.pad(cache_mla_kpe[:, j].reshape(DEC_BATCH * PAST_LEN, MLA_ROPE_DIM),
                              ((0, 0), (MLA_ROPE_LO, LANES - MLA_ROPE_LO - MLA_ROPE_DIM)))
            kpe_all = jnp.concatenate([kpe, kpe_ctx], 0)
            k_all, v_all = _mla_kv(c_all, kpe_all, wk_pad, wv)
            o_p = _mla_attention(q, k_all, v_all, latent=False)
            o_s = _mla_attention(q, k_all, v_all, latent=True)
            w_o = mla_w_o[j]
            new_ckv.append(ckv[:N_PROMPT].reshape(BATCH, SEQ, KV_RANK))
            new_kpe.append(kpe[:N_PROMPT, MLA_ROPE_LO:MLA_ROPE_LO + MLA_ROPE_DIM].reshape(BATCH, SEQ, MLA_ROPE_DIM))
        x1, h2, aff_t = _post_attn(o_p, o_s, x, w_o.astype(BF16), mods5, i, ln_g[i, 0][None], ln_b[i, 0][None],
                                   moe_w_router[i].T)
        f = _moe(h2, aff_t, moe_w_gate, moe_w_up, moe_w_down, i)
        x = _post_moe(x1, f, mods5, i, ln_g[i, 1][None], ln_b[i, 1][None])

    y_prompt = x[:N_PROMPT].reshape(BATCH, SEQ, D_MODEL)
    y_sample = x[N_PROMPT:].reshape(DEC_BATCH, DEC_SEQ, D_MODEL)
    return (y_prompt, y_sample, jnp.stack(new_k, 1), jnp.stack(new_v, 1),
            jnp.stack(new_ckv, 1), jnp.stack(new_kpe, 1))
```

```python
import functools
import math

import jax
import jax.numpy as jnp
from jax import lax
from jax.experimental import pallas as pl
from jax.experimental.pallas import tpu as pltpu

F32 = jnp.float32
BF16 = jnp.bfloat16

D_MODEL = 1024
BATCH = 16
SEQ = 256
DEPTH = 4
DEC_BATCH = 4
DEC_SEQ = 2048
PAST_LEN = 512
GRID_W = 64
N_HEADS = 16
N_KV_HEADS = 4
GROUP = N_HEADS // N_KV_HEADS
HEAD_DIM = 64
WINDOW = 128
MLA_HEADS = 16
Q_RANK = 256
KV_RANK = 128
MLA_NOPE_DIM = 64
MLA_ROPE_DIM = 32
MLA_V_DIM = 64
MLA_QK_DIM = MLA_NOPE_DIM + MLA_ROPE_DIM
MLA_SCALE = MLA_QK_DIM ** -0.5
GQA_SCALE = HEAD_DIM ** -0.5
N_EXPERTS = 16
EXPERT_FF = 2048
CAPACITY_FACTOR = 2
ROPE_THETA = 10000.0
NORM_EPS = 1e-6
LN_EPS = 1e-5
ALPHA = (2 * DEPTH) ** 0.25
LOG2E = math.log2(math.e)

LANES = 128
N_PROMPT = BATCH * SEQ
N_SAMPLE = DEC_BATCH * DEC_SEQ
N_TOK = N_PROMPT + N_SAMPLE
TM = 512
ATT_TQ = SEQ
PROMPT_TILES = N_PROMPT // TM
TILES = N_TOK // TM
SAMPLE_TILES_PER_REQ = DEC_SEQ // TM
N_COND = 8
CAP_P = CAPACITY_FACTOR * N_PROMPT // N_EXPERTS
CAP_S = CAPACITY_FACTOR * N_SAMPLE // N_EXPERTS
CAP = CAP_P + CAP_S
FF_TILE = 512
KV_CHUNK = 512
GQA_TQ = 512
MLA_TQ = 1024

_NT = (((1,), (1,)), ((), ()))
_VMEM_LIMIT = 56 * 1024 * 1024


def _params(n_axes, vmem=_VMEM_LIMIT):
    return pltpu.CompilerParams(dimension_semantics=("arbitrary",) * n_axes, vmem_limit_bytes=vmem)


def _cond_of_tile(i):
    return jnp.where(i < PROMPT_TILES, 0, 1 + (i - PROMPT_TILES) // SAMPLE_TILES_PER_REQ)


def _pos_block_of_tile(i):
    return jnp.where(i < PROMPT_TILES, 0, 1 + (i - PROMPT_TILES) % SAMPLE_TILES_PER_REQ)


def _mod_spec(layer, k):
    return pl.BlockSpec((None, None, None, 1, D_MODEL), lambda i: (layer, k, _cond_of_tile(i), 0, 0))


def _row_spec(width):
    return pl.BlockSpec((TM, width), lambda i: (i, 0))


def _full_spec(shape):
    nd = len(shape)
    return pl.BlockSpec(shape, lambda *_: (0,) * nd)


TOK_TILE = 8


def _tok_tile_spec():
    return pl.BlockSpec((TM * TOK_TILE, LANES), lambda i: (i, 0))


def _tok_rows(t):
    return pl.ds(pl.multiple_of(t * TOK_TILE, TOK_TILE), TOK_TILE)


def _lane_block_rows(j, n_tokens, t0=0):
    return pl.ds(t0 * TOK_TILE + j, n_tokens, stride=TOK_TILE)


def _store_token_tiles(ref, val):
    for j in range(TOK_TILE):
        ref[_lane_block_rows(j, val.shape[0]), :] = val[:, j * LANES:(j + 1) * LANES]


def _load_token_tiles(ref, n_tokens, t0=0):
    return jnp.concatenate([ref[_lane_block_rows(j, n_tokens, t0), :] for j in range(TOK_TILE)], axis=-1)


def _split_bf16(a):
    hi = a.astype(BF16)
    lo = (a - hi.astype(F32)).astype(BF16)
    return hi, lo


def _dot3(a, b, dims=(((1,), (0,)), ((), ()))):
    a_hi, a_lo = _split_bf16(a)
    b_hi, b_lo = _split_bf16(b)
    dg = functools.partial(lax.dot_general, dimension_numbers=dims, preferred_element_type=F32)
    return dg(a_hi, b_hi) + dg(a_lo, b_hi) + dg(a_hi, b_lo)


def _layer_norm(y, g, b):
    mu = jnp.mean(y, axis=-1, keepdims=True)
    d = y - mu
    var = jnp.mean(d * d, axis=-1, keepdims=True)
    return d * lax.rsqrt(var + LN_EPS) * g + b


def _rope_lanes(t, cos, sin, half):
    lane = lax.broadcasted_iota(jnp.int32, t.shape, 1)
    first = (lane & (2 * half - 1)) < half
    partner = jnp.where(first, pltpu.roll(t, LANES - half, 1), pltpu.roll(t, half, 1))
    return t * cos + partner * sin


def _pair_rms(t, gain):
    sq = t * t
    lane = lax.broadcasted_iota(jnp.int32, t.shape, 1)
    lo = lane < HEAD_DIM
    s_lo = jnp.sum(jnp.where(lo, sq, 0.0), axis=-1, keepdims=True)
    s_hi = jnp.sum(jnp.where(lo, 0.0, sq), axis=-1, keepdims=True)
    ms = jnp.where(lo, s_lo, s_hi) * (1.0 / HEAD_DIM)
    return t * lax.rsqrt(ms + NORM_EPS) * gain


def _adaln_kernel(c_ref, w_ref, b_ref, o_ref):
    c = c_ref[...]
    a = c / (1.0 + jnp.exp(-c))
    o_ref[...] = _dot3(a, w_ref[...]) + b_ref[...]


def _adaln(cond, ada_w, ada_b):
    return pl.pallas_call(
        _adaln_kernel,
        out_shape=jax.ShapeDtypeStruct((DEPTH, 6, N_COND, D_MODEL), F32),
        grid=(DEPTH, 6),
        in_specs=[
            pl.BlockSpec((N_COND, D_MODEL), lambda l, k: (0, 0)),
            pl.BlockSpec((None, D_MODEL, D_MODEL), lambda l, k: (l, 0, k)),
            pl.BlockSpec((None, None, 1, D_MODEL), lambda l, k: (l, k, 0, 0)),
        ],
        out_specs=pl.BlockSpec((None, None, N_COND, D_MODEL), lambda l, k: (l, k, 0, 0)),
        compiler_params=_params(2),
        name="adaln",
    )(cond, ada_w, ada_b.reshape(DEPTH, 6, 1, D_MODEL))


QK_WIDTH = (N_HEADS + N_KV_HEADS) * HEAD_DIM
QKV_WIDTH = (N_HEADS + 2 * N_KV_HEADS) * HEAD_DIM
Q_WIDTH = N_HEADS * HEAD_DIM


def _gqa_proj_kernel(x_ref, sh_ref, sc_ref, w_ref, gain_ref, cos_ref, sin_ref,
                     q_ref, kh_ref, vh_ref, k32_ref, v32_ref, *, use_norm):
    h = (x_ref[...] * (1.0 + sc_ref[...]) + sh_ref[...]).astype(BF16)
    acc = jnp.dot(h, w_ref[...], preferred_element_type=F32)
    cos = cos_ref[...]
    sin = sin_ref[...]
    for j in range(QK_WIDTH // LANES):
        t = acc[:, j * LANES:(j + 1) * LANES]
        if use_norm:
            t = _pair_rms(t, gain_ref[:, j * LANES:(j + 1) * LANES])
        t = _rope_lanes(t, cos, sin, HEAD_DIM // 2)
        if j < Q_WIDTH // LANES:
            q_ref[:, j * LANES:(j + 1) * LANES] = (t * GQA_SCALE).astype(BF16)
        else:
            for half in range(2):
                kv_head = 2 * (j - Q_WIDTH // LANES) + half
                th = t[:, half * HEAD_DIM:(half + 1) * HEAD_DIM]
                kh_ref[kv_head] = th.astype(BF16)
                k32_ref[kv_head] = th
    for kv_head in range(N_KV_HEADS):
        lo = QK_WIDTH + kv_head * HEAD_DIM
        tv = acc[:, lo:lo + HEAD_DIM]
        vh_ref[kv_head] = tv.astype(BF16)
        v32_ref[kv_head] = tv


def _gqa_proj(x, mods5, layer, w_qkv, gain, cos_t, sin_t, use_norm):
    head_spec = pl.BlockSpec((N_KV_HEADS, TM, HEAD_DIM), lambda i: (0, i, 0))
    pos_spec = pl.BlockSpec((TM, LANES), lambda i: (_pos_block_of_tile(i), 0))
    return pl.pallas_call(
        functools.partial(_gqa_proj_kernel, use_norm=use_norm),
        out_shape=(
            jax.ShapeDtypeStruct((N_TOK, Q_WIDTH), BF16),
            jax.ShapeDtypeStruct((N_KV_HEADS, N_TOK, HEAD_DIM), BF16),
            jax.ShapeDtypeStruct((N_KV_HEADS, N_TOK, HEAD_DIM), BF16),
            jax.ShapeDtypeStruct((N_KV_HEADS, N_TOK, HEAD_DIM), F32),
            jax.ShapeDtypeStruct((N_KV_HEADS, N_TOK, HEAD_DIM), F32),
        ),
        grid=(TILES,),
        in_specs=[
            _row_spec(D_MODEL), _mod_spec(layer, 0), _mod_spec(layer, 1),
            _full_spec((D_MODEL, QKV_WIDTH)), _full_spec((1, QK_WIDTH)),
            pos_spec, pos_spec,
        ],
        out_specs=(_row_spec(Q_WIDTH), head_spec, head_spec, head_spec, head_spec),
        compiler_params=_params(1),
        name="gqa_proj",
    )(x, mods5, mods5, w_qkv, gain, cos_t, sin_t)


def _attend(qs, chunks, sink_col, c1):
    m = den = acc = None
    for k_fn, v_fn, bias, _ in chunks:
        s = lax.dot_general(qs, k_fn(), _NT, preferred_element_type=F32)
        if bias is not None:
            s = s + bias
        mc = jnp.max(s, axis=-1, keepdims=True)
        m_new = mc if m is None else jnp.maximum(m, mc)
        p = jnp.exp2((s - m_new) * c1)
        ps = jnp.sum(p, axis=-1, keepdims=True)
        pv = jnp.dot(p.astype(BF16), v_fn(), preferred_element_type=F32)
        if m is None:
            den, acc = ps, pv
        else:
            alpha = jnp.exp2((m - m_new) * c1)
            den = den * alpha + ps
            acc = acc * alpha + pv
        m = m_new
    if sink_col is not None:
        den = den + jnp.exp2((sink_col - m) * c1)
    return acc / den


def _chunk_list(k_ref, v_ref, length, cast):
    out = []
    width = min(KV_CHUNK, length)
    for c0 in range(0, length, width):
        if cast:
            k_fn = lambda c0=c0: k_ref[c0:c0 + width, :].astype(BF16)
            v_fn = lambda c0=c0: v_ref[c0:c0 + width, :].astype(BF16)
        else:
            k_fn = lambda c0=c0: k_ref[c0:c0 + width, :]
            v_fn = lambda c0=c0: v_ref[c0:c0 + width, :]
        out.append((k_fn, v_fn, None, width))
    return out


def _gqa_attn_kernel(*refs, tq, n_keys, kv_groups, has_ctx, use_sink, window):
    refs = list(refs)
    q_ref, k_ref, v_ref = refs[:3]
    pos = 3
    if has_ctx:
        kc_ref, vc_ref = refs[pos:pos + 2]
        pos += 2
    if use_sink:
        sink_ref = refs[pos]
        pos += 1
    o_ref = refs[pos]
    qi = pl.program_id(2)
    width = GROUP * HEAD_DIM

    for gi in range(kv_groups):
        g = pl.program_id(1) * kv_groups + gi
        qf = q_ref[:, gi * width:(gi + 1) * width].astype(F32)
        qs = jnp.concatenate(
            [qf[:, j * HEAD_DIM:(j + 1) * HEAD_DIM] for j in range(GROUP)], axis=0).astype(BF16)
        kg_ref, vg_ref = k_ref.at[gi], v_ref.at[gi]

        if window:
            span = tq + 2 * WINDOW
            kstart = pl.multiple_of(jnp.clip(qi * tq - WINDOW, 0, n_keys - span), LANES)
            qpos = qi * tq + lax.broadcasted_iota(jnp.int32, (tq, span), 0)
            kpos = kstart + lax.broadcasted_iota(jnp.int32, (tq, span), 1)
            band = jnp.where(jnp.abs(kpos - qpos) <= WINDOW, 0.0, -jnp.inf).astype(F32)
            bias = jnp.concatenate([band] * GROUP, axis=0)
            chunks = [(lambda: kg_ref[pl.ds(kstart, span), :], lambda: vg_ref[pl.ds(kstart, span), :], bias, span)]
        else:
            chunks = _chunk_list(kg_ref, vg_ref, n_keys, cast=False)
        if has_ctx:
            chunks += _chunk_list(kc_ref.at[gi], vc_ref.at[gi], PAST_LEN, cast=True)

        sink_col = None
        if use_sink:
            row = lax.broadcasted_iota(jnp.int32, (GROUP * tq, 1), 0)
            sink_col = jnp.full((GROUP * tq, 1), sink_ref[g * GROUP + GROUP - 1], F32)
            for j in reversed(range(GROUP - 1)):
                sink_col = jnp.where(row < (j + 1) * tq, sink_ref[g * GROUP + j], sink_col)

        o = _attend(qs, chunks, sink_col, LOG2E)
        o_ref[:, gi * width:(gi + 1) * width] = jnp.concatenate(
            [o[j * tq:(j + 1) * tq, :] for j in range(GROUP)], axis=-1).astype(BF16)


def _gqa_attention(q, kh, vh, ctx_k, ctx_v, slot, sink, *, latent, window):
    tq = GQA_TQ if latent and not window else ATT_TQ
    if latent:
        n_b, n_keys, n_qt = DEC_BATCH, DEC_SEQ, DEC_SEQ // tq
        row0 = N_PROMPT // tq
        kv_blk0 = N_PROMPT // DEC_SEQ
        kv_groups = 2 if window else 1
    else:
        n_b, n_keys, n_qt = BATCH, SEQ, 1
        row0 = 0
        kv_blk0 = 0
        kv_groups = N_KV_HEADS
    width = kv_groups * GROUP * HEAD_DIM
    q_spec = pl.BlockSpec((tq, width), lambda b, g, t: (row0 + b * n_qt + t, g))
    kv_spec = pl.BlockSpec((kv_groups, n_keys, HEAD_DIM), lambda b, g, t: (g, kv_blk0 + b, 0))
    in_specs = [q_spec, kv_spec, kv_spec]
    args = [q, kh, vh]
    if latent:
        ctx_spec = pl.BlockSpec((None, None, kv_groups, PAST_LEN, HEAD_DIM), lambda b, g, t: (b, slot, g, 0, 0))
        in_specs += [ctx_spec, ctx_spec]
        args += [ctx_k, ctx_v]
    use_sink = sink is not None
    if use_sink:
        in_specs.append(pl.BlockSpec(memory_space=pltpu.SMEM))
        args.append(sink)
    return pl.pallas_call(
        functools.partial(_gqa_attn_kernel, tq=tq, n_keys=n_keys, kv_groups=kv_groups, has_ctx=latent,
                          use_sink=use_sink, window=window),
        out_shape=jax.ShapeDtypeStruct((n_b * n_keys, Q_WIDTH), BF16),
        grid=(n_b, N_KV_HEADS // kv_groups, n_qt),
        in_specs=in_specs,
        out_specs=pl.BlockSpec((tq, width), lambda b, g, t: (b * n_qt + t, g)),
        compiler_params=_params(3),
        name="gqa_attn_latent" if latent else "gqa_attn_context",
    )(*args)


MLA_DOWN_EXT = Q_RANK + KV_RANK + LANES
MLA_HEAD_LANES = LANES
MLA_ROPE_LO = MLA_NOPE_DIM


def _mla_proj_kernel(x_ref, sh_ref, sc_ref, wd_ref, qg_ref, wuq_ref, kvg_ref, cos_ref, sin_ref,
                     q_ref, ckv_ref, kpe_ref):
    h = (x_ref[...] * (1.0 + sc_ref[...]) + sh_ref[...]).astype(BF16)
    acc = jnp.dot(h, wd_ref[...], preferred_element_type=F32)
    cos = cos_ref[...]
    sin = sin_ref[...]
    qa = acc[:, :Q_RANK]
    qn = qa * lax.rsqrt(jnp.mean(qa * qa, axis=-1, keepdims=True) + NORM_EPS) * qg_ref[...]
    q = jnp.dot(qn.astype(BF16), wuq_ref[...], preferred_element_type=F32)
    for hd in range(MLA_HEADS):
        t = q[:, hd * LANES:(hd + 1) * LANES]
        q_ref[:, hd * LANES:(hd + 1) * LANES] = _rope_lanes(t, cos, sin, MLA_ROPE_DIM // 2).astype(BF16)
    ckv = acc[:, Q_RANK:Q_RANK + KV_RANK]
    ckv_ref[...] = ckv * lax.rsqrt(jnp.mean(ckv * ckv, axis=-1, keepdims=True) + NORM_EPS) * kvg_ref[...]
    kpe_ref[...] = _rope_lanes(acc[:, Q_RANK + KV_RANK:], cos, sin, MLA_ROPE_DIM // 2)


def _mla_proj(x, mods5, layer, wd_ext, q_gain, wuq_pad, kv_gain, cos_t, sin_t):
    pos_spec = pl.BlockSpec((TM, LANES), lambda i: (_pos_block_of_tile(i), 0))
    return pl.pallas_call(
        _mla_proj_kernel,
        out_shape=(
            jax.ShapeDtypeStruct((N_TOK, MLA_HEADS * LANES), BF16),
            jax.ShapeDtypeStruct((N_TOK, KV_RANK), F32),
            jax.ShapeDtypeStruct((N_TOK, LANES), F32),
        ),
        grid=(TILES,),
        in_specs=[
            _row_spec(D_MODEL), _mod_spec(layer, 0), _mod_spec(layer, 1),
            _full_spec((D_MODEL, MLA_DOWN_EXT)), _full_spec((1, Q_RANK)),
            _full_spec((Q_RANK, MLA_HEADS * LANES)), _full_spec((1, KV_RANK)),
            pos_spec, pos_spec,
        ],
        out_specs=(_row_spec(MLA_HEADS * LANES), _row_spec(KV_RANK), _row_spec(LANES)),
        compiler_params=_params(1),
        name="mla_proj",
    )(x, mods5, mods5, wd_ext, q_gain, wuq_pad, kv_gain, cos_t, sin_t)


def _mla_kv_kernel(c_ref, kpe_ref, wk_ref, wv_ref, k_ref, v_ref):
    c = c_ref[...].astype(BF16)
    kk = jnp.dot(c, wk_ref[...], preferred_element_type=F32)
    vv = jnp.dot(c, wv_ref[...], preferred_element_type=F32)
    kpe = kpe_ref[...]
    for hd in range(MLA_HEADS):
        k_ref[hd] = (kk[:, hd * LANES:(hd + 1) * LANES] + kpe).astype(BF16)
        v_ref[hd] = vv[:, hd * MLA_V_DIM:(hd + 1) * MLA_V_DIM].astype(BF16)


def _mla_kv(c_all, kpe_all, wk_pad, wv):
    n_rows = c_all.shape[0]
    return pl.pallas_call(
        _mla_kv_kernel,
        out_shape=(
            jax.ShapeDtypeStruct((MLA_HEADS, n_rows, LANES), BF16),
            jax.ShapeDtypeStruct((MLA_HEADS, n_rows, MLA_V_DIM), BF16),
        ),
        grid=(n_rows // TM,),
        in_specs=[
            _row_spec(KV_RANK), _row_spec(LANES),
            _full_spec((KV_RANK, MLA_HEADS * LANES)), _full_spec((KV_RANK, MLA_HEADS * MLA_V_DIM)),
        ],
        out_specs=(
            pl.BlockSpec((MLA_HEADS, TM, LANES), lambda i: (0, i, 0)),
            pl.BlockSpec((MLA_HEADS, TM, MLA_V_DIM), lambda i: (0, i, 0)),
        ),
        compiler_params=_params(1),
        name="mla_kv",
    )(c_all, kpe_all, wk_pad, wv)


def _mla_attn_kernel(*refs, tq, n_keys, heads, has_ctx):
    refs = list(refs)
    q_ref, k_ref, v_ref = refs[:3]
    pos = 3
    if has_ctx:
        kc_ref, vc_ref = refs[pos:pos + 2]
        pos += 2
    o_ref = refs[pos]
    outs = []
    for j in range(heads):
        qs = q_ref[:, j * LANES:(j + 1) * LANES]
        chunks = _chunk_list(k_ref.at[j], v_ref.at[j], n_keys, cast=False)
        if has_ctx:
            chunks += _chunk_list(kc_ref.at[j], vc_ref.at[j], PAST_LEN, cast=False)
        outs.append(_attend(qs, chunks, None, MLA_SCALE * LOG2E))
    o_ref[...] = jnp.concatenate(outs, axis=-1).astype(BF16)


def _mla_attention(q, k_all, v_all, *, latent):
    if latent:
        tq = MLA_TQ
        n_b, n_keys, n_qt = DEC_BATCH, DEC_SEQ, DEC_SEQ // tq
        row0 = N_PROMPT // tq
        kv_blk0 = N_PROMPT // DEC_SEQ
        heads = 2
    else:
        tq = ATT_TQ
        n_b, n_keys, n_qt = BATCH, SEQ, 1
        row0 = 0
        kv_blk0 = 0
        heads = 8
    q_spec = pl.BlockSpec((tq, heads * LANES), lambda b, hp, t: (row0 + b * n_qt + t, hp))
    k_spec = pl.BlockSpec((heads, n_keys, LANES), lambda b, hp, t: (hp, kv_blk0 + b, 0))
    v_spec = pl.BlockSpec((heads, n_keys, MLA_V_DIM), lambda b, hp, t: (hp, kv_blk0 + b, 0))
    in_specs = [q_spec, k_spec, v_spec]
    args = [q, k_all, v_all]
    if latent:
        ctx0 = N_TOK // PAST_LEN
        in_specs += [
            pl.BlockSpec((heads, PAST_LEN, LANES), lambda b, hp, t: (hp, ctx0 + b, 0)),
            pl.BlockSpec((heads, PAST_LEN, MLA_V_DIM), lambda b, hp, t: (hp, ctx0 + b, 0)),
        ]
        args += [k_all, v_all]
    return pl.pallas_call(
        functools.partial(_mla_attn_kernel, tq=tq, n_keys=n_keys, heads=heads, has_ctx=latent),
        out_shape=jax.ShapeDtypeStruct((n_b * n_keys, MLA_HEADS * MLA_V_DIM), BF16),
        grid=(n_b, MLA_HEADS // heads, n_qt),
        in_specs=in_specs,
        out_specs=pl.BlockSpec((tq, heads * MLA_V_DIM), lambda b, hp, t: (b * n_qt + t, hp)),
        compiler_params=_params(3),
        name="mla_attn_latent" if latent else "mla_attn_context",
    )(*args)


def _post_attn_kernel(op_ref, os_ref, x_ref, wo_ref, g1_ref, sh2_ref, sc2_ref, lng_ref, lnb_ref, wr_ref,
                      x1_ref, h2_ref, aff_ref):
    o = jnp.where(pl.program_id(0) < PROMPT_TILES, op_ref[...], os_ref[...])
    proj = jnp.dot(o, wo_ref[...], preferred_element_type=F32)
    x1 = _layer_norm(ALPHA * x_ref[...] + g1_ref[...] * proj, lng_ref[...], lnb_ref[...])
    x1_ref[...] = x1
    h2 = x1 * (1.0 + sc2_ref[...]) + sh2_ref[...]
    _store_token_tiles(h2_ref, h2)
    logits_t = _dot3(wr_ref[...], h2, _NT)
    e = jnp.exp(logits_t - jnp.max(logits_t, axis=0, keepdims=True))
    aff_ref[...] = e / jnp.sum(e, axis=0, keepdims=True)


def _post_attn(o_p, o_s, x, w_o, mods5, layer, ln_g, ln_b, w_router_t):
    op_spec = pl.BlockSpec((TM, D_MODEL), lambda i: (jnp.minimum(i, PROMPT_TILES - 1), 0))
    os_spec = pl.BlockSpec((TM, D_MODEL), lambda i: (jnp.maximum(i - PROMPT_TILES, 0), 0))
    return pl.pallas_call(
        _post_attn_kernel,
        out_shape=(
            jax.ShapeDtypeStruct((N_TOK, D_MODEL), F32),
            jax.ShapeDtypeStruct((N_TOK * TOK_TILE, LANES), F32),
            jax.ShapeDtypeStruct((N_EXPERTS, N_TOK), F32),
        ),
        grid=(TILES,),
        in_specs=[
            op_spec, os_spec, _row_spec(D_MODEL), _full_spec((D_MODEL, D_MODEL)),
            _mod_spec(layer, 2), _mod_spec(layer, 3), _mod_spec(layer, 4),
            _full_spec((1, D_MODEL)), _full_spec((1, D_MODEL)), _full_spec((N_EXPERTS, D_MODEL)),
        ],
        out_specs=(_row_spec(D_MODEL), _tok_tile_spec(), pl.BlockSpec((N_EXPERTS, TM), lambda i: (0, i))),
        compiler_params=_params(1),
        name="post_attn",
    )(o_p, o_s, x, w_o, mods5, mods5, mods5, ln_g, ln_b, w_router_t)


FFN_ROWS = 512
FFN_STEPS = EXPERT_FF // FF_TILE
ROW_CHUNKS = CAP // FFN_ROWS
GATHER_PER_CHUNK = CAP // (FFN_STEPS * ROW_CHUNKS)


N_DMA_THREADS = 2


def _token_copy(src, src_tok, dst, dst_tok, sem):
    return pltpu.make_async_copy(src.at[_tok_rows(src_tok), :], dst.at[_tok_rows(dst_tok), :], sem)


def _wait_buffer(buf_ref, sem):
    pltpu.make_async_copy(buf_ref, buf_ref, sem).wait()


def _ffn_kernel(idx_ref, h_hbm, f_in_hbm, wg_ref, wu_ref, wd_ref, gate_ref, f_hbm,
                xbuf, acc, fbuf, gsem, rsem, wsem):
    del f_in_hbm
    e = pl.program_id(0)
    f = pl.program_id(1)
    buf = e % 2
    nxt = jnp.minimum(e + 1, N_EXPERTS - 1)

    def gather(expert, s, b):
        return _token_copy(h_hbm, idx_ref[expert * CAP + s], xbuf.at[b], s, gsem.at[b])

    @pl.when((e == 0) & (f == 0))
    def _():
        def issue(s, carry):
            gather(0, s, 0).start()
            return carry
        lax.fori_loop(0, CAP, issue, 0, unroll=8)

    @pl.when(f == 0)
    def _():
        _wait_buffer(xbuf.at[buf], gsem.at[buf])
        acc[...] = jnp.zeros_like(acc)

    def rows_copy(r_lo, n_rows, to_hbm):
        def body(i, carry):
            for thread in range(N_DMA_THREADS):
                r = r_lo + i * N_DMA_THREADS + thread
                tok = idx_ref[e * CAP + r]
                if to_hbm:
                    _token_copy(fbuf, r, f_hbm, tok, wsem.at[0]).start(priority=thread)
                else:
                    _token_copy(f_hbm, tok, fbuf, r, rsem.at[0]).start(priority=thread)
            return carry
        lax.fori_loop(0, n_rows // N_DMA_THREADS, body, 0, unroll=4)

    @pl.when(f == FFN_STEPS - 2)
    def _():
        rows_copy(0, CAP, False)

    wg = wg_ref[...].astype(BF16)
    wu = wu_ref[...].astype(BF16)
    wd = wd_ref[...].astype(BF16)
    for c in range(ROW_CHUNKS):
        r0 = c * FFN_ROWS
        for j in range(GATHER_PER_CHUNK):
            gather(nxt, (f * ROW_CHUNKS + c) * GATHER_PER_CHUNK + j, 1 - buf).start(priority=j % N_DMA_THREADS)
        x = _load_token_tiles(xbuf.at[buf], FFN_ROWS, r0).astype(BF16)
        a = jnp.dot(x, wg, preferred_element_type=F32)
        u = jnp.dot(x, wu, preferred_element_type=F32)
        hid = (a / (1.0 + jnp.exp(-a)) * u).astype(BF16)
        acc[r0:r0 + FFN_ROWS, :] += jnp.dot(hid, wd, preferred_element_type=F32)

    @pl.when(f == FFN_STEPS - 1)
    def _():
        _wait_buffer(fbuf, rsem.at[0])
        for k in range(ROW_CHUNKS):
            r0 = k * FFN_ROWS
            g = gate_ref[r0:r0 + FFN_ROWS, :]
            for j in range(TOK_TILE):
                rows = _lane_block_rows(j, FFN_ROWS, r0)
                fbuf[rows, :] = fbuf[rows, :] + acc[r0:r0 + FFN_ROWS, j * LANES:(j + 1) * LANES] * g
            rows_copy(r0, FFN_ROWS, True)
        _wait_buffer(fbuf, wsem.at[0])

    @pl.when((e == N_EXPERTS - 1) & (f == FFN_STEPS - 1))
    def _():
        _wait_buffer(xbuf.at[1 - buf], gsem.at[1 - buf])


def _expert_ffn(idx, h2, gate, w_gate, w_up, w_down, layer):
    grid_spec = pltpu.PrefetchScalarGridSpec(
        num_scalar_prefetch=1,
        grid=(N_EXPERTS, FFN_STEPS),
        in_specs=[
            pl.BlockSpec(memory_space=pl.ANY),
            pl.BlockSpec(memory_space=pl.ANY),
            pl.BlockSpec((None, None, D_MODEL, FF_TILE), lambda e, f, idx: (layer, e, 0, f)),
            pl.BlockSpec((None, None, D_MODEL, FF_TILE), lambda e, f, idx: (layer, e, 0, f)),
            pl.BlockSpec((None, None, FF_TILE, D_MODEL), lambda e, f, idx: (layer, e, f, 0)),
            pl.BlockSpec((None, CAP, 1), lambda e, f, idx: (e, 0, 0)),
        ],
        out_specs=pl.BlockSpec(memory_space=pl.ANY),
        scratch_shapes=[
            pltpu.VMEM((2, CAP * TOK_TILE, LANES), F32),
            pltpu.VMEM((CAP, D_MODEL), F32),
            pltpu.VMEM((CAP * TOK_TILE, LANES), F32),
            pltpu.SemaphoreType.DMA((2,)), pltpu.SemaphoreType.DMA((1,)), pltpu.SemaphoreType.DMA((1,)),
        ],
    )
    f_zero = jnp.zeros((N_TOK * TOK_TILE, LANES), F32)
    return pl.pallas_call(
        _ffn_kernel,
        out_shape=jax.ShapeDtypeStruct((N_TOK * TOK_TILE, LANES), F32),
        grid_spec=grid_spec,
        input_output_aliases={2: 0},
        compiler_params=_params(2),
        name="expert_ffn",
    )(idx, h2, f_zero, w_gate, w_up, w_down, gate)


def _post_moe_kernel(x_ref, f_ref, g2_ref, lng_ref, lnb_ref, o_ref):
    f = _load_token_tiles(f_ref, TM)
    o_ref[...] = _layer_norm(ALPHA * x_ref[...] + g2_ref[...] * f, lng_ref[...], lnb_ref[...])


def _post_moe(x1, f, mods5, layer, ln_g, ln_b):
    return pl.pallas_call(
        _post_moe_kernel,
        out_shape=jax.ShapeDtypeStruct((N_TOK, D_MODEL), F32),
        grid=(TILES,),
        in_specs=[_row_spec(D_MODEL), _tok_tile_spec(), _mod_spec(layer, 5),
                  _full_spec((1, D_MODEL)), _full_spec((1, D_MODEL))],
        out_specs=_row_spec(D_MODEL),
        compiler_params=_params(1),
        name="post_moe",
    )(x1, f, mods5, ln_g, ln_b)


def _axial_tables(dim):
    n_rows = DEC_SEQ // GRID_W
    rows = jnp.repeat(jnp.arange(n_rows, dtype=F32), GRID_W)
    cols = jnp.tile(jnp.arange(GRID_W, dtype=F32), n_rows)
    n_freq = dim // 4
    inv_freq = ROPE_THETA ** (-jnp.arange(n_freq, dtype=F32) / n_freq)
    ang = jnp.concatenate([rows[:, None] * inv_freq, cols[:, None] * inv_freq], -1)
    cos, sin = jnp.cos(ang), jnp.sin(ang)
    return jnp.concatenate([cos, cos], -1), jnp.concatenate([-sin, sin], -1)


def _with_identity_rows(cos_l, sin_l):
    return (jnp.concatenate([jnp.ones((TM, LANES), F32), cos_l], 0),
            jnp.concatenate([jnp.zeros((TM, LANES), F32), sin_l], 0))


def _gqa_rope_tables():
    cos, sin = _axial_tables(HEAD_DIM)
    return _with_identity_rows(jnp.tile(cos, (1, 2)), jnp.tile(sin, (1, 2)))


def _mla_rope_tables():
    cos, sin = _axial_tables(MLA_ROPE_DIM)
    pad_hi = LANES - MLA_ROPE_LO - MLA_ROPE_DIM
    cos_l = jnp.concatenate([jnp.ones((DEC_SEQ, MLA_ROPE_LO), F32), cos, jnp.ones((DEC_SEQ, pad_hi), F32)], 1)
    sin_l = jnp.pad(sin, ((0, 0), (MLA_ROPE_LO, pad_hi)))
    return _with_identity_rows(cos_l, sin_l)


def _moe(h2, aff_t, w_gate, w_up, w_down, layer):
    gate_p, idx_p = lax.top_k(aff_t[:, :N_PROMPT], CAP_P)
    gate_s, idx_s = lax.top_k(aff_t[:, N_PROMPT:], CAP_S)
    idx = jnp.concatenate([idx_p, idx_s + N_PROMPT], axis=1).reshape(-1)
    gate = jnp.concatenate([gate_p, gate_s], axis=1)[..., None]
    return _expert_ffn(idx, h2, gate, w_gate, w_up, w_down, layer)


def kernel(x_prompt, x_sample, cache_gqa_k, cache_gqa_v, cache_mla_ckv, cache_mla_kpe, c, c_ctx,
           ada_w, ada_b, ln_g, ln_b, a_w_qkv, a_q_norm, a_k_norm, a_w_o, b_w_qkv, b_sink, b_w_o,
           mla_w_down, mla_q_norm, mla_w_uq, mla_kv_norm, mla_w_ukv, mla_w_o,
           moe_w_router, moe_w_gate, moe_w_up, moe_w_down):
    x = jnp.concatenate([x_prompt.reshape(N_PROMPT, D_MODEL), x_sample.reshape(N_SAMPLE, D_MODEL)], 0)
    cond = jnp.concatenate([c_ctx[None], c, jnp.zeros((N_COND - 1 - DEC_BATCH, D_MODEL), F32)], 0)
    mods5 = _adaln(cond, ada_w, ada_b).reshape(DEPTH, 6, N_COND, 1, D_MODEL)
    gqa_cos, gqa_sin = _gqa_rope_tables()
    mla_cos, mla_sin = _mla_rope_tables()

    new_k, new_v, new_ckv, new_kpe = [], [], [], []
    gqa_slot = 0
    for i in range(DEPTH):
        kind, j = i % 3, i // 3
        if kind in (0, 1):
            if kind == 0:
                w_qkv, w_o, sink = a_w_qkv[j], a_w_o[j], None
                gain = jnp.concatenate([jnp.tile(a_q_norm[j], N_HEADS), jnp.tile(a_k_norm[j], N_KV_HEADS)])[None]
            else:
                w_qkv, w_o, sink = b_w_qkv[j], b_w_o[j], b_sink[j]
                gain = jnp.ones((1, QK_WIDTH), F32)
            q, kh, vh, k32, v32 = _gqa_proj(x, mods5, i, w_qkv.astype(BF16), gain, gqa_cos, gqa_sin,
                                            use_norm=(kind == 0))
            o_p = _gqa_attention(q, kh, vh, None, None, gqa_slot, sink, latent=False, window=False)
            o_s = _gqa_attention(q, kh, vh, cache_gqa_k, cache_gqa_v, gqa_slot, sink,
                                 latent=True, window=(kind == 1))
            for store, arr in ((new_k, k32), (new_v, v32)):
                store.append(arr[:, :N_PROMPT].reshape(N_KV_HEADS, BATCH, SEQ, HEAD_DIM).transpose(1, 0, 2, 3))
            gqa_slot += 1
        else:
            w_down = mla_w_down[j]
            rope_lo = Q_RANK + KV_RANK
            wd_ext = jnp.concatenate([
                w_down[:, :rope_lo], jnp.zeros((D_MODEL, MLA_ROPE_LO), F32), w_down[:, rope_lo:],
                jnp.zeros((D_MODEL, LANES - MLA_ROPE_LO - MLA_ROPE_DIM), F32)], 1).astype(BF16)
            wuq_pad = jnp.pad(mla_w_uq[j].reshape(Q_RANK, MLA_HEADS, MLA_QK_DIM),
                              ((0, 0), (0, 0), (0, LANES - MLA_QK_DIM))).reshape(Q_RANK, MLA_HEADS * LANES)
            w_ukv = mla_w_ukv[j].reshape(KV_RANK, MLA_HEADS, MLA_NOPE_DIM + MLA_V_DIM)
            wk_pad = jnp.pad(w_ukv[:, :, :MLA_NOPE_DIM], ((0, 0), (0, 0), (0, LANES - MLA_NOPE_DIM)))
            wk_pad = wk_pad.reshape(KV_RANK, MLA_HEADS * LANES).astype(BF16)
            wv = w_ukv[:, :, MLA_NOPE_DIM:].reshape(KV_RANK, MLA_HEADS * MLA_V_DIM).astype(BF16)
            q, ckv, kpe = _mla_proj(x, mods5, i, wd_ext, mla_q_norm[j][None], wuq_pad.astype(BF16),
                                    mla_kv_norm[j][None], mla_cos, mla_sin)
            c_all = jnp.concatenate([ckv, cache_mla_ckv[:, j].reshape(DEC_BATCH * PAST_LEN, KV_RANK)], 0)
            kpe_ctx = jnp.pad(cache_mla_kpe[:, j].reshape(DEC_BATCH * PAST_LEN, MLA_ROPE_DIM),
                              ((0, 0), (MLA_ROPE_LO, LANES - MLA_ROPE_LO - MLA_ROPE_DIM)))
            kpe_all = jnp.concatenate([kpe, kpe_ctx], 0)
            k_all, v_all = _mla_kv(c_all, kpe_all, wk_pad, wv)
            o_p = _mla_attention(q, k_all, v_all, latent=False)
            o_s = _mla_attention(q, k_all, v_all, latent=True)
            w_o = mla_w_o[j]
            new_ckv.append(ckv[:N_PROMPT].reshape(BATCH, SEQ, KV_RANK))
            new_kpe.append(kpe[:N_PROMPT, MLA_ROPE_LO:MLA_ROPE_LO + MLA_ROPE_DIM].reshape(BATCH, SEQ, MLA_ROPE_DIM))
        x1, h2, aff_t = _post_attn(o_p, o_s, x, w_o.astype(BF16), mods5, i, ln_g[i, 0][None], ln_b[i, 0][None],
                                   moe_w_router[i].T)
        f = _moe(h2, aff_t, moe_w_gate, moe_w_up, moe_w_down, i)
        x = _post_moe(x1, f, mods5, i, ln_g[i, 1][None], ln_b[i, 1][None])

    y_prompt = x[:N_PROMPT].reshape(BATCH, SEQ, D_MODEL)
    y_sample = x[N_PROMPT:].reshape(DEC_BATCH, DEC_SEQ, D_MODEL)
    return (y_prompt, y_sample, jnp.stack(new_k, 1), jnp.stack(new_v, 1),
            jnp.stack(new_ckv, 1), jnp.stack(new_kpe, 1))
```

```python
import functools
import math

import jax
import jax.numpy as jnp
from jax import lax
from jax.experimental import pallas as pl
from jax.experimental.pallas import tpu as pltpu

F32 = jnp.float32
BF16 = jnp.bfloat16

D_MODEL = 1024
BATCH = 16
SEQ = 256
DEPTH = 4
DEC_BATCH = 4
DEC_SEQ = 2048
PAST_LEN = 512
GRID_W = 64
N_HEADS = 16
N_KV_HEADS = 4
GROUP = N_HEADS // N_KV_HEADS
HEAD_DIM = 64
WINDOW = 128
MLA_HEADS = 16
Q_RANK = 256
KV_RANK = 128
MLA_NOPE_DIM = 64
MLA_ROPE_DIM = 32
MLA_V_DIM = 64
MLA_QK_DIM = MLA_NOPE_DIM + MLA_ROPE_DIM
MLA_SCALE = MLA_QK_DIM ** -0.5
GQA_SCALE = HEAD_DIM ** -0.5
N_EXPERTS = 16
EXPERT_FF = 2048
CAPACITY_FACTOR = 2
ROPE_THETA = 10000.0
NORM_EPS = 1e-6
LN_EPS = 1e-5
ALPHA = (2 * DEPTH) ** 0.25
LOG2E = math.log2(math.e)

LANES = 128
N_PROMPT = BATCH * SEQ
N_SAMPLE = DEC_BATCH * DEC_SEQ
N_TOK = N_PROMPT + N_SAMPLE
TM = 512
ATT_TQ = SEQ
PROMPT_TILES = N_PROMPT // TM
TILES = N_TOK // TM
SAMPLE_TILES_PER_REQ = DEC_SEQ // TM
N_COND = 8
CAP_P = CAPACITY_FACTOR * N_PROMPT // N_EXPERTS
CAP_S = CAPACITY_FACTOR * N_SAMPLE // N_EXPERTS
CAP = CAP_P + CAP_S
FF_TILE = 512
KV_CHUNK = 1024
GQA_TQ = 512
MLA_TQ = 1024

_NT = (((1,), (1,)), ((), ()))
_VMEM_LIMIT = 56 * 1024 * 1024


def _params(n_axes, vmem=_VMEM_LIMIT):
    return pltpu.CompilerParams(dimension_semantics=("arbitrary",) * n_axes, vmem_limit_bytes=vmem)


def _cond_of_tile(i):
    return jnp.where(i < PROMPT_TILES, 0, 1 + (i - PROMPT_TILES) // SAMPLE_TILES_PER_REQ)


def _pos_block_of_tile(i):
    return jnp.where(i < PROMPT_TILES, 0, 1 + (i - PROMPT_TILES) % SAMPLE_TILES_PER_REQ)


def _mod_spec(layer, k):
    return pl.BlockSpec((None, None, None, 1, D_MODEL), lambda i: (layer, k, _cond_of_tile(i), 0, 0))


def _row_spec(width):
    return pl.BlockSpec((TM, width), lambda i: (i, 0))


def _full_spec(shape):
    nd = len(shape)
    return pl.BlockSpec(shape, lambda *_: (0,) * nd)


def _stream_specs(width):
    return [pl.BlockSpec((TM, width), lambda i: (jnp.minimum(i, PROMPT_TILES - 1), 0)),
            pl.BlockSpec((TM, width), lambda i: (jnp.maximum(i - PROMPT_TILES, 0), 0))]


def _read_rows(refs):
    if len(refs) == 1:
        return refs[0][...]
    return jnp.where(pl.program_id(0) < PROMPT_TILES, refs[0][...], refs[1][...])


def _row_inputs(x):
    if isinstance(x, tuple):
        return _stream_specs(x[0].shape[1]), list(x)
    return [_row_spec(x.shape[1])], [x]


TOK_TILE = 8


def _tok_tile_spec():
    return pl.BlockSpec((TM * TOK_TILE, LANES), lambda i: (i, 0))


def _token_tiles(ref):
    if len(ref.shape) == 3:
        return ref
    return ref.reshape(ref.shape[0] // TOK_TILE, TOK_TILE, LANES)


def _lane_block_rows(j, n_tokens, t0=0):
    return pl.ds(t0 * TOK_TILE + j, n_tokens, stride=TOK_TILE)


def _store_token_tiles(ref, val):
    for j in range(TOK_TILE):
        ref[_lane_block_rows(j, val.shape[0]), :] = val[:, j * LANES:(j + 1) * LANES]


def _load_token_tiles(ref, n_tokens, t0=0):
    return jnp.concatenate([ref[_lane_block_rows(j, n_tokens, t0), :] for j in range(TOK_TILE)], axis=-1)


def _split_bf16(a):
    hi = a.astype(BF16)
    lo = (a - hi.astype(F32)).astype(BF16)
    return hi, lo


def _dot3(a, b, dims=(((1,), (0,)), ((), ()))):
    a_hi, a_lo = _split_bf16(a)
    b_hi, b_lo = _split_bf16(b)
    dg = functools.partial(lax.dot_general, dimension_numbers=dims, preferred_element_type=F32)
    return dg(a_hi, b_hi) + dg(a_lo, b_hi) + dg(a_hi, b_lo)


def _layer_norm(y, g, b):
    mu = jnp.mean(y, axis=-1, keepdims=True)
    d = y - mu
    var = jnp.mean(d * d, axis=-1, keepdims=True)
    return d * lax.rsqrt(var + LN_EPS) * g + b


def _rope_lanes(t, cos, sin, half):
    lane = lax.broadcasted_iota(jnp.int32, t.shape, 1)
    first = (lane & (2 * half - 1)) < half
    partner = jnp.where(first, pltpu.roll(t, LANES - half, 1), pltpu.roll(t, half, 1))
    return t * cos + partner * sin


def _pair_rms(t, gain):
    sq = t * t
    lane = lax.broadcasted_iota(jnp.int32, t.shape, 1)
    lo = lane < HEAD_DIM
    s_lo = jnp.sum(jnp.where(lo, sq, 0.0), axis=-1, keepdims=True)
    s_hi = jnp.sum(jnp.where(lo, 0.0, sq), axis=-1, keepdims=True)
    ms = jnp.where(lo, s_lo, s_hi) * (1.0 / HEAD_DIM)
    return t * lax.rsqrt(ms + NORM_EPS) * gain


def _adaln_kernel(c_ref, w_ref, b_ref, o_ref):
    c = c_ref[...]
    a = c / (1.0 + jnp.exp(-c))
    o_ref[...] = _dot3(a, w_ref[...]) + b_ref[...]


def _adaln(cond, ada_w, ada_b):
    return pl.pallas_call(
        _adaln_kernel,
        out_shape=jax.ShapeDtypeStruct((DEPTH, 6, N_COND, D_MODEL), F32),
        grid=(DEPTH, 6),
        in_specs=[
            pl.BlockSpec((N_COND, D_MODEL), lambda l, k: (0, 0)),
            pl.BlockSpec((None, D_MODEL, D_MODEL), lambda l, k: (l, 0, k)),
            pl.BlockSpec((None, None, 1, D_MODEL), lambda l, k: (l, k, 0, 0)),
        ],
        out_specs=pl.BlockSpec((None, None, N_COND, D_MODEL), lambda l, k: (l, k, 0, 0)),
        compiler_params=_params(2),
        name="adaln",
    )(cond, ada_w, ada_b.reshape(DEPTH, 6, 1, D_MODEL))


QK_WIDTH = (N_HEADS + N_KV_HEADS) * HEAD_DIM
QKV_WIDTH = (N_HEADS + 2 * N_KV_HEADS) * HEAD_DIM
Q_WIDTH = N_HEADS * HEAD_DIM


def _gqa_proj_kernel(*refs, use_norm, n_x):
    x_refs, refs = refs[:n_x], refs[n_x:]
    sh_ref, sc_ref, w_ref, gain_ref, cos_ref, sin_ref, q_ref, kh_ref, vh_ref, k32_ref, v32_ref = refs
    h = (_read_rows(x_refs) * (1.0 + sc_ref[...]) + sh_ref[...]).astype(BF16)
    acc = jnp.dot(h, w_ref[...], preferred_element_type=F32)
    cos = cos_ref[...]
    sin = sin_ref[...]
    k_heads, v_heads = [], []
    for j in range(QK_WIDTH // LANES):
        t = acc[:, j * LANES:(j + 1) * LANES]
        if use_norm:
            t = _pair_rms(t, gain_ref[:, j * LANES:(j + 1) * LANES])
        t = _rope_lanes(t, cos, sin, HEAD_DIM // 2)
        if j < Q_WIDTH // LANES:
            q_ref[:, j * LANES:(j + 1) * LANES] = (t * GQA_SCALE).astype(BF16)
        else:
            for half in range(2):
                th = t[:, half * HEAD_DIM:(half + 1) * HEAD_DIM]
                kh_ref[len(k_heads)] = th.astype(BF16)
                k_heads.append(th)
    for kv_head in range(N_KV_HEADS):
        lo = QK_WIDTH + kv_head * HEAD_DIM
        tv = acc[:, lo:lo + HEAD_DIM]
        vh_ref[kv_head] = tv.astype(BF16)
        v_heads.append(tv)

    @pl.when(pl.program_id(0) < PROMPT_TILES)
    def _():
        for req in range(TM // SEQ):
            for kv_head in range(N_KV_HEADS):
                k32_ref[req, kv_head] = k_heads[kv_head][req * SEQ:(req + 1) * SEQ, :]
                v32_ref[req, kv_head] = v_heads[kv_head][req * SEQ:(req + 1) * SEQ, :]


def _gqa_proj(x, mods5, layer, w_qkv, gain, cos_t, sin_t, use_norm):
    head_spec = pl.BlockSpec((N_KV_HEADS, TM, HEAD_DIM), lambda i: (0, i, 0))
    pos_spec = pl.BlockSpec((TM, LANES), lambda i: (_pos_block_of_tile(i), 0))
    cache_spec = pl.BlockSpec((TM // SEQ, N_KV_HEADS, SEQ, HEAD_DIM),
                              lambda i: (jnp.minimum(i, PROMPT_TILES - 1), 0, 0, 0))
    x_specs, x_args = _row_inputs(x)
    return pl.pallas_call(
        functools.partial(_gqa_proj_kernel, use_norm=use_norm, n_x=len(x_args)),
        out_shape=(
            jax.ShapeDtypeStruct((N_TOK, Q_WIDTH), BF16),
            jax.ShapeDtypeStruct((N_KV_HEADS, N_TOK, HEAD_DIM), BF16),
            jax.ShapeDtypeStruct((N_KV_HEADS, N_TOK, HEAD_DIM), BF16),
            jax.ShapeDtypeStruct((BATCH, N_KV_HEADS, SEQ, HEAD_DIM), F32),
            jax.ShapeDtypeStruct((BATCH, N_KV_HEADS, SEQ, HEAD_DIM), F32),
        ),
        grid=(TILES,),
        in_specs=x_specs + [
            _mod_spec(layer, 0), _mod_spec(layer, 1),
            _full_spec((D_MODEL, QKV_WIDTH)), _full_spec((1, QK_WIDTH)),
            pos_spec, pos_spec,
        ],
        out_specs=(_row_spec(Q_WIDTH), head_spec, head_spec, cache_spec, cache_spec),
        compiler_params=_params(1),
        name="gqa_proj",
    )(*x_args, mods5, mods5, w_qkv, gain, cos_t, sin_t)


def _attend(qs, chunks, sink_col, c1):
    m = den = acc = None
    for k_fn, v_fn, bias, _ in chunks:
        s = lax.dot_general(qs, k_fn(), _NT, preferred_element_type=F32)
        if bias is not None:
            s = s + bias
        mc = jnp.max(s, axis=-1, keepdims=True)
        m_new = mc if m is None else jnp.maximum(m, mc)
        p = jnp.exp2((s - m_new) * c1)
        ps = jnp.sum(p, axis=-1, keepdims=True)
        pv = jnp.dot(p.astype(BF16), v_fn(), preferred_element_type=F32)
        if m is None:
            den, acc = ps, pv
        else:
            alpha = jnp.exp2((m - m_new) * c1)
            den = den * alpha + ps
            acc = acc * alpha + pv
        m = m_new
    if sink_col is not None:
        den = den + jnp.exp2((sink_col - m) * c1)
    return acc / den


def _chunk_list(k_ref, v_ref, length, cast):
    out = []
    width = min(KV_CHUNK, length)
    for c0 in range(0, length, width):
        if cast:
            k_fn = lambda c0=c0: k_ref[c0:c0 + width, :].astype(BF16)
            v_fn = lambda c0=c0: v_ref[c0:c0 + width, :].astype(BF16)
        else:
            k_fn = lambda c0=c0: k_ref[c0:c0 + width, :]
            v_fn = lambda c0=c0: v_ref[c0:c0 + width, :]
        out.append((k_fn, v_fn, None, width))
    return out


def _gqa_attn_kernel(*refs, tq, n_keys, kv_groups, has_ctx, use_sink, window):
    refs = list(refs)
    q_ref, k_ref, v_ref = refs[:3]
    pos = 3
    if has_ctx:
        kc_ref, vc_ref = refs[pos:pos + 2]
        pos += 2
    if use_sink:
        sink_ref = refs[pos]
        pos += 1
    o_ref = refs[pos]
    qi = pl.program_id(2)
    width = GROUP * HEAD_DIM

    for gi in range(kv_groups):
        g = pl.program_id(1) * kv_groups + gi
        qf = q_ref[:, gi * width:(gi + 1) * width].astype(F32)
        qs = jnp.concatenate(
            [qf[:, j * HEAD_DIM:(j + 1) * HEAD_DIM] for j in range(GROUP)], axis=0).astype(BF16)
        kg_ref, vg_ref = k_ref.at[gi], v_ref.at[gi]

        if window:
            span = tq + 2 * WINDOW
            kstart = pl.multiple_of(jnp.clip(qi * tq - WINDOW, 0, n_keys - span), LANES)
            qpos = qi * tq + lax.broadcasted_iota(jnp.int32, (tq, span), 0)
            kpos = kstart + lax.broadcasted_iota(jnp.int32, (tq, span), 1)
            band = jnp.where(jnp.abs(kpos - qpos) <= WINDOW, 0.0, -jnp.inf).astype(F32)
            bias = jnp.concatenate([band] * GROUP, axis=0)
            chunks = [(lambda: kg_ref[pl.ds(kstart, span), :], lambda: vg_ref[pl.ds(kstart, span), :], bias, span)]
        else:
            chunks = _chunk_list(kg_ref, vg_ref, n_keys, cast=False)
        if has_ctx:
            chunks += _chunk_list(kc_ref.at[gi], vc_ref.at[gi], PAST_LEN, cast=True)

        sink_col = None
        if use_sink:
            row = lax.broadcasted_iota(jnp.int32, (GROUP * tq, 1), 0)
            sink_col = jnp.full((GROUP * tq, 1), sink_ref[g * GROUP + GROUP - 1], F32)
            for j in reversed(range(GROUP - 1)):
                sink_col = jnp.where(row < (j + 1) * tq, sink_ref[g * GROUP + j], sink_col)

        o = _attend(qs, chunks, sink_col, LOG2E)
        o_ref[:, gi * width:(gi + 1) * width] = jnp.concatenate(
            [o[j * tq:(j + 1) * tq, :] for j in range(GROUP)], axis=-1).astype(BF16)


def _gqa_attention(q, kh, vh, ctx_k, ctx_v, slot, sink, *, latent, window):
    tq = GQA_TQ if latent and not window else ATT_TQ
    if latent:
        n_b, n_keys, n_qt = DEC_BATCH, DEC_SEQ, DEC_SEQ // tq
        row0 = N_PROMPT // tq
        kv_blk0 = N_PROMPT // DEC_SEQ
        kv_groups = 2 if window else 1
    else:
        n_b, n_keys, n_qt = BATCH, SEQ, 1
        row0 = 0
        kv_blk0 = 0
        kv_groups = N_KV_HEADS
    width = kv_groups * GROUP * HEAD_DIM
    q_spec = pl.BlockSpec((tq, width), lambda b, g, t: (row0 + b * n_qt + t, g))
    kv_spec = pl.BlockSpec((kv_groups, n_keys, HEAD_DIM), lambda b, g, t: (g, kv_blk0 + b, 0))
    in_specs = [q_spec, kv_spec, kv_spec]
    args = [q, kh, vh]
    if latent:
        ctx_spec = pl.BlockSpec((None, None, kv_groups, PAST_LEN, HEAD_DIM), lambda b, g, t: (b, slot, g, 0, 0))
        in_specs += [ctx_spec, ctx_spec]
        args += [ctx_k, ctx_v]
    use_sink = sink is not None
    if use_sink:
        in_specs.append(pl.BlockSpec(memory_space=pltpu.SMEM))
        args.append(sink)
    return pl.pallas_call(
        functools.partial(_gqa_attn_kernel, tq=tq, n_keys=n_keys, kv_groups=kv_groups, has_ctx=latent,
                          use_sink=use_sink, window=window),
        out_shape=jax.ShapeDtypeStruct((n_b * n_keys, Q_WIDTH), BF16),
        grid=(n_b, N_KV_HEADS // kv_groups, n_qt),
        in_specs=in_specs,
        out_specs=pl.BlockSpec((tq, width), lambda b, g, t: (b * n_qt + t, g)),
        compiler_params=_params(3),
        name="gqa_attn_latent" if latent else "gqa_attn_context",
    )(*args)


MLA_DOWN_EXT = Q_RANK + KV_RANK + LANES
MLA_HEAD_LANES = LANES
MLA_ROPE_LO = MLA_NOPE_DIM


def _mla_proj_kernel(x_ref, sh_ref, sc_ref, wd_ref, qg_ref, wuq_ref, kvg_ref, cos_ref, sin_ref,
                     q_ref, ckv_ref, kpe_ref):
    h = (x_ref[...] * (1.0 + sc_ref[...]) + sh_ref[...]).astype(BF16)
    acc = jnp.dot(h, wd_ref[...], preferred_element_type=F32)
    cos = cos_ref[...]
    sin = sin_ref[...]
    qa = acc[:, :Q_RANK]
    qn = qa * lax.rsqrt(jnp.mean(qa * qa, axis=-1, keepdims=True) + NORM_EPS) * qg_ref[...]
    q = jnp.dot(qn.astype(BF16), wuq_ref[...], preferred_element_type=F32)
    for hd in range(MLA_HEADS):
        t = q[:, hd * LANES:(hd + 1) * LANES]
        q_ref[:, hd * LANES:(hd + 1) * LANES] = _rope_lanes(t, cos, sin, MLA_ROPE_DIM // 2).astype(BF16)
    ckv = acc[:, Q_RANK:Q_RANK + KV_RANK]
    ckv_ref[...] = ckv * lax.rsqrt(jnp.mean(ckv * ckv, axis=-1, keepdims=True) + NORM_EPS) * kvg_ref[...]
    kpe_ref[...] = _rope_lanes(acc[:, Q_RANK + KV_RANK:], cos, sin, MLA_ROPE_DIM // 2)


def _mla_proj(x, mods5, layer, wd_ext, q_gain, wuq_pad, kv_gain, cos_t, sin_t):
    pos_spec = pl.BlockSpec((TM, LANES), lambda i: (_pos_block_of_tile(i), 0))
    return pl.pallas_call(
        _mla_proj_kernel,
        out_shape=(
            jax.ShapeDtypeStruct((N_TOK, MLA_HEADS * LANES), BF16),
            jax.ShapeDtypeStruct((N_TOK, KV_RANK), F32),
            jax.ShapeDtypeStruct((N_TOK, LANES), F32),
        ),
        grid=(TILES,),
        in_specs=[
            _row_spec(D_MODEL), _mod_spec(layer, 0), _mod_spec(layer, 1),
            _full_spec((D_MODEL, MLA_DOWN_EXT)), _full_spec((1, Q_RANK)),
            _full_spec((Q_RANK, MLA_HEADS * LANES)), _full_spec((1, KV_RANK)),
            pos_spec, pos_spec,
        ],
        out_specs=(_row_spec(MLA_HEADS * LANES), _row_spec(KV_RANK), _row_spec(LANES)),
        compiler_params=_params(1),
        name="mla_proj",
    )(x, mods5, mods5, wd_ext, q_gain, wuq_pad, kv_gain, cos_t, sin_t)


def _mla_kv_kernel(c_ref, kpe_ref, wk_ref, wv_ref, k_ref, v_ref):
    c = c_ref[...].astype(BF16)
    kk = jnp.dot(c, wk_ref[...], preferred_element_type=F32)
    vv = jnp.dot(c, wv_ref[...], preferred_element_type=F32)
    kpe = kpe_ref[...]
    for hd in range(MLA_HEADS):
        k_ref[hd] = (kk[:, hd * LANES:(hd + 1) * LANES] + kpe).astype(BF16)
        v_ref[hd] = vv[:, hd * MLA_V_DIM:(hd + 1) * MLA_V_DIM].astype(BF16)


def _mla_kv(c_all, kpe_all, wk_pad, wv):
    n_rows = c_all.shape[0]
    return pl.pallas_call(
        _mla_kv_kernel,
        out_shape=(
            jax.ShapeDtypeStruct((MLA_HEADS, n_rows, LANES), BF16),
            jax.ShapeDtypeStruct((MLA_HEADS, n_rows, MLA_V_DIM), BF16),
        ),
        grid=(n_rows // TM,),
        in_specs=[
            _row_spec(KV_RANK), _row_spec(LANES),
            _full_spec((KV_RANK, MLA_HEADS * LANES)), _full_spec((KV_RANK, MLA_HEADS * MLA_V_DIM)),
        ],
        out_specs=(
            pl.BlockSpec((MLA_HEADS, TM, LANES), lambda i: (0, i, 0)),
            pl.BlockSpec((MLA_HEADS, TM, MLA_V_DIM), lambda i: (0, i, 0)),
        ),
        compiler_params=_params(1),
        name="mla_kv",
    )(c_all, kpe_all, wk_pad, wv)


def _mla_attn_kernel(*refs, tq, n_keys, heads, has_ctx):
    refs = list(refs)
    q_ref, k_ref, v_ref = refs[:3]
    pos = 3
    if has_ctx:
        kc_ref, vc_ref = refs[pos:pos + 2]
        pos += 2
    o_ref = refs[pos]
    outs = []
    for j in range(heads):
        qs = q_ref[:, j * LANES:(j + 1) * LANES]
        chunks = _chunk_list(k_ref.at[j], v_ref.at[j], n_keys, cast=False)
        if has_ctx:
            chunks += _chunk_list(kc_ref.at[j], vc_ref.at[j], PAST_LEN, cast=False)
        outs.append(_attend(qs, chunks, None, MLA_SCALE * LOG2E))
    o_ref[...] = jnp.concatenate(outs, axis=-1).astype(BF16)


def _mla_attention(q, k_all, v_all, *, latent):
    if latent:
        tq = MLA_TQ
        n_b, n_keys, n_qt = DEC_BATCH, DEC_SEQ, DEC_SEQ // tq
        row0 = N_PROMPT // tq
        kv_blk0 = N_PROMPT // DEC_SEQ
        heads = 2
    else:
        tq = ATT_TQ
        n_b, n_keys, n_qt = BATCH, SEQ, 1
        row0 = 0
        kv_blk0 = 0
        heads = 8
    q_spec = pl.BlockSpec((tq, heads * LANES), lambda b, hp, t: (row0 + b * n_qt + t, hp))
    k_spec = pl.BlockSpec((heads, n_keys, LANES), lambda b, hp, t: (hp, kv_blk0 + b, 0))
    v_spec = pl.BlockSpec((heads, n_keys, MLA_V_DIM), lambda b, hp, t: (hp, kv_blk0 + b, 0))
    in_specs = [q_spec, k_spec, v_spec]
    args = [q, k_all, v_all]
    if latent:
        ctx0 = N_TOK // PAST_LEN
        in_specs += [
            pl.BlockSpec((heads, PAST_LEN, LANES), lambda b, hp, t: (hp, ctx0 + b, 0)),
            pl.BlockSpec((heads, PAST_LEN, MLA_V_DIM), lambda b, hp, t: (hp, ctx0 + b, 0)),
        ]
        args += [k_all, v_all]
    return pl.pallas_call(
        functools.partial(_mla_attn_kernel, tq=tq, n_keys=n_keys, heads=heads, has_ctx=latent),
        out_shape=jax.ShapeDtypeStruct((n_b * n_keys, MLA_HEADS * MLA_V_DIM), BF16),
        grid=(n_b, MLA_HEADS // heads, n_qt),
        in_specs=in_specs,
        out_specs=pl.BlockSpec((tq, heads * MLA_V_DIM), lambda b, hp, t: (b * n_qt + t, hp)),
        compiler_params=_params(3),
        name="mla_attn_latent" if latent else "mla_attn_context",
    )(*args)


def _post_attn_kernel(*refs, n_x):
    (op_ref, os_ref), x_refs, refs = refs[:2], refs[2:2 + n_x], refs[2 + n_x:]
    wo_ref, g1_ref, sh2_ref, sc2_ref, lng_ref, lnb_ref, wr_ref, x1_ref, h2_ref, aff_ref, fz_ref = refs
    proj = jnp.dot(_read_rows((op_ref, os_ref)), wo_ref[...], preferred_element_type=F32)
    x1 = _layer_norm(ALPHA * _read_rows(x_refs) + g1_ref[...] * proj, lng_ref[...], lnb_ref[...])
    x1_ref[...] = x1
    h2 = x1 * (1.0 + sc2_ref[...]) + sh2_ref[...]
    _store_token_tiles(h2_ref, h2)
    logits_t = _dot3(wr_ref[...], h2, _NT)
    e = jnp.exp(logits_t - jnp.max(logits_t, axis=0, keepdims=True))
    aff_ref[...] = e / jnp.sum(e, axis=0, keepdims=True)
    fz_ref[...] = jnp.zeros_like(fz_ref)


def _post_attn(o_p, o_s, x, w_o, mods5, layer, ln_g, ln_b, w_router_t):
    x_specs, x_args = _row_inputs(x)
    return pl.pallas_call(
        functools.partial(_post_attn_kernel, n_x=len(x_args)),
        out_shape=(
            jax.ShapeDtypeStruct((N_TOK, D_MODEL), F32),
            jax.ShapeDtypeStruct((N_TOK * TOK_TILE, LANES), F32),
            jax.ShapeDtypeStruct((N_EXPERTS, N_TOK), F32),
            jax.ShapeDtypeStruct((N_TOK * TOK_TILE, LANES), F32),
        ),
        grid=(TILES,),
        in_specs=_stream_specs(D_MODEL) + x_specs + [
            _full_spec((D_MODEL, D_MODEL)),
            _mod_spec(layer, 2), _mod_spec(layer, 3), _mod_spec(layer, 4),
            _full_spec((1, D_MODEL)), _full_spec((1, D_MODEL)), _full_spec((N_EXPERTS, D_MODEL)),
        ],
        out_specs=(_row_spec(D_MODEL), _tok_tile_spec(), pl.BlockSpec((N_EXPERTS, TM), lambda i: (0, i)),
                   _tok_tile_spec()),
        compiler_params=_params(1),
        name="post_attn",
    )(o_p, o_s, *x_args, w_o, mods5, mods5, mods5, ln_g, ln_b, w_router_t)


FFN_ROWS = 512
FFN_STEPS = EXPERT_FF // FF_TILE
ROW_CHUNKS = CAP // FFN_ROWS
GATHER_PER_CHUNK = CAP // (FFN_STEPS * ROW_CHUNKS)


N_DMA_THREADS = 2


def _token_copy(src, src_tok, dst, dst_tok, sem):
    return pltpu.make_async_copy(_token_tiles(src).at[src_tok], _token_tiles(dst).at[dst_tok], sem)


def _wait_buffer(buf_ref, sem):
    pltpu.make_async_copy(buf_ref, buf_ref, sem).wait()


def _ffn_kernel(idx_ref, h_hbm, f_in_hbm, wg_ref, wu_ref, wd_ref, gate_ref, f_hbm,
                xbuf, acc, fbuf, gsem, rsem, wsem):
    del f_in_hbm
    e = pl.program_id(0)
    f = pl.program_id(1)
    buf = e % 2
    nxt = jnp.minimum(e + 1, N_EXPERTS - 1)

    def gather(expert, s, b):
        return _token_copy(h_hbm, idx_ref[expert * CAP + s], xbuf.at[b], s, gsem.at[b])

    @pl.when((e == 0) & (f == 0))
    def _():
        def issue(s, carry):
            gather(0, s, 0).start()
            return carry
        lax.fori_loop(0, CAP, issue, 0, unroll=8)

    @pl.when(f == 0)
    def _():
        _wait_buffer(xbuf.at[buf], gsem.at[buf])
        acc[...] = jnp.zeros_like(acc)

    def rows_copy(r_lo, n_rows, to_hbm):
        def body(i, carry):
            for thread in range(N_DMA_THREADS):
                r = r_lo + i * N_DMA_THREADS + thread
                tok = idx_ref[e * CAP + r]
                if to_hbm:
                    _token_copy(fbuf, r, f_hbm, tok, wsem.at[0]).start(priority=thread)
                else:
                    _token_copy(f_hbm, tok, fbuf, r, rsem.at[0]).start(priority=thread)
            return carry
        lax.fori_loop(0, n_rows // N_DMA_THREADS, body, 0, unroll=8)

    @pl.when(f == FFN_STEPS - 2)
    def _():
        rows_copy(0, CAP, False)

    wg = wg_ref[...].astype(BF16)
    wu = wu_ref[...].astype(BF16)
    wd = wd_ref[...].astype(BF16)
    for c in range(ROW_CHUNKS):
        r0 = c * FFN_ROWS
        for j in range(GATHER_PER_CHUNK):
            gather(nxt, (f * ROW_CHUNKS + c) * GATHER_PER_CHUNK + j, 1 - buf).start(priority=j % N_DMA_THREADS)
        x = _load_token_tiles(xbuf.at[buf], FFN_ROWS, r0).astype(BF16)
        a = jnp.dot(x, wg, preferred_element_type=F32)
        u = jnp.dot(x, wu, preferred_element_type=F32)
        hid = (a / (1.0 + jnp.exp(-a)) * u).astype(BF16)
        acc[r0:r0 + FFN_ROWS, :] += jnp.dot(hid, wd, preferred_element_type=F32)

    @pl.when(f == FFN_STEPS - 1)
    def _():
        _wait_buffer(fbuf, rsem.at[0])
        for k in range(ROW_CHUNKS):
            r0 = k * FFN_ROWS
            g = gate_ref[r0:r0 + FFN_ROWS, :]
            for j in range(TOK_TILE):
                rows = _lane_block_rows(j, FFN_ROWS, r0)
                fbuf[rows, :] = fbuf[rows, :] + acc[r0:r0 + FFN_ROWS, j * LANES:(j + 1) * LANES] * g
            rows_copy(r0, FFN_ROWS, True)
        _wait_buffer(fbuf, wsem.at[0])

    @pl.when((e == N_EXPERTS - 1) & (f == FFN_STEPS - 1))
    def _():
        _wait_buffer(xbuf.at[1 - buf], gsem.at[1 - buf])


def _expert_ffn(idx, h2, f_zero, gate, w_gate, w_up, w_down, layer):
    grid_spec = pltpu.PrefetchScalarGridSpec(
        num_scalar_prefetch=1,
        grid=(N_EXPERTS, FFN_STEPS),
        in_specs=[
            pl.BlockSpec(memory_space=pl.ANY),
            pl.BlockSpec(memory_space=pl.ANY),
            pl.BlockSpec((None, None, D_MODEL, FF_TILE), lambda e, f, idx: (layer, e, 0, f)),
            pl.BlockSpec((None, None, D_MODEL, FF_TILE), lambda e, f, idx: (layer, e, 0, f)),
            pl.BlockSpec((None, None, FF_TILE, D_MODEL), lambda e, f, idx: (layer, e, f, 0)),
            pl.BlockSpec((None, CAP, 1), lambda e, f, idx: (e, 0, 0)),
        ],
        out_specs=pl.BlockSpec(memory_space=pl.ANY),
        scratch_shapes=[
            pltpu.VMEM((2, CAP * TOK_TILE, LANES), F32),
            pltpu.VMEM((CAP, D_MODEL), F32),
            pltpu.VMEM((CAP * TOK_TILE, LANES), F32),
            pltpu.SemaphoreType.DMA((2,)), pltpu.SemaphoreType.DMA((1,)), pltpu.SemaphoreType.DMA((1,)),
        ],
    )
    tiles = (N_TOK, TOK_TILE, LANES)
    f = pl.pallas_call(
        _ffn_kernel,
        out_shape=jax.ShapeDtypeStruct(tiles, F32),
        grid_spec=grid_spec,
        input_output_aliases={2: 0},
        compiler_params=_params(2),
        name="expert_ffn",
    )(idx, h2.reshape(tiles), f_zero.reshape(tiles), w_gate, w_up, w_down, gate)
    return f.reshape(N_TOK * TOK_TILE, LANES)


def _post_moe_kernel(x_ref, f_ref, g2_ref, lng_ref, lnb_ref, *o_refs):
    f = _load_token_tiles(f_ref, TM)
    y = _layer_norm(ALPHA * x_ref[...] + g2_ref[...] * f, lng_ref[...], lnb_ref[...])
    if len(o_refs) == 1:
        o_refs[0][...] = y
    else:
        @pl.when(pl.program_id(0) < PROMPT_TILES)
        def _():
            o_refs[0][...] = y

        @pl.when(pl.program_id(0) >= PROMPT_TILES)
        def _():
            o_refs[1][...] = y


def _post_moe(x1, f, mods5, layer, ln_g, ln_b, split_streams):
    if split_streams:
        out_shape = (jax.ShapeDtypeStruct((N_PROMPT, D_MODEL), F32), jax.ShapeDtypeStruct((N_SAMPLE, D_MODEL), F32))
        out_specs = tuple(_stream_specs(D_MODEL))
    else:
        out_shape = jax.ShapeDtypeStruct((N_TOK, D_MODEL), F32)
        out_specs = _row_spec(D_MODEL)
    return pl.pallas_call(
        _post_moe_kernel,
        out_shape=out_shape,
        grid=(TILES,),
        in_specs=[_row_spec(D_MODEL), _tok_tile_spec(), _mod_spec(layer, 5),
                  _full_spec((1, D_MODEL)), _full_spec((1, D_MODEL))],
        out_specs=out_specs,
        compiler_params=_params(1),
        name="post_moe",
    )(x1, f, mods5, ln_g, ln_b)


def _axial_tables(dim):
    n_rows = DEC_SEQ // GRID_W
    rows = jnp.repeat(jnp.arange(n_rows, dtype=F32), GRID_W)
    cols = jnp.tile(jnp.arange(GRID_W, dtype=F32), n_rows)
    n_freq = dim // 4
    inv_freq = ROPE_THETA ** (-jnp.arange(n_freq, dtype=F32) / n_freq)
    ang = jnp.concatenate([rows[:, None] * inv_freq, cols[:, None] * inv_freq], -1)
    cos, sin = jnp.cos(ang), jnp.sin(ang)
    return jnp.concatenate([cos, cos], -1), jnp.concatenate([-sin, sin], -1)


def _with_identity_rows(cos_l, sin_l):
    return (jnp.concatenate([jnp.ones((TM, LANES), F32), cos_l], 0),
            jnp.concatenate([jnp.zeros((TM, LANES), F32), sin_l], 0))


def _gqa_rope_tables():
    cos, sin = _axial_tables(HEAD_DIM)
    return _with_identity_rows(jnp.tile(cos, (1, 2)), jnp.tile(sin, (1, 2)))


def _mla_rope_tables():
    cos, sin = _axial_tables(MLA_ROPE_DIM)
    pad_hi = LANES - MLA_ROPE_LO - MLA_ROPE_DIM
    cos_l = jnp.concatenate([jnp.ones((DEC_SEQ, MLA_ROPE_LO), F32), cos, jnp.ones((DEC_SEQ, pad_hi), F32)], 1)
    sin_l = jnp.pad(sin, ((0, 0), (MLA_ROPE_LO, pad_hi)))
    return _with_identity_rows(cos_l, sin_l)


def _moe(h2, f_zero, aff_t, w_gate, w_up, w_down, layer):
    gate_p, idx_p = lax.top_k(aff_t[:, :N_PROMPT], CAP_P)
    gate_s, idx_s = lax.top_k(aff_t[:, N_PROMPT:], CAP_S)
    idx = jnp.concatenate([idx_p, idx_s + N_PROMPT], axis=1).reshape(-1)
    gate = jnp.concatenate([gate_p, gate_s], axis=1)[..., None]
    return _expert_ffn(idx, h2, f_zero, gate, w_gate, w_up, w_down, layer)


def kernel(x_prompt, x_sample, cache_gqa_k, cache_gqa_v, cache_mla_ckv, cache_mla_kpe, c, c_ctx,
           ada_w, ada_b, ln_g, ln_b, a_w_qkv, a_q_norm, a_k_norm, a_w_o, b_w_qkv, b_sink, b_w_o,
           mla_w_down, mla_q_norm, mla_w_uq, mla_kv_norm, mla_w_ukv, mla_w_o,
           moe_w_router, moe_w_gate, moe_w_up, moe_w_down):
    x = (x_prompt.reshape(N_PROMPT, D_MODEL), x_sample.reshape(N_SAMPLE, D_MODEL))
    cond = jnp.concatenate([c_ctx[None], c, jnp.zeros((N_COND - 1 - DEC_BATCH, D_MODEL), F32)], 0)
    mods5 = _adaln(cond, ada_w, ada_b).reshape(DEPTH, 6, N_COND, 1, D_MODEL)
    gqa_cos, gqa_sin = _gqa_rope_tables()
    mla_cos, mla_sin = _mla_rope_tables()

    new_k, new_v, new_ckv, new_kpe = [], [], [], []
    gqa_slot = 0
    for i in range(DEPTH):
        kind, j = i % 3, i // 3
        if kind in (0, 1):
            if kind == 0:
                w_qkv, w_o, sink = a_w_qkv[j], a_w_o[j], None
                gain = jnp.concatenate([jnp.tile(a_q_norm[j], N_HEADS), jnp.tile(a_k_norm[j], N_KV_HEADS)])[None]
            else:
                w_qkv, w_o, sink = b_w_qkv[j], b_w_o[j], b_sink[j]
                gain = jnp.ones((1, QK_WIDTH), F32)
            q, kh, vh, k32, v32 = _gqa_proj(x, mods5, i, w_qkv.astype(BF16), gain, gqa_cos, gqa_sin,
                                            use_norm=(kind == 0))
            o_p = _gqa_attention(q, kh, vh, None, None, gqa_slot, sink, latent=False, window=False)
            o_s = _gqa_attention(q, kh, vh, cache_gqa_k, cache_gqa_v, gqa_slot, sink,
                                 latent=True, window=(kind == 1))
            new_k.append(k32)
            new_v.append(v32)
            gqa_slot += 1
        else:
            w_down = mla_w_down[j]
            rope_lo = Q_RANK + KV_RANK
            wd_ext = jnp.concatenate([
                w_down[:, :rope_lo], jnp.zeros((D_MODEL, MLA_ROPE_LO), F32), w_down[:, rope_lo:],
                jnp.zeros((D_MODEL, LANES - MLA_ROPE_LO - MLA_ROPE_DIM), F32)], 1).astype(BF16)
            wuq_pad = jnp.pad(mla_w_uq[j].reshape(Q_RANK, MLA_HEADS, MLA_QK_DIM),
                              ((0, 0), (0, 0), (0, LANES - MLA_QK_DIM))).reshape(Q_RANK, MLA_HEADS * LANES)
            w_ukv = mla_w_ukv[j].reshape(KV_RANK, MLA_HEADS, MLA_NOPE_DIM + MLA_V_DIM)
            wk_pad = jnp.pad(w_ukv[:, :, :MLA_NOPE_DIM], ((0, 0), (0, 0), (0, LANES - MLA_NOPE_DIM)))
            wk_pad = wk_pad.reshape(KV_RANK, MLA_HEADS * LANES).astype(BF16)
            wv = w_ukv[:, :, MLA_NOPE_DIM:].reshape(KV_RANK, MLA_HEADS * MLA_V_DIM).astype(BF16)
            q, ckv, kpe = _mla_proj(x, mods5, i, wd_ext, mla_q_norm[j][None], wuq_pad.astype(BF16),
                                    mla_kv_norm[j][None], mla_cos, mla_sin)
            c_all = jnp.concatenate([ckv, cache_mla_ckv[:, j].reshape(DEC_BATCH * PAST_LEN, KV_RANK)], 0)
            kpe_ctx = jnp.pad(cache_mla_kpe[:, j].reshape(DEC_BATCH * PAST_LEN, MLA_ROPE_DIM),
                              ((0, 0), (MLA_ROPE_LO, LANES - MLA_ROPE_LO - MLA_ROPE_DIM)))
            kpe_all = jnp.concatenate([kpe, kpe_ctx], 0)
            k_all, v_all = _mla_kv(c_all, kpe_all, wk_pad, wv)
            o_p = _mla_attention(q, k_all, v_all, latent=False)
            o_s = _mla_attention(q, k_all, v_all, latent=True)
            w_o = mla_w_o[j]
            new_ckv.append(ckv[:N_PROMPT].reshape(BATCH, SEQ, KV_RANK))
            new_kpe.append(kpe[:N_PROMPT, MLA_ROPE_LO:MLA_ROPE_LO + MLA_ROPE_DIM].reshape(BATCH, SEQ, MLA_ROPE_DIM))
        x1, h2, aff_t, f_zero = _post_attn(o_p, o_s, x, w_o.astype(BF16), mods5, i, ln_g[i, 0][None],
                                           ln_b[i, 0][None], moe_w_router[i].T)
        f = _moe(h2, f_zero, aff_t, moe_w_gate, moe_w_up, moe_w_down, i)
        x = _post_moe(x1, f, mods5, i, ln_g[i, 1][None], ln_b[i, 1][None], split_streams=(i == DEPTH - 1))

    y_prompt = x[0].reshape(BATCH, SEQ, D_MODEL)
    y_sample = x[1].reshape(DEC_BATCH, DEC_SEQ, D_MODEL)
    return (y_prompt, y_sample, jnp.stack(new_k, 1), jnp.stack(new_v, 1),
            jnp.stack(new_ckv, 1), jnp.stack(new_kpe, 1))
```

```python
import functools
import math

import jax
import jax.numpy as jnp
from jax import lax
from jax.experimental import pallas as pl
from jax.experimental.pallas import tpu as pltpu

F32 = jnp.float32
BF16 = jnp.bfloat16

D_MODEL = 1024
BATCH = 16
SEQ = 256
DEPTH = 4
DEC_BATCH = 4
DEC_SEQ = 2048
PAST_LEN = 512
GRID_W = 64
N_HEADS = 16
N_KV_HEADS = 4
GROUP = N_HEADS // N_KV_HEADS
HEAD_DIM = 64
WINDOW = 128
MLA_HEADS = 16
Q_RANK = 256
KV_RANK = 128
MLA_NOPE_DIM = 64
MLA_ROPE_DIM = 32
MLA_V_DIM = 64
MLA_QK_DIM = MLA_NOPE_DIM + MLA_ROPE_DIM
MLA_SCALE = MLA_QK_DIM ** -0.5
GQA_SCALE = HEAD_DIM ** -0.5
N_EXPERTS = 16
EXPERT_FF = 2048
CAPACITY_FACTOR = 2
ROPE_THETA = 10000.0
NORM_EPS = 1e-6
LN_EPS = 1e-5
ALPHA = (2 * DEPTH) ** 0.25
LOG2E = math.log2(math.e)
GQA_Q_SCALE = GQA_SCALE * LOG2E
MLA_Q_SCALE = MLA_SCALE * LOG2E

LANES = 128
N_PROMPT = BATCH * SEQ
N_SAMPLE = DEC_BATCH * DEC_SEQ
N_TOK = N_PROMPT + N_SAMPLE
TM = 512
ATT_TQ = SEQ
PROMPT_TILES = N_PROMPT // TM
TILES = N_TOK // TM
SAMPLE_TILES_PER_REQ = DEC_SEQ // TM
N_COND = 8
CAP_P = CAPACITY_FACTOR * N_PROMPT // N_EXPERTS
CAP_S = CAPACITY_FACTOR * N_SAMPLE // N_EXPERTS
CAP = CAP_P + CAP_S
FF_TILE = 512
KV_CHUNK = 1024
GQA_TQ = 512
MLA_TQ = 1024

_NT = (((1,), (1,)), ((), ()))
_VMEM_LIMIT = 56 * 1024 * 1024


def _params(n_axes, vmem=_VMEM_LIMIT):
    return pltpu.CompilerParams(dimension_semantics=("arbitrary",) * n_axes, vmem_limit_bytes=vmem)


def _cond_of_tile(i):
    return jnp.where(i < PROMPT_TILES, 0, 1 + (i - PROMPT_TILES) // SAMPLE_TILES_PER_REQ)


def _pos_block_of_tile(i):
    return jnp.where(i < PROMPT_TILES, 0, 1 + (i - PROMPT_TILES) % SAMPLE_TILES_PER_REQ)


def _mod_spec(layer, k):
    return pl.BlockSpec((None, None, None, 1, D_MODEL), lambda i: (layer, k, _cond_of_tile(i), 0, 0))


def _row_spec(width):
    return pl.BlockSpec((TM, width), lambda i: (i, 0))


def _full_spec(shape):
    nd = len(shape)
    return pl.BlockSpec(shape, lambda *_: (0,) * nd)


def _stream_specs(width):
    return [pl.BlockSpec((TM, width), lambda i: (jnp.minimum(i, PROMPT_TILES - 1), 0)),
            pl.BlockSpec((TM, width), lambda i: (jnp.maximum(i - PROMPT_TILES, 0), 0))]


def _read_rows(refs):
    if len(refs) == 1:
        return refs[0][...]
    return jnp.where(pl.program_id(0) < PROMPT_TILES, refs[0][...], refs[1][...])


def _row_inputs(x):
    if isinstance(x, tuple):
        return _stream_specs(x[0].shape[1]), list(x)
    return [_row_spec(x.shape[1])], [x]


TOK_TILE = 8


def _tok_tile_spec():
    return pl.BlockSpec((TM * TOK_TILE, LANES), lambda i: (i, 0))


def _token_tiles(ref):
    if len(ref.shape) == 3:
        return ref
    return ref.reshape(ref.shape[0] // TOK_TILE, TOK_TILE, LANES)


def _lane_block_rows(j, n_tokens, t0=0):
    return pl.ds(t0 * TOK_TILE + j, n_tokens, stride=TOK_TILE)


def _store_token_tiles(ref, val):
    for j in range(TOK_TILE):
        ref[_lane_block_rows(j, val.shape[0]), :] = val[:, j * LANES:(j + 1) * LANES]


def _load_token_tiles(ref, n_tokens, t0=0):
    return jnp.concatenate([ref[_lane_block_rows(j, n_tokens, t0), :] for j in range(TOK_TILE)], axis=-1)


def _split_bf16(a):
    hi = a.astype(BF16)
    lo = (a - hi.astype(F32)).astype(BF16)
    return hi, lo


def _dot3(a, b, dims=(((1,), (0,)), ((), ()))):
    a_hi, a_lo = _split_bf16(a)
    b_hi, b_lo = _split_bf16(b)
    dg = functools.partial(lax.dot_general, dimension_numbers=dims, preferred_element_type=F32)
    return dg(a_hi, b_hi) + dg(a_lo, b_hi) + dg(a_hi, b_lo)


def _layer_norm(y, g, b):
    mu = jnp.mean(y, axis=-1, keepdims=True)
    d = y - mu
    var = jnp.mean(d * d, axis=-1, keepdims=True)
    return d * lax.rsqrt(var + LN_EPS) * g + b


def _rope_lanes(t, cos, sin, half):
    lane = lax.broadcasted_iota(jnp.int32, t.shape, 1)
    first = (lane & (2 * half - 1)) < half
    partner = jnp.where(first, pltpu.roll(t, LANES - half, 1), pltpu.roll(t, half, 1))
    return t * cos + partner * sin


def _pair_rms(t, gain):
    sq = t * t
    lane = lax.broadcasted_iota(jnp.int32, t.shape, 1)
    lo = lane < HEAD_DIM
    s_lo = jnp.sum(jnp.where(lo, sq, 0.0), axis=-1, keepdims=True)
    s_hi = jnp.sum(jnp.where(lo, 0.0, sq), axis=-1, keepdims=True)
    ms = jnp.where(lo, s_lo, s_hi) * (1.0 / HEAD_DIM)
    return t * lax.rsqrt(ms + NORM_EPS) * gain


def _adaln_kernel(c_ref, w_ref, b_ref, o_ref):
    c = c_ref[...]
    a = c / (1.0 + jnp.exp(-c))
    o_ref[...] = _dot3(a, w_ref[...]) + b_ref[...]


def _adaln(cond, ada_w, ada_b):
    return pl.pallas_call(
        _adaln_kernel,
        out_shape=jax.ShapeDtypeStruct((DEPTH, 6, N_COND, D_MODEL), F32),
        grid=(DEPTH, 6),
        in_specs=[
            pl.BlockSpec((N_COND, D_MODEL), lambda l, k: (0, 0)),
            pl.BlockSpec((None, D_MODEL, D_MODEL), lambda l, k: (l, 0, k)),
            pl.BlockSpec((None, None, 1, D_MODEL), lambda l, k: (l, k, 0, 0)),
        ],
        out_specs=pl.BlockSpec((None, None, N_COND, D_MODEL), lambda l, k: (l, k, 0, 0)),
        compiler_params=_params(2),
        name="adaln",
    )(cond, ada_w, ada_b.reshape(DEPTH, 6, 1, D_MODEL))


QK_WIDTH = (N_HEADS + N_KV_HEADS) * HEAD_DIM
QKV_WIDTH = (N_HEADS + 2 * N_KV_HEADS) * HEAD_DIM
Q_WIDTH = N_HEADS * HEAD_DIM


def _gqa_proj_kernel(*refs, use_norm, n_x):
    x_refs, refs = refs[:n_x], refs[n_x:]
    sh_ref, sc_ref, w_ref, gain_ref, cos_ref, sin_ref, q_ref, kh_ref, vh_ref, k32_ref, v32_ref = refs
    h = (_read_rows(x_refs) * (1.0 + sc_ref[...]) + sh_ref[...]).astype(BF16)
    acc = jnp.dot(h, w_ref[...], preferred_element_type=F32)
    cos = cos_ref[...]
    sin = sin_ref[...]
    k_heads, v_heads = [], []
    for j in range(QK_WIDTH // LANES):
        t = acc[:, j * LANES:(j + 1) * LANES]
        if use_norm:
            t = _pair_rms(t, gain_ref[:, j * LANES:(j + 1) * LANES])
        t = _rope_lanes(t, cos, sin, HEAD_DIM // 2)
        if j < Q_WIDTH // LANES:
            q_ref[:, j * LANES:(j + 1) * LANES] = (t * GQA_Q_SCALE).astype(BF16)
        else:
            for half in range(2):
                th = t[:, half * HEAD_DIM:(half + 1) * HEAD_DIM]
                kh_ref[len(k_heads)] = th.astype(BF16)
                k_heads.append(th)
    for kv_head in range(N_KV_HEADS):
        lo = QK_WIDTH + kv_head * HEAD_DIM
        tv = acc[:, lo:lo + HEAD_DIM]
        vh_ref[kv_head] = tv.astype(BF16)
        v_heads.append(tv)

    @pl.when(pl.program_id(0) < PROMPT_TILES)
    def _():
        for req in range(TM // SEQ):
            for kv_head in range(N_KV_HEADS):
                k32_ref[req, kv_head] = k_heads[kv_head][req * SEQ:(req + 1) * SEQ, :]
                v32_ref[req, kv_head] = v_heads[kv_head][req * SEQ:(req + 1) * SEQ, :]


def _gqa_proj(x, mods5, layer, w_qkv, gain, cos_t, sin_t, use_norm):
    head_spec = pl.BlockSpec((N_KV_HEADS, TM, HEAD_DIM), lambda i: (0, i, 0))
    pos_spec = pl.BlockSpec((TM, LANES), lambda i: (_pos_block_of_tile(i), 0))
    cache_spec = pl.BlockSpec((TM // SEQ, N_KV_HEADS, SEQ, HEAD_DIM),
                              lambda i: (jnp.minimum(i, PROMPT_TILES - 1), 0, 0, 0))
    x_specs, x_args = _row_inputs(x)
    return pl.pallas_call(
        functools.partial(_gqa_proj_kernel, use_norm=use_norm, n_x=len(x_args)),
        out_shape=(
            jax.ShapeDtypeStruct((N_TOK, Q_WIDTH), BF16),
            jax.ShapeDtypeStruct((N_KV_HEADS, N_TOK, HEAD_DIM), BF16),
            jax.ShapeDtypeStruct((N_KV_HEADS, N_TOK, HEAD_DIM), BF16),
            jax.ShapeDtypeStruct((BATCH, N_KV_HEADS, SEQ, HEAD_DIM), F32),
            jax.ShapeDtypeStruct((BATCH, N_KV_HEADS, SEQ, HEAD_DIM), F32),
        ),
        grid=(TILES,),
        in_specs=x_specs + [
            _mod_spec(layer, 0), _mod_spec(layer, 1),
            _full_spec((D_MODEL, QKV_WIDTH)), _full_spec((1, QK_WIDTH)),
            pos_spec, pos_spec,
        ],
        out_specs=(_row_spec(Q_WIDTH), head_spec, head_spec, cache_spec, cache_spec),
        compiler_params=_params(1),
        name="gqa_proj",
    )(*x_args, mods5, mods5, w_qkv, gain, cos_t, sin_t)


def _attend(qs, chunks, sink_col):
    m = den = acc = None
    for k_fn, v_fn, bias, _ in chunks:
        s = lax.dot_general(qs, k_fn(), _NT, preferred_element_type=F32)
        if bias is not None:
            s = s + bias
        mc = jnp.max(s, axis=-1, keepdims=True)
        m_new = mc if m is None else jnp.maximum(m, mc)
        p = jnp.exp2(s - m_new)
        ps = jnp.sum(p, axis=-1, keepdims=True)
        pv = jnp.dot(p.astype(BF16), v_fn(), preferred_element_type=F32)
        if m is None:
            den, acc = ps, pv
        else:
            alpha = jnp.exp2(m - m_new)
            den = den * alpha + ps
            acc = acc * alpha + pv
        m = m_new
    if sink_col is not None:
        den = den + jnp.exp2(sink_col - m)
    return acc / den


def _chunk_list(k_ref, v_ref, length, cast):
    out = []
    width = min(KV_CHUNK, length)
    for c0 in range(0, length, width):
        if cast:
            k_fn = lambda c0=c0: k_ref[c0:c0 + width, :].astype(BF16)
            v_fn = lambda c0=c0: v_ref[c0:c0 + width, :].astype(BF16)
        else:
            k_fn = lambda c0=c0: k_ref[c0:c0 + width, :]
            v_fn = lambda c0=c0: v_ref[c0:c0 + width, :]
        out.append((k_fn, v_fn, None, width))
    return out


def _gqa_attn_kernel(*refs, tq, n_keys, kv_groups, has_ctx, use_sink, window):
    refs = list(refs)
    q_ref, k_ref, v_ref = refs[:3]
    pos = 3
    if has_ctx:
        kc_ref, vc_ref = refs[pos:pos + 2]
        pos += 2
    if use_sink:
        sink_ref = refs[pos]
        pos += 1
    o_ref = refs[pos]
    qi = pl.program_id(2)
    width = GROUP * HEAD_DIM

    for gi in range(kv_groups):
        g = pl.program_id(1) * kv_groups + gi
        qf = q_ref[:, gi * width:(gi + 1) * width].astype(F32)
        qs = jnp.concatenate(
            [qf[:, j * HEAD_DIM:(j + 1) * HEAD_DIM] for j in range(GROUP)], axis=0).astype(BF16)
        kg_ref, vg_ref = k_ref.at[gi], v_ref.at[gi]

        if window:
            span = tq + 2 * WINDOW
            kstart = pl.multiple_of(jnp.clip(qi * tq - WINDOW, 0, n_keys - span), LANES)
            qpos = qi * tq + lax.broadcasted_iota(jnp.int32, (tq, span), 0)
            kpos = kstart + lax.broadcasted_iota(jnp.int32, (tq, span), 1)
            band = jnp.where(jnp.abs(kpos - qpos) <= WINDOW, 0.0, -jnp.inf).astype(F32)
            bias = jnp.concatenate([band] * GROUP, axis=0)
            chunks = [(lambda: kg_ref[pl.ds(kstart, span), :], lambda: vg_ref[pl.ds(kstart, span), :], bias, span)]
        else:
            chunks = _chunk_list(kg_ref, vg_ref, n_keys, cast=False)
        if has_ctx:
            chunks += _chunk_list(kc_ref.at[gi], vc_ref.at[gi], PAST_LEN, cast=True)

        sink_col = None
        if use_sink:
            row = lax.broadcasted_iota(jnp.int32, (GROUP * tq, 1), 0)
            sink_col = jnp.full((GROUP * tq, 1), sink_ref[g * GROUP + GROUP - 1], F32)
            for j in reversed(range(GROUP - 1)):
                sink_col = jnp.where(row < (j + 1) * tq, sink_ref[g * GROUP + j], sink_col)
            sink_col = sink_col * LOG2E

        o = _attend(qs, chunks, sink_col)
        o_ref[:, gi * width:(gi + 1) * width] = jnp.concatenate(
            [o[j * tq:(j + 1) * tq, :] for j in range(GROUP)], axis=-1).astype(BF16)


def _gqa_attention(q, kh, vh, ctx_k, ctx_v, slot, sink, *, latent, window):
    tq = GQA_TQ if latent and not window else ATT_TQ
    if latent:
        n_b, n_keys, n_qt = DEC_BATCH, DEC_SEQ, DEC_SEQ // tq
        row0 = N_PROMPT // tq
        kv_blk0 = N_PROMPT // DEC_SEQ
        kv_groups = 2
    else:
        n_b, n_keys, n_qt = BATCH, SEQ, 1
        row0 = 0
        kv_blk0 = 0
        kv_groups = N_KV_HEADS
    width = kv_groups * GROUP * HEAD_DIM
    q_spec = pl.BlockSpec((tq, width), lambda b, g, t: (row0 + b * n_qt + t, g))
    kv_spec = pl.BlockSpec((kv_groups, n_keys, HEAD_DIM), lambda b, g, t: (g, kv_blk0 + b, 0))
    in_specs = [q_spec, kv_spec, kv_spec]
    args = [q, kh, vh]
    if latent:
        ctx_spec = pl.BlockSpec((None, None, kv_groups, PAST_LEN, HEAD_DIM), lambda b, g, t: (b, slot, g, 0, 0))
        in_specs += [ctx_spec, ctx_spec]
        args += [ctx_k, ctx_v]
    use_sink = sink is not None
    if use_sink:
        in_specs.append(pl.BlockSpec(memory_space=pltpu.SMEM))
        args.append(sink)
    return pl.pallas_call(
        functools.partial(_gqa_attn_kernel, tq=tq, n_keys=n_keys, kv_groups=kv_groups, has_ctx=latent,
                          use_sink=use_sink, window=window),
        out_shape=jax.ShapeDtypeStruct((n_b * n_keys, Q_WIDTH), BF16),
        grid=(n_b, N_KV_HEADS // kv_groups, n_qt),
        in_specs=in_specs,
        out_specs=pl.BlockSpec((tq, width), lambda b, g, t: (b * n_qt + t, g)),
        compiler_params=_params(3),
        name="gqa_attn_latent" if latent else "gqa_attn_context",
    )(*args)


MLA_DOWN_EXT = Q_RANK + KV_RANK + LANES
MLA_HEAD_LANES = LANES
MLA_ROPE_LO = MLA_NOPE_DIM


def _mla_proj_kernel(x_ref, sh_ref, sc_ref, wd_ref, qg_ref, wuq_ref, kvg_ref, cos_ref, sin_ref,
                     q_ref, ckv_ref, kpe_ref):
    h = (x_ref[...] * (1.0 + sc_ref[...]) + sh_ref[...]).astype(BF16)
    acc = jnp.dot(h, wd_ref[...], preferred_element_type=F32)
    cos = cos_ref[...]
    sin = sin_ref[...]
    qa = acc[:, :Q_RANK]
    qn = qa * lax.rsqrt(jnp.mean(qa * qa, axis=-1, keepdims=True) + NORM_EPS) * qg_ref[...]
    q = jnp.dot(qn.astype(BF16), wuq_ref[...], preferred_element_type=F32)
    for hd in range(MLA_HEADS):
        t = q[:, hd * LANES:(hd + 1) * LANES]
        q_ref[:, hd * LANES:(hd + 1) * LANES] = (_rope_lanes(t, cos, sin, MLA_ROPE_DIM // 2) * MLA_Q_SCALE).astype(BF16)
    ckv = acc[:, Q_RANK:Q_RANK + KV_RANK]
    ckv_ref[...] = ckv * lax.rsqrt(jnp.mean(ckv * ckv, axis=-1, keepdims=True) + NORM_EPS) * kvg_ref[...]
    kpe_ref[...] = _rope_lanes(acc[:, Q_RANK + KV_RANK:], cos, sin, MLA_ROPE_DIM // 2)


def _mla_proj(x, mods5, layer, wd_ext, q_gain, wuq_pad, kv_gain, cos_t, sin_t):
    pos_spec = pl.BlockSpec((TM, LANES), lambda i: (_pos_block_of_tile(i), 0))
    return pl.pallas_call(
        _mla_proj_kernel,
        out_shape=(
            jax.ShapeDtypeStruct((N_TOK, MLA_HEADS * LANES), BF16),
            jax.ShapeDtypeStruct((N_TOK, KV_RANK), F32),
            jax.ShapeDtypeStruct((N_TOK, LANES), F32),
        ),
        grid=(TILES,),
        in_specs=[
            _row_spec(D_MODEL), _mod_spec(layer, 0), _mod_spec(layer, 1),
            _full_spec((D_MODEL, MLA_DOWN_EXT)), _full_spec((1, Q_RANK)),
            _full_spec((Q_RANK, MLA_HEADS * LANES)), _full_spec((1, KV_RANK)),
            pos_spec, pos_spec,
        ],
        out_specs=(_row_spec(MLA_HEADS * LANES), _row_spec(KV_RANK), _row_spec(LANES)),
        compiler_params=_params(1),
        name="mla_proj",
    )(x, mods5, mods5, wd_ext, q_gain, wuq_pad, kv_gain, cos_t, sin_t)


def _mla_kv_kernel(c_ref, kpe_ref, wk_ref, wv_ref, k_ref, v_ref):
    c = c_ref[...].astype(BF16)
    kk = jnp.dot(c, wk_ref[...], preferred_element_type=F32)
    vv = jnp.dot(c, wv_ref[...], preferred_element_type=F32)
    kpe = kpe_ref[...]
    for hd in range(MLA_HEADS):
        k_ref[hd] = (kk[:, hd * LANES:(hd + 1) * LANES] + kpe).astype(BF16)
        v_ref[hd] = vv[:, hd * MLA_V_DIM:(hd + 1) * MLA_V_DIM].astype(BF16)


def _mla_kv(c_all, kpe_all, wk_pad, wv):
    n_rows = c_all.shape[0]
    return pl.pallas_call(
        _mla_kv_kernel,
        out_shape=(
            jax.ShapeDtypeStruct((MLA_HEADS, n_rows, LANES), BF16),
            jax.ShapeDtypeStruct((MLA_HEADS, n_rows, MLA_V_DIM), BF16),
        ),
        grid=(n_rows // TM,),
        in_specs=[
            _row_spec(KV_RANK), _row_spec(LANES),
            _full_spec((KV_RANK, MLA_HEADS * LANES)), _full_spec((KV_RANK, MLA_HEADS * MLA_V_DIM)),
        ],
        out_specs=(
            pl.BlockSpec((MLA_HEADS, TM, LANES), lambda i: (0, i, 0)),
            pl.BlockSpec((MLA_HEADS, TM, MLA_V_DIM), lambda i: (0, i, 0)),
        ),
        compiler_params=_params(1),
        name="mla_kv",
    )(c_all, kpe_all, wk_pad, wv)


def _mla_attn_kernel(*refs, tq, n_keys, heads, has_ctx):
    refs = list(refs)
    q_ref, k_ref, v_ref = refs[:3]
    pos = 3
    if has_ctx:
        kc_ref, vc_ref = refs[pos:pos + 2]
        pos += 2
    o_ref = refs[pos]
    outs = []
    for j in range(heads):
        qs = q_ref[:, j * LANES:(j + 1) * LANES]
        chunks = _chunk_list(k_ref.at[j], v_ref.at[j], n_keys, cast=False)
        if has_ctx:
            chunks += _chunk_list(kc_ref.at[j], vc_ref.at[j], PAST_LEN, cast=False)
        outs.append(_attend(qs, chunks, None))
    o_ref[...] = jnp.concatenate(outs, axis=-1).astype(BF16)


def _mla_attention(q, k_all, v_all, *, latent):
    if latent:
        tq = MLA_TQ
        n_b, n_keys, n_qt = DEC_BATCH, DEC_SEQ, DEC_SEQ // tq
        row0 = N_PROMPT // tq
        kv_blk0 = N_PROMPT // DEC_SEQ
        heads = 4
    else:
        tq = ATT_TQ
        n_b, n_keys, n_qt = BATCH, SEQ, 1
        row0 = 0
        kv_blk0 = 0
        heads = 8
    q_spec = pl.BlockSpec((tq, heads * LANES), lambda b, hp, t: (row0 + b * n_qt + t, hp))
    k_spec = pl.BlockSpec((heads, n_keys, LANES), lambda b, hp, t: (hp, kv_blk0 + b, 0))
    v_spec = pl.BlockSpec((heads, n_keys, MLA_V_DIM), lambda b, hp, t: (hp, kv_blk0 + b, 0))
    in_specs = [q_spec, k_spec, v_spec]
    args = [q, k_all, v_all]
    if latent:
        ctx0 = N_TOK // PAST_LEN
        in_specs += [
            pl.BlockSpec((heads, PAST_LEN, LANES), lambda b, hp, t: (hp, ctx0 + b, 0)),
            pl.BlockSpec((heads, PAST_LEN, MLA_V_DIM), lambda b, hp, t: (hp, ctx0 + b, 0)),
        ]
        args += [k_all, v_all]
    return pl.pallas_call(
        functools.partial(_mla_attn_kernel, tq=tq, n_keys=n_keys, heads=heads, has_ctx=latent),
        out_shape=jax.ShapeDtypeStruct((n_b * n_keys, MLA_HEADS * MLA_V_DIM), BF16),
        grid=(n_b, MLA_HEADS // heads, n_qt),
        in_specs=in_specs,
        out_specs=pl.BlockSpec((tq, heads * MLA_V_DIM), lambda b, hp, t: (b * n_qt + t, hp)),
        compiler_params=_params(3),
        name="mla_attn_latent" if latent else "mla_attn_context",
    )(*args)


def _post_attn_kernel(*refs, n_x):
    (op_ref, os_ref), x_refs, refs = refs[:2], refs[2:2 + n_x], refs[2 + n_x:]
    wo_ref, g1_ref, sh2_ref, sc2_ref, lng_ref, lnb_ref, wr_ref, x1_ref, h2_ref, aff_ref, fz_ref = refs
    proj = jnp.dot(_read_rows((op_ref, os_ref)), wo_ref[...], preferred_element_type=F32)
    x1 = _layer_norm(ALPHA * _read_rows(x_refs) + g1_ref[...] * proj, lng_ref[...], lnb_ref[...])
    x1_ref[...] = x1
    h2 = x1 * (1.0 + sc2_ref[...]) + sh2_ref[...]
    _store_token_tiles(h2_ref, h2)
    logits_t = _dot3(wr_ref[...], h2, _NT)
    e = jnp.exp(logits_t - jnp.max(logits_t, axis=0, keepdims=True))
    aff_ref[...] = e / jnp.sum(e, axis=0, keepdims=True)
    fz_ref[...] = jnp.zeros_like(fz_ref)


def _post_attn(o_p, o_s, x, w_o, mods5, layer, ln_g, ln_b, w_router_t):
    x_specs, x_args = _row_inputs(x)
    return pl.pallas_call(
        functools.partial(_post_attn_kernel, n_x=len(x_args)),
        out_shape=(
            jax.ShapeDtypeStruct((N_TOK, D_MODEL), F32),
            jax.ShapeDtypeStruct((N_TOK * TOK_TILE, LANES), F32),
            jax.ShapeDtypeStruct((N_EXPERTS, N_TOK), F32),
            jax.ShapeDtypeStruct((N_TOK * TOK_TILE, LANES), F32),
        ),
        grid=(TILES,),
        in_specs=_stream_specs(D_MODEL) + x_specs + [
            _full_spec((D_MODEL, D_MODEL)),
            _mod_spec(layer, 2), _mod_spec(layer, 3), _mod_spec(layer, 4),
            _full_spec((1, D_MODEL)), _full_spec((1, D_MODEL)), _full_spec((N_EXPERTS, D_MODEL)),
        ],
        out_specs=(_row_spec(D_MODEL), _tok_tile_spec(), pl.BlockSpec((N_EXPERTS, TM), lambda i: (0, i)),
                   _tok_tile_spec()),
        compiler_params=_params(1),
        name="post_attn",
    )(o_p, o_s, *x_args, w_o, mods5, mods5, mods5, ln_g, ln_b, w_router_t)


FFN_ROWS = 512
FFN_STEPS = EXPERT_FF // FF_TILE
ROW_CHUNKS = CAP // FFN_ROWS
GATHER_PER_CHUNK = CAP // (FFN_STEPS * ROW_CHUNKS)


N_DMA_THREADS = 2


def _token_copy(src, src_tok, dst, dst_tok, sem):
    return pltpu.make_async_copy(_token_tiles(src).at[src_tok], _token_tiles(dst).at[dst_tok], sem)


def _wait_buffer(buf_ref, sem):
    pltpu.make_async_copy(buf_ref, buf_ref, sem).wait()


def _ffn_kernel(idx_ref, h_hbm, f_in_hbm, wg_ref, wu_ref, wd_ref, gate_ref, f_hbm,
                xbuf, acc, fbuf, gsem, rsem, wsem):
    del f_in_hbm
    e = pl.program_id(0)
    f = pl.program_id(1)
    buf = e % 2
    nxt = jnp.minimum(e + 1, N_EXPERTS - 1)

    def gather(expert, s, b):
        return _token_copy(h_hbm, idx_ref[expert * CAP + s], xbuf.at[b], s, gsem.at[b])

    @pl.when((e == 0) & (f == 0))
    def _():
        def issue(s, carry):
            gather(0, s, 0).start()
            return carry
        lax.fori_loop(0, CAP, issue, 0, unroll=8)

    @pl.when(f == 0)
    def _():
        _wait_buffer(xbuf.at[buf], gsem.at[buf])
        acc[...] = jnp.zeros_like(acc)

    def rows_copy(r_lo, n_rows, to_hbm):
        def body(i, carry):
            for thread in range(N_DMA_THREADS):
                r = r_lo + i * N_DMA_THREADS + thread
                tok = idx_ref[e * CAP + r]
                if to_hbm:
                    _token_copy(fbuf, r, f_hbm, tok, wsem.at[0]).start(priority=thread)
                else:
                    _token_copy(f_hbm, tok, fbuf, r, rsem.at[0]).start(priority=thread)
            return carry
        lax.fori_loop(0, n_rows // N_DMA_THREADS, body, 0, unroll=8)

    @pl.when(f == FFN_STEPS - 2)
    def _():
        rows_copy(0, CAP, False)

    wg = wg_ref[...].astype(BF16)
    wu = wu_ref[...].astype(BF16)
    wd = wd_ref[...].astype(BF16)
    for c in range(ROW_CHUNKS):
        r0 = c * FFN_ROWS
        for j in range(GATHER_PER_CHUNK):
            gather(nxt, (f * ROW_CHUNKS + c) * GATHER_PER_CHUNK + j, 1 - buf).start(priority=j % N_DMA_THREADS)
        x = _load_token_tiles(xbuf.at[buf], FFN_ROWS, r0).astype(BF16)
        a = jnp.dot(x, wg, preferred_element_type=F32)
        u = jnp.dot(x, wu, preferred_element_type=F32)
        hid = (a / (1.0 + jnp.exp(-a)) * u).astype(BF16)
        acc[r0:r0 + FFN_ROWS, :] += jnp.dot(hid, wd, preferred_element_type=F32)

    @pl.when(f == FFN_STEPS - 1)
    def _():
        _wait_buffer(fbuf, rsem.at[0])
        for k in range(ROW_CHUNKS):
            r0 = k * FFN_ROWS
            g = gate_ref[r0:r0 + FFN_ROWS, :]
            for j in range(TOK_TILE):
                rows = _lane_block_rows(j, FFN_ROWS, r0)
                fbuf[rows, :] = fbuf[rows, :] + acc[r0:r0 + FFN_ROWS, j * LANES:(j + 1) * LANES] * g
            rows_copy(r0, FFN_ROWS, True)
        _wait_buffer(fbuf, wsem.at[0])

    @pl.when((e == N_EXPERTS - 1) & (f == FFN_STEPS - 1))
    def _():
        _wait_buffer(xbuf.at[1 - buf], gsem.at[1 - buf])


def _expert_ffn(idx, h2, f_zero, gate, w_gate, w_up, w_down, layer):
    grid_spec = pltpu.PrefetchScalarGridSpec(
        num_scalar_prefetch=1,
        grid=(N_EXPERTS, FFN_STEPS),
        in_specs=[
            pl.BlockSpec(memory_space=pl.ANY),
            pl.BlockSpec(memory_space=pl.ANY),
            pl.BlockSpec((None, None, D_MODEL, FF_TILE), lambda e, f, idx: (layer, e, 0, f)),
            pl.BlockSpec((None, None, D_MODEL, FF_TILE), lambda e, f, idx: (layer, e, 0, f)),
            pl.BlockSpec((None, None, FF_TILE, D_MODEL), lambda e, f, idx: (layer, e, f, 0)),
            pl.BlockSpec((None, CAP, 1), lambda e, f, idx: (e, 0, 0)),
        ],
        out_specs=pl.BlockSpec(memory_space=pl.ANY),
        scratch_shapes=[
            pltpu.VMEM((2, CAP * TOK_TILE, LANES), F32),
            pltpu.VMEM((CAP, D_MODEL), F32),
            pltpu.VMEM((CAP * TOK_TILE, LANES), F32),
            pltpu.SemaphoreType.DMA((2,)), pltpu.SemaphoreType.DMA((1,)), pltpu.SemaphoreType.DMA((1,)),
        ],
    )
    tiles = (N_TOK, TOK_TILE, LANES)
    f = pl.pallas_call(
        _ffn_kernel,
        out_shape=jax.ShapeDtypeStruct(tiles, F32),
        grid_spec=grid_spec,
        input_output_aliases={2: 0},
        compiler_params=_params(2),
        name="expert_ffn",
    )(idx, h2.reshape(tiles), f_zero.reshape(tiles), w_gate, w_up, w_down, gate)
    return f.reshape(N_TOK * TOK_TILE, LANES)


def _post_moe_kernel(x_ref, f_ref, g2_ref, lng_ref, lnb_ref, *o_refs):
    f = _load_token_tiles(f_ref, TM)
    y = _layer_norm(ALPHA * x_ref[...] + g2_ref[...] * f, lng_ref[...], lnb_ref[...])
    if len(o_refs) == 1:
        o_refs[0][...] = y
    else:
        @pl.when(pl.program_id(0) < PROMPT_TILES)
        def _():
            o_refs[0][...] = y

        @pl.when(pl.program_id(0) >= PROMPT_TILES)
        def _():
            o_refs[1][...] = y


def _post_moe(x1, f, mods5, layer, ln_g, ln_b, split_streams):
    if split_streams:
        out_shape = (jax.ShapeDtypeStruct((N_PROMPT, D_MODEL), F32), jax.ShapeDtypeStruct((N_SAMPLE, D_MODEL), F32))
        out_specs = tuple(_stream_specs(D_MODEL))
    else:
        out_shape = jax.ShapeDtypeStruct((N_TOK, D_MODEL), F32)
        out_specs = _row_spec(D_MODEL)
    return pl.pallas_call(
        _post_moe_kernel,
        out_shape=out_shape,
        grid=(TILES,),
        in_specs=[_row_spec(D_MODEL), _tok_tile_spec(), _mod_spec(layer, 5),
                  _full_spec((1, D_MODEL)), _full_spec((1, D_MODEL))],
        out_specs=out_specs,
        compiler_params=_params(1),
        name="post_moe",
    )(x1, f, mods5, ln_g, ln_b)


def _axial_tables(dim):
    n_rows = DEC_SEQ // GRID_W
    rows = jnp.repeat(jnp.arange(n_rows, dtype=F32), GRID_W)
    cols = jnp.tile(jnp.arange(GRID_W, dtype=F32), n_rows)
    n_freq = dim // 4
    inv_freq = ROPE_THETA ** (-jnp.arange(n_freq, dtype=F32) / n_freq)
    ang = jnp.concatenate([rows[:, None] * inv_freq, cols[:, None] * inv_freq], -1)
    cos, sin = jnp.cos(ang), jnp.sin(ang)
    return jnp.concatenate([cos, cos], -1), jnp.concatenate([-sin, sin], -1)


def _with_identity_rows(cos_l, sin_l):
    return (jnp.concatenate([jnp.ones((TM, LANES), F32), cos_l], 0),
            jnp.concatenate([jnp.zeros((TM, LANES), F32), sin_l], 0))


def _gqa_rope_tables():
    cos, sin = _axial_tables(HEAD_DIM)
    return _with_identity_rows(jnp.tile(cos, (1, 2)), jnp.tile(sin, (1, 2)))


def _mla_rope_tables():
    cos, sin = _axial_tables(MLA_ROPE_DIM)
    pad_hi = LANES - MLA_ROPE_LO - MLA_ROPE_DIM
    cos_l = jnp.concatenate([jnp.ones((DEC_SEQ, MLA_ROPE_LO), F32), cos, jnp.ones((DEC_SEQ, pad_hi), F32)], 1)
    sin_l = jnp.pad(sin, ((0, 0), (MLA_ROPE_LO, pad_hi)))
    return _with_identity_rows(cos_l, sin_l)


def _moe(h2, f_zero, aff_t, w_gate, w_up, w_down, layer):
    gate_p, idx_p = lax.top_k(aff_t[:, :N_PROMPT], CAP_P)
    gate_s, idx_s = lax.top_k(aff_t[:, N_PROMPT:], CAP_S)
    idx = jnp.concatenate([idx_p, idx_s + N_PROMPT], axis=1).reshape(-1)
    gate = jnp.concatenate([gate_p, gate_s], axis=1)[..., None]
    return _expert_ffn(idx, h2, f_zero, gate, w_gate, w_up, w_down, layer)


def kernel(x_prompt, x_sample, cache_gqa_k, cache_gqa_v, cache_mla_ckv, cache_mla_kpe, c, c_ctx,
           ada_w, ada_b, ln_g, ln_b, a_w_qkv, a_q_norm, a_k_norm, a_w_o, b_w_qkv, b_sink, b_w_o,
           mla_w_down, mla_q_norm, mla_w_uq, mla_kv_norm, mla_w_ukv, mla_w_o,
           moe_w_router, moe_w_gate, moe_w_up, moe_w_down):
    x = (x_prompt.reshape(N_PROMPT, D_MODEL), x_sample.reshape(N_SAMPLE, D_MODEL))
    cond = jnp.concatenate([c_ctx[None], c, jnp.zeros((N_COND - 1 - DEC_BATCH, D_MODEL), F32)], 0)
    mods5 = _adaln(cond, ada_w, ada_b).reshape(DEPTH, 6, N_COND, 1, D_MODEL)
    gqa_cos, gqa_sin = _gqa_rope_tables()
    mla_cos, mla_sin = _mla_rope_tables()

    new_k, new_v, new_ckv, new_kpe = [], [], [], []
    gqa_slot = 0
    for i in range(DEPTH):
        kind, j = i % 3, i // 3
        if kind in (0, 1):
            if kind == 0:
                w_qkv, w_o, sink = a_w_qkv[j], a_w_o[j], None
                gain = jnp.concatenate([jnp.tile(a_q_norm[j], N_HEADS), jnp.tile(a_k_norm[j], N_KV_HEADS)])[None]
            else:
                w_qkv, w_o, sink = b_w_qkv[j], b_w_o[j], b_sink[j]
                gain = jnp.ones((1, QK_WIDTH), F32)
            q, kh, vh, k32, v32 = _gqa_proj(x, mods5, i, w_qkv.astype(BF16), gain, gqa_cos, gqa_sin,
                                            use_norm=(kind == 0))
            o_p = _gqa_attention(q, kh, vh, None, None, gqa_slot, sink, latent=False, window=False)
            o_s = _gqa_attention(q, kh, vh, cache_gqa_k, cache_gqa_v, gqa_slot, sink,
                                 latent=True, window=(kind == 1))
            new_k.append(k32)
            new_v.append(v32)
            gqa_slot += 1
        else:
            w_down = mla_w_down[j]
            rope_lo = Q_RANK + KV_RANK
            wd_ext = jnp.concatenate([
                w_down[:, :rope_lo], jnp.zeros((D_MODEL, MLA_ROPE_LO), F32), w_down[:, rope_lo:],
                jnp.zeros((D_MODEL, LANES - MLA_ROPE_LO - MLA_ROPE_DIM), F32)], 1).astype(BF16)
            wuq_pad = jnp.pad(mla_w_uq[j].reshape(Q_RANK, MLA_HEADS, MLA_QK_DIM),
                              ((0, 0), (0, 0), (0, LANES - MLA_QK_DIM))).reshape(Q_RANK, MLA_HEADS * LANES)
            w_ukv = mla_w_ukv[j].reshape(KV_RANK, MLA_HEADS, MLA_NOPE_DIM + MLA_V_DIM)
            wk_pad = jnp.pad(w_ukv[:, :, :MLA_NOPE_DIM], ((0, 0), (0, 0), (0, LANES - MLA_NOPE_DIM)))
            wk_pad = wk_pad.reshape(KV_RANK, MLA_HEADS * LANES).astype(BF16)
            wv = w_ukv[:, :, MLA_NOPE_DIM:].reshape(KV_RANK, MLA_HEADS * MLA_V_DIM).astype(BF16)
            q, ckv, kpe = _mla_proj(x, mods5, i, wd_ext, mla_q_norm[j][None], wuq_pad.astype(BF16),
                                    mla_kv_norm[j][None], mla_cos, mla_sin)
            c_all = jnp.concatenate([ckv, cache_mla_ckv[:, j].reshape(DEC_BATCH * PAST_LEN, KV_RANK)], 0)
            kpe_ctx = jnp.pad(cache_mla_kpe[:, j].reshape(DEC_BATCH * PAST_LEN, MLA_ROPE_DIM),
                              ((0, 0), (MLA_ROPE_LO, LANES - MLA_ROPE_LO - MLA_ROPE_DIM)))
            kpe_all = jnp.concatenate([kpe, kpe_ctx], 0)
            k_all, v_all = _mla_kv(c_all, kpe_all, wk_pad, wv)
            o_p = _mla_attention(q, k_all, v_all, latent=False)
            o_s = _mla_attention(q, k_all, v_all, latent=True)
            w_o = mla_w_o[j]
            new_ckv.append(ckv[:N_PROMPT].reshape(BATCH, SEQ, KV_RANK))
            new_kpe.append(kpe[:N_PROMPT, MLA_ROPE_LO:MLA_ROPE_LO + MLA_ROPE_DIM].reshape(BATCH, SEQ, MLA_ROPE_DIM))
        x1, h2, aff_t, f_zero = _post_attn(o_p, o_s, x, w_o.astype(BF16), mods5, i, ln_g[i, 0][None],
                                           ln_b[i, 0][None], moe_w_router[i].T)
        f = _moe(h2, f_zero, aff_t, moe_w_gate, moe_w_up, moe_w_down, i)
        x = _post_moe(x1, f, mods5, i, ln_g[i, 1][None], ln_b[i, 1][None], split_streams=(i == DEPTH - 1))

    y_prompt = x[0].reshape(BATCH, SEQ, D_MODEL)
    y_sample = x[1].reshape(DEC_BATCH, DEC_SEQ, D_MODEL)
    return (y_prompt, y_sample, jnp.stack(new_k, 1), jnp.stack(new_v, 1),
            jnp.stack(new_ckv, 1), jnp.stack(new_kpe, 1))
```

```python
import functools
import math

import jax
import jax.numpy as jnp
from jax import lax
from jax.experimental import pallas as pl
from jax.experimental.pallas import tpu as pltpu

F32 = jnp.float32
BF16 = jnp.bfloat16

D_MODEL = 1024
BATCH = 16
SEQ = 256
DEPTH = 4
DEC_BATCH = 4
DEC_SEQ = 2048
PAST_LEN = 512
GRID_W = 64
N_HEADS = 16
N_KV_HEADS = 4
GROUP = N_HEADS // N_KV_HEADS
HEAD_DIM = 64
WINDOW = 128
MLA_HEADS = 16
Q_RANK = 256
KV_RANK = 128
MLA_NOPE_DIM = 64
MLA_ROPE_DIM = 32
MLA_V_DIM = 64
MLA_QK_DIM = MLA_NOPE_DIM + MLA_ROPE_DIM
MLA_SCALE = MLA_QK_DIM ** -0.5
GQA_SCALE = HEAD_DIM ** -0.5
N_EXPERTS = 16
EXPERT_FF = 2048
CAPACITY_FACTOR = 2
ROPE_THETA = 10000.0
NORM_EPS = 1e-6
LN_EPS = 1e-5
ALPHA = (2 * DEPTH) ** 0.25
LOG2E = math.log2(math.e)
GQA_Q_SCALE = GQA_SCALE * LOG2E
MLA_Q_SCALE = MLA_SCALE * LOG2E

LANES = 128
N_PROMPT = BATCH * SEQ
N_SAMPLE = DEC_BATCH * DEC_SEQ
N_TOK = N_PROMPT + N_SAMPLE
TM = 512
ATT_TQ = SEQ
PROMPT_TILES = N_PROMPT // TM
TILES = N_TOK // TM
SAMPLE_TILES_PER_REQ = DEC_SEQ // TM
N_COND = 8
CAP_P = CAPACITY_FACTOR * N_PROMPT // N_EXPERTS
CAP_S = CAPACITY_FACTOR * N_SAMPLE // N_EXPERTS
CAP = CAP_P + CAP_S
FF_TILE = 512
KV_CHUNK = 1024
GQA_TQ = 512
MLA_TQ = 1024

_NT = (((1,), (1,)), ((), ()))
_VMEM_LIMIT = 56 * 1024 * 1024


def _params(n_axes, vmem=_VMEM_LIMIT):
    return pltpu.CompilerParams(dimension_semantics=("arbitrary",) * n_axes, vmem_limit_bytes=vmem)


def _cond_of_tile(i):
    return jnp.where(i < PROMPT_TILES, 0, 1 + (i - PROMPT_TILES) // SAMPLE_TILES_PER_REQ)


def _pos_block_of_tile(i):
    return jnp.where(i < PROMPT_TILES, 0, 1 + (i - PROMPT_TILES) % SAMPLE_TILES_PER_REQ)


def _mod_spec(layer, k):
    return pl.BlockSpec((None, None, None, 1, D_MODEL), lambda i: (layer, k, _cond_of_tile(i), 0, 0))


def _row_spec(width):
    return pl.BlockSpec((TM, width), lambda i: (i, 0))


def _full_spec(shape):
    nd = len(shape)
    return pl.BlockSpec(shape, lambda *_: (0,) * nd)


def _stream_specs(width):
    return [pl.BlockSpec((TM, width), lambda i: (jnp.minimum(i, PROMPT_TILES - 1), 0)),
            pl.BlockSpec((TM, width), lambda i: (jnp.maximum(i - PROMPT_TILES, 0), 0))]


def _read_rows(refs):
    if len(refs) == 1:
        return refs[0][...]
    return jnp.where(pl.program_id(0) < PROMPT_TILES, refs[0][...], refs[1][...])


def _row_inputs(x):
    if isinstance(x, tuple):
        return _stream_specs(x[0].shape[1]), list(x)
    return [_row_spec(x.shape[1])], [x]


TOK_TILE = 8


def _tok_tile_spec():
    return pl.BlockSpec((TM * TOK_TILE, LANES), lambda i: (i, 0))


def _token_tiles(ref):
    if len(ref.shape) == 3:
        return ref
    return ref.reshape(ref.shape[0] // TOK_TILE, TOK_TILE, LANES)


def _lane_block_rows(j, n_tokens, t0=0):
    return pl.ds(t0 * TOK_TILE + j, n_tokens, stride=TOK_TILE)


def _store_token_tiles(ref, val):
    for j in range(TOK_TILE):
        ref[_lane_block_rows(j, val.shape[0]), :] = val[:, j * LANES:(j + 1) * LANES]


def _load_token_tiles(ref, n_tokens, t0=0):
    return jnp.concatenate([ref[_lane_block_rows(j, n_tokens, t0), :] for j in range(TOK_TILE)], axis=-1)


def _split_bf16(a):
    hi = a.astype(BF16)
    lo = (a - hi.astype(F32)).astype(BF16)
    return hi, lo


def _dot3(a, b, dims=(((1,), (0,)), ((), ()))):
    a_hi, a_lo = _split_bf16(a)
    b_hi, b_lo = _split_bf16(b)
    dg = functools.partial(lax.dot_general, dimension_numbers=dims, preferred_element_type=F32)
    return dg(a_hi, b_hi) + dg(a_lo, b_hi) + dg(a_hi, b_lo)


def _layer_norm(y, g, b):
    mu = jnp.mean(y, axis=-1, keepdims=True)
    d = y - mu
    var = jnp.mean(d * d, axis=-1, keepdims=True)
    return d * lax.rsqrt(var + LN_EPS) * g + b


def _rope_lanes(t, cos, sin, half):
    lane = lax.broadcasted_iota(jnp.int32, t.shape, 1)
    first = (lane & (2 * half - 1)) < half
    partner = jnp.where(first, pltpu.roll(t, LANES - half, 1), pltpu.roll(t, half, 1))
    return t * cos + partner * sin


def _pair_rms(t, gain):
    sq = t * t
    lane = lax.broadcasted_iota(jnp.int32, t.shape, 1)
    lo = lane < HEAD_DIM
    s_lo = jnp.sum(jnp.where(lo, sq, 0.0), axis=-1, keepdims=True)
    s_hi = jnp.sum(jnp.where(lo, 0.0, sq), axis=-1, keepdims=True)
    ms = jnp.where(lo, s_lo, s_hi) * (1.0 / HEAD_DIM)
    return t * lax.rsqrt(ms + NORM_EPS) * gain


def _adaln_kernel(c_ref, w_ref, b_ref, o_ref):
    c = c_ref[...]
    a = c / (1.0 + jnp.exp(-c))
    o_ref[...] = _dot3(a, w_ref[...]) + b_ref[...]


def _adaln(cond, ada_w, ada_b):
    return pl.pallas_call(
        _adaln_kernel,
        out_shape=jax.ShapeDtypeStruct((DEPTH, 6, N_COND, D_MODEL), F32),
        grid=(DEPTH, 6),
        in_specs=[
            pl.BlockSpec((N_COND, D_MODEL), lambda l, k: (0, 0)),
            pl.BlockSpec((None, D_MODEL, D_MODEL), lambda l, k: (l, 0, k)),
            pl.BlockSpec((None, None, 1, D_MODEL), lambda l, k: (l, k, 0, 0)),
        ],
        out_specs=pl.BlockSpec((None, None, N_COND, D_MODEL), lambda l, k: (l, k, 0, 0)),
        compiler_params=_params(2),
        name="adaln",
    )(cond, ada_w, ada_b.reshape(DEPTH, 6, 1, D_MODEL))


QK_WIDTH = (N_HEADS + N_KV_HEADS) * HEAD_DIM
QKV_WIDTH = (N_HEADS + 2 * N_KV_HEADS) * HEAD_DIM
Q_WIDTH = N_HEADS * HEAD_DIM


def _gqa_proj_kernel(*refs, use_norm, n_x):
    x_refs, refs = refs[:n_x], refs[n_x:]
    sh_ref, sc_ref, w_ref, gain_ref, cos_ref, sin_ref, q_ref, kh_ref, vh_ref, k32_ref, v32_ref = refs
    h = (_read_rows(x_refs) * (1.0 + sc_ref[...]) + sh_ref[...]).astype(BF16)
    acc = jnp.dot(h, w_ref[...], preferred_element_type=F32)
    cos = cos_ref[...]
    sin = sin_ref[...]
    k_heads, v_heads = [], []
    for j in range(QK_WIDTH // LANES):
        t = acc[:, j * LANES:(j + 1) * LANES]
        if use_norm:
            t = _pair_rms(t, gain_ref[:, j * LANES:(j + 1) * LANES])
        t = _rope_lanes(t, cos, sin, HEAD_DIM // 2)
        if j < Q_WIDTH // LANES:
            q_ref[:, j * LANES:(j + 1) * LANES] = (t * GQA_Q_SCALE).astype(BF16)
        else:
            for half in range(2):
                th = t[:, half * HEAD_DIM:(half + 1) * HEAD_DIM]
                kh_ref[len(k_heads)] = th.astype(BF16)
                k_heads.append(th)
    for kv_head in range(N_KV_HEADS):
        lo = QK_WIDTH + kv_head * HEAD_DIM
        tv = acc[:, lo:lo + HEAD_DIM]
        vh_ref[kv_head] = tv.astype(BF16)
        v_heads.append(tv)

    @pl.when(pl.program_id(0) < PROMPT_TILES)
    def _():
        for req in range(TM // SEQ):
            for kv_head in range(N_KV_HEADS):
                k32_ref[req, kv_head] = k_heads[kv_head][req * SEQ:(req + 1) * SEQ, :]
                v32_ref[req, kv_head] = v_heads[kv_head][req * SEQ:(req + 1) * SEQ, :]


def _gqa_proj(x, mods5, layer, w_qkv, gain, cos_t, sin_t, use_norm):
    head_spec = pl.BlockSpec((N_KV_HEADS, TM, HEAD_DIM), lambda i: (0, i, 0))
    pos_spec = pl.BlockSpec((TM, LANES), lambda i: (_pos_block_of_tile(i), 0))
    cache_spec = pl.BlockSpec((TM // SEQ, N_KV_HEADS, SEQ, HEAD_DIM),
                              lambda i: (jnp.minimum(i, PROMPT_TILES - 1), 0, 0, 0))
    x_specs, x_args = _row_inputs(x)
    return pl.pallas_call(
        functools.partial(_gqa_proj_kernel, use_norm=use_norm, n_x=len(x_args)),
        out_shape=(
            jax.ShapeDtypeStruct((N_TOK, Q_WIDTH), BF16),
            jax.ShapeDtypeStruct((N_KV_HEADS, N_TOK, HEAD_DIM), BF16),
            jax.ShapeDtypeStruct((N_KV_HEADS, N_TOK, HEAD_DIM), BF16),
            jax.ShapeDtypeStruct((BATCH, N_KV_HEADS, SEQ, HEAD_DIM), F32),
            jax.ShapeDtypeStruct((BATCH, N_KV_HEADS, SEQ, HEAD_DIM), F32),
        ),
        grid=(TILES,),
        in_specs=x_specs + [
            _mod_spec(layer, 0), _mod_spec(layer, 1),
            _full_spec((D_MODEL, QKV_WIDTH)), _full_spec((1, QK_WIDTH)),
            pos_spec, pos_spec,
        ],
        out_specs=(_row_spec(Q_WIDTH), head_spec, head_spec, cache_spec, cache_spec),
        compiler_params=_params(1),
        name="gqa_proj",
    )(*x_args, mods5, mods5, w_qkv, gain, cos_t, sin_t)


def _attend(qs, chunks, sink_col):
    m = den = acc = None
    for k_fn, v_fn, bias, _ in chunks:
        s = lax.dot_general(qs, k_fn(), _NT, preferred_element_type=F32)
        if bias is not None:
            s = s + bias
        mc = jnp.max(s, axis=-1, keepdims=True)
        m_new = mc if m is None else jnp.maximum(m, mc)
        p = jnp.exp2(s - m_new)
        ps = jnp.sum(p, axis=-1, keepdims=True)
        pv = jnp.dot(p.astype(BF16), v_fn(), preferred_element_type=F32)
        if m is None:
            den, acc = ps, pv
        else:
            alpha = jnp.exp2(m - m_new)
            den = den * alpha + ps
            acc = acc * alpha + pv
        m = m_new
    if sink_col is not None:
        den = den + jnp.exp2(sink_col - m)
    return acc / den


def _chunk_list(k_ref, v_ref, length, cast):
    out = []
    width = min(KV_CHUNK, length)
    for c0 in range(0, length, width):
        if cast:
            k_fn = lambda c0=c0: k_ref[c0:c0 + width, :].astype(BF16)
            v_fn = lambda c0=c0: v_ref[c0:c0 + width, :].astype(BF16)
        else:
            k_fn = lambda c0=c0: k_ref[c0:c0 + width, :]
            v_fn = lambda c0=c0: v_ref[c0:c0 + width, :]
        out.append((k_fn, v_fn, None, width))
    return out


def _gqa_attn_kernel(*refs, tq, n_keys, kv_groups, has_ctx, use_sink, window):
    refs = list(refs)
    q_ref, k_ref, v_ref = refs[:3]
    pos = 3
    if has_ctx:
        kc_ref, vc_ref = refs[pos:pos + 2]
        pos += 2
    if use_sink:
        sink_ref = refs[pos]
        pos += 1
    o_ref = refs[pos]
    qi = pl.program_id(2)
    width = GROUP * HEAD_DIM

    for gi in range(kv_groups):
        g = pl.program_id(1) * kv_groups + gi
        qf = q_ref[:, gi * width:(gi + 1) * width].astype(F32)
        qs = jnp.concatenate(
            [qf[:, j * HEAD_DIM:(j + 1) * HEAD_DIM] for j in range(GROUP)], axis=0).astype(BF16)
        kg_ref, vg_ref = k_ref.at[gi], v_ref.at[gi]

        if window:
            span = tq + 2 * WINDOW
            kstart = pl.multiple_of(jnp.clip(qi * tq - WINDOW, 0, n_keys - span), LANES)
            qpos = qi * tq + lax.broadcasted_iota(jnp.int32, (tq, span), 0)
            kpos = kstart + lax.broadcasted_iota(jnp.int32, (tq, span), 1)
            band = jnp.where(jnp.abs(kpos - qpos) <= WINDOW, 0.0, -jnp.inf).astype(F32)
            bias = jnp.concatenate([band] * GROUP, axis=0)
            chunks = [(lambda: kg_ref[pl.ds(kstart, span), :], lambda: vg_ref[pl.ds(kstart, span), :], bias, span)]
        else:
            chunks = _chunk_list(kg_ref, vg_ref, n_keys, cast=False)
        if has_ctx:
            chunks += _chunk_list(kc_ref.at[gi], vc_ref.at[gi], PAST_LEN, cast=True)

        sink_col = None
        if use_sink:
            row = lax.broadcasted_iota(jnp.int32, (GROUP * tq, 1), 0)
            sink_col = jnp.full((GROUP * tq, 1), sink_ref[g * GROUP + GROUP - 1], F32)
            for j in reversed(range(GROUP - 1)):
                sink_col = jnp.where(row < (j + 1) * tq, sink_ref[g * GROUP + j], sink_col)
            sink_col = sink_col * LOG2E

        o = _attend(qs, chunks, sink_col)
        o_ref[:, gi * width:(gi + 1) * width] = jnp.concatenate(
            [o[j * tq:(j + 1) * tq, :] for j in range(GROUP)], axis=-1).astype(BF16)


def _gqa_attention(q, kh, vh, ctx_k, ctx_v, slot, sink, *, latent, window):
    tq = GQA_TQ if latent and not window else ATT_TQ
    if latent:
        n_b, n_keys, n_qt = DEC_BATCH, DEC_SEQ, DEC_SEQ // tq
        row0 = N_PROMPT // tq
        kv_blk0 = N_PROMPT // DEC_SEQ
        kv_groups = 2
    else:
        n_b, n_keys, n_qt = BATCH, SEQ, 1
        row0 = 0
        kv_blk0 = 0
        kv_groups = N_KV_HEADS
    width = kv_groups * GROUP * HEAD_DIM
    q_spec = pl.BlockSpec((tq, width), lambda b, g, t: (row0 + b * n_qt + t, g))
    kv_spec = pl.BlockSpec((kv_groups, n_keys, HEAD_DIM), lambda b, g, t: (g, kv_blk0 + b, 0))
    in_specs = [q_spec, kv_spec, kv_spec]
    args = [q, kh, vh]
    if latent:
        ctx_spec = pl.BlockSpec((None, None, kv_groups, PAST_LEN, HEAD_DIM), lambda b, g, t: (b, slot, g, 0, 0))
        in_specs += [ctx_spec, ctx_spec]
        args += [ctx_k, ctx_v]
    use_sink = sink is not None
    if use_sink:
        in_specs.append(pl.BlockSpec(memory_space=pltpu.SMEM))
        args.append(sink)
    return pl.pallas_call(
        functools.partial(_gqa_attn_kernel, tq=tq, n_keys=n_keys, kv_groups=kv_groups, has_ctx=latent,
                          use_sink=use_sink, window=window),
        out_shape=jax.ShapeDtypeStruct((n_b * n_keys, Q_WIDTH), BF16),
        grid=(n_b, N_KV_HEADS // kv_groups, n_qt),
        in_specs=in_specs,
        out_specs=pl.BlockSpec((tq, width), lambda b, g, t: (b * n_qt + t, g)),
        compiler_params=_params(3),
        name="gqa_attn_latent" if latent else "gqa_attn_context",
    )(*args)


MLA_DOWN_EXT = Q_RANK + KV_RANK + LANES
MLA_HEAD_LANES = LANES
MLA_ROPE_LO = MLA_NOPE_DIM


def _mla_proj_kernel(x_ref, sh_ref, sc_ref, wd_ref, qg_ref, wuq_ref, kvg_ref, cos_ref, sin_ref,
                     q_ref, ckv_ref, kpe_ref):
    h = (x_ref[...] * (1.0 + sc_ref[...]) + sh_ref[...]).astype(BF16)
    acc = jnp.dot(h, wd_ref[...], preferred_element_type=F32)
    cos = cos_ref[...]
    sin = sin_ref[...]
    qa = acc[:, :Q_RANK]
    qn = qa * lax.rsqrt(jnp.mean(qa * qa, axis=-1, keepdims=True) + NORM_EPS) * qg_ref[...]
    q = jnp.dot(qn.astype(BF16), wuq_ref[...], preferred_element_type=F32)
    for hd in range(MLA_HEADS):
        t = q[:, hd * LANES:(hd + 1) * LANES]
        q_ref[:, hd * LANES:(hd + 1) * LANES] = (_rope_lanes(t, cos, sin, MLA_ROPE_DIM // 2) * MLA_Q_SCALE).astype(BF16)
    ckv = acc[:, Q_RANK:Q_RANK + KV_RANK]
    ckv_ref[...] = ckv * lax.rsqrt(jnp.mean(ckv * ckv, axis=-1, keepdims=True) + NORM_EPS) * kvg_ref[...]
    kpe_ref[...] = _rope_lanes(acc[:, Q_RANK + KV_RANK:], cos, sin, MLA_ROPE_DIM // 2)


def _mla_proj(x, mods5, layer, wd_ext, q_gain, wuq_pad, kv_gain, cos_t, sin_t):
    pos_spec = pl.BlockSpec((TM, LANES), lambda i: (_pos_block_of_tile(i), 0))
    return pl.pallas_call(
        _mla_proj_kernel,
        out_shape=(
            jax.ShapeDtypeStruct((N_TOK, MLA_HEADS * LANES), BF16),
            jax.ShapeDtypeStruct((N_TOK, KV_RANK), F32),
            jax.ShapeDtypeStruct((N_TOK, LANES), F32),
        ),
        grid=(TILES,),
        in_specs=[
            _row_spec(D_MODEL), _mod_spec(layer, 0), _mod_spec(layer, 1),
            _full_spec((D_MODEL, MLA_DOWN_EXT)), _full_spec((1, Q_RANK)),
            _full_spec((Q_RANK, MLA_HEADS * LANES)), _full_spec((1, KV_RANK)),
            pos_spec, pos_spec,
        ],
        out_specs=(_row_spec(MLA_HEADS * LANES), _row_spec(KV_RANK), _row_spec(LANES)),
        compiler_params=_params(1),
        name="mla_proj",
    )(x, mods5, mods5, wd_ext, q_gain, wuq_pad, kv_gain, cos_t, sin_t)


def _mla_kv_kernel(c_ref, kpe_ref, wk_ref, wv_ref, k_ref, v_ref):
    c = c_ref[...].astype(BF16)
    kk = jnp.dot(c, wk_ref[...], preferred_element_type=F32)
    vv = jnp.dot(c, wv_ref[...], preferred_element_type=F32)
    kpe = kpe_ref[...]
    for hd in range(MLA_HEADS):
        k_ref[hd] = (kk[:, hd * LANES:(hd + 1) * LANES] + kpe).astype(BF16)
        v_ref[hd] = vv[:, hd * MLA_V_DIM:(hd + 1) * MLA_V_DIM].astype(BF16)


def _mla_kv(c_all, kpe_all, wk_pad, wv):
    n_rows = c_all.shape[0]
    return pl.pallas_call(
        _mla_kv_kernel,
        out_shape=(
            jax.ShapeDtypeStruct((MLA_HEADS, n_rows, LANES), BF16),
            jax.ShapeDtypeStruct((MLA_HEADS, n_rows, MLA_V_DIM), BF16),
        ),
        grid=(n_rows // TM,),
        in_specs=[
            _row_spec(KV_RANK), _row_spec(LANES),
            _full_spec((KV_RANK, MLA_HEADS * LANES)), _full_spec((KV_RANK, MLA_HEADS * MLA_V_DIM)),
        ],
        out_specs=(
            pl.BlockSpec((MLA_HEADS, TM, LANES), lambda i: (0, i, 0)),
            pl.BlockSpec((MLA_HEADS, TM, MLA_V_DIM), lambda i: (0, i, 0)),
        ),
        compiler_params=_params(1),
        name="mla_kv",
    )(c_all, kpe_all, wk_pad, wv)


def _mla_attn_kernel(*refs, tq, n_keys, heads, has_ctx):
    refs = list(refs)
    q_ref, k_ref, v_ref = refs[:3]
    pos = 3
    if has_ctx:
        kc_ref, vc_ref = refs[pos:pos + 2]
        pos += 2
    o_ref = refs[pos]
    outs = []
    for j in range(heads):
        qs = q_ref[:, j * LANES:(j + 1) * LANES]
        chunks = _chunk_list(k_ref.at[j], v_ref.at[j], n_keys, cast=False)
        if has_ctx:
            chunks += _chunk_list(kc_ref.at[j], vc_ref.at[j], PAST_LEN, cast=False)
        outs.append(_attend(qs, chunks, None))
    o_ref[...] = jnp.concatenate(outs, axis=-1).astype(BF16)


def _mla_attention(q, k_all, v_all, *, latent):
    if latent:
        tq = MLA_TQ
        n_b, n_keys, n_qt = DEC_BATCH, DEC_SEQ, DEC_SEQ // tq
        row0 = N_PROMPT // tq
        kv_blk0 = N_PROMPT // DEC_SEQ
        heads = 4
    else:
        tq = ATT_TQ
        n_b, n_keys, n_qt = BATCH, SEQ, 1
        row0 = 0
        kv_blk0 = 0
        heads = 8
    q_spec = pl.BlockSpec((tq, heads * LANES), lambda b, hp, t: (row0 + b * n_qt + t, hp))
    k_spec = pl.BlockSpec((heads, n_keys, LANES), lambda b, hp, t: (hp, kv_blk0 + b, 0))
    v_spec = pl.BlockSpec((heads, n_keys, MLA_V_DIM), lambda b, hp, t: (hp, kv_blk0 + b, 0))
    in_specs = [q_spec, k_spec, v_spec]
    args = [q, k_all, v_all]
    if latent:
        ctx0 = N_TOK // PAST_LEN
        in_specs += [
            pl.BlockSpec((heads, PAST_LEN, LANES), lambda b, hp, t: (hp, ctx0 + b, 0)),
            pl.BlockSpec((heads, PAST_LEN, MLA_V_DIM), lambda b, hp, t: (hp, ctx0 + b, 0)),
        ]
        args += [k_all, v_all]
    return pl.pallas_call(
        functools.partial(_mla_attn_kernel, tq=tq, n_keys=n_keys, heads=heads, has_ctx=latent),
        out_shape=jax.ShapeDtypeStruct((n_b * n_keys, MLA_HEADS * MLA_V_DIM), BF16),
        grid=(n_b, MLA_HEADS // heads, n_qt),
        in_specs=in_specs,
        out_specs=pl.BlockSpec((tq, heads * MLA_V_DIM), lambda b, hp, t: (b * n_qt + t, hp)),
        compiler_params=_params(3),
        name="mla_attn_latent" if latent else "mla_attn_context",
    )(*args)


def _post_attn_kernel(*refs, n_x):
    (op_ref, os_ref), x_refs, refs = refs[:2], refs[2:2 + n_x], refs[2 + n_x:]
    wo_ref, g1_ref, sh2_ref, sc2_ref, lng_ref, lnb_ref, wr_ref, x1_ref, h2_ref, aff_ref, fz_ref = refs
    proj = jnp.dot(_read_rows((op_ref, os_ref)), wo_ref[...], preferred_element_type=F32)
    x1 = _layer_norm(ALPHA * _read_rows(x_refs) + g1_ref[...] * proj, lng_ref[...], lnb_ref[...])
    x1_ref[...] = x1
    h2 = x1 * (1.0 + sc2_ref[...]) + sh2_ref[...]
    _store_token_tiles(h2_ref, h2)
    logits_t = _dot3(wr_ref[...], h2, _NT)
    e = jnp.exp(logits_t - jnp.max(logits_t, axis=0, keepdims=True))
    aff_ref[...] = e / jnp.sum(e, axis=0, keepdims=True)
    fz_ref[...] = jnp.zeros_like(fz_ref)


def _post_attn(o_p, o_s, x, w_o, mods5, layer, ln_g, ln_b, w_router_t):
    x_specs, x_args = _row_inputs(x)
    return pl.pallas_call(
        functools.partial(_post_attn_kernel, n_x=len(x_args)),
        out_shape=(
            jax.ShapeDtypeStruct((N_TOK, D_MODEL), F32),
            jax.ShapeDtypeStruct((N_TOK * TOK_TILE, LANES), F32),
            jax.ShapeDtypeStruct((N_EXPERTS, N_TOK), F32),
            jax.ShapeDtypeStruct((N_TOK * TOK_TILE, LANES), F32),
        ),
        grid=(TILES,),
        in_specs=_stream_specs(D_MODEL) + x_specs + [
            _full_spec((D_MODEL, D_MODEL)),
            _mod_spec(layer, 2), _mod_spec(layer, 3), _mod_spec(layer, 4),
            _full_spec((1, D_MODEL)), _full_spec((1, D_MODEL)), _full_spec((N_EXPERTS, D_MODEL)),
        ],
        out_specs=(_row_spec(D_MODEL), _tok_tile_spec(), pl.BlockSpec((N_EXPERTS, TM), lambda i: (0, i)),
                   _tok_tile_spec()),
        compiler_params=_params(1),
        name="post_attn",
    )(o_p, o_s, *x_args, w_o, mods5, mods5, mods5, ln_g, ln_b, w_router_t)


FFN_ROWS = 512
FFN_STEPS = EXPERT_FF // FF_TILE
ROW_CHUNKS = CAP // FFN_ROWS
GATHER_PER_CHUNK = CAP // (FFN_STEPS * ROW_CHUNKS)


N_DMA_THREADS = 2


def _token_copy(src, src_tok, dst, dst_tok, sem):
    return pltpu.make_async_copy(_token_tiles(src).at[src_tok], _token_tiles(dst).at[dst_tok], sem)


def _wait_buffer(buf_ref, sem):
    pltpu.make_async_copy(buf_ref, buf_ref, sem).wait()


def _ffn_kernel(idx_ref, h_hbm, f_in_hbm, wg_ref, wu_ref, wd_ref, gate_ref, f_hbm,
                xbuf, acc, fbuf, gsem, rsem, wsem):
    del f_in_hbm
    e = pl.program_id(0)
    f = pl.program_id(1)
    buf = e % 2
    nxt = jnp.minimum(e + 1, N_EXPERTS - 1)

    def gather(expert, s, b):
        return _token_copy(h_hbm, idx_ref[expert * CAP + s], xbuf.at[b], s, gsem.at[b])

    @pl.when((e == 0) & (f == 0))
    def _():
        def issue(s, carry):
            gather(0, s, 0).start()
            return carry
        lax.fori_loop(0, CAP, issue, 0, unroll=8)

    @pl.when(f == 0)
    def _():
        _wait_buffer(xbuf.at[buf], gsem.at[buf])

    def rows_copy(r_lo, n_rows, to_hbm):
        def body(i, carry):
            for thread in range(N_DMA_THREADS):
                r = r_lo + i * N_DMA_THREADS + thread
                tok = idx_ref[e * CAP + r]
                if to_hbm:
                    _token_copy(fbuf, r, f_hbm, tok, wsem.at[0]).start(priority=thread)
                else:
                    _token_copy(f_hbm, tok, fbuf, r, rsem.at[0]).start(priority=thread)
            return carry
        lax.fori_loop(0, n_rows // N_DMA_THREADS, body, 0, unroll=8)

    wg = wg_ref[...].astype(BF16)
    wu = wu_ref[...].astype(BF16)
    wd = wd_ref[...].astype(BF16)
    for c in range(ROW_CHUNKS):
        r0 = c * FFN_ROWS
        x = _load_token_tiles(xbuf.at[buf], FFN_ROWS, r0).astype(BF16)
        a = jnp.dot(x, wg, preferred_element_type=F32)
        u = jnp.dot(x, wu, preferred_element_type=F32)
        hid = (a / (1.0 + jnp.exp(-a)) * u).astype(BF16)
        part = jnp.dot(hid, wd, preferred_element_type=F32)
        for j in range(GATHER_PER_CHUNK):
            s = (f * ROW_CHUNKS + c) * GATHER_PER_CHUNK + j
            gather(nxt, s, 1 - buf).start(priority=j % N_DMA_THREADS)
            _token_copy(f_hbm, idx_ref[e * CAP + s], fbuf, s, rsem.at[0]).start(priority=(j + 1) % N_DMA_THREADS)
        acc[r0:r0 + FFN_ROWS, :] = jnp.where(f == 0, 0.0, acc[r0:r0 + FFN_ROWS, :]) + part

    @pl.when(f == FFN_STEPS - 1)
    def _():
        _wait_buffer(fbuf, rsem.at[0])
        for k in range(ROW_CHUNKS):
            r0 = k * FFN_ROWS
            g = gate_ref[r0:r0 + FFN_ROWS, :]
            for j in range(TOK_TILE):
                rows = _lane_block_rows(j, FFN_ROWS, r0)
                fbuf[rows, :] = fbuf[rows, :] + acc[r0:r0 + FFN_ROWS, j * LANES:(j + 1) * LANES] * g
            rows_copy(r0, FFN_ROWS, True)
        _wait_buffer(fbuf, wsem.at[0])

    @pl.when((e == N_EXPERTS - 1) & (f == FFN_STEPS - 1))
    def _():
        _wait_buffer(xbuf.at[1 - buf], gsem.at[1 - buf])


def _expert_ffn(idx, h2, f_zero, gate, w_gate, w_up, w_down, layer):
    grid_spec = pltpu.PrefetchScalarGridSpec(
        num_scalar_prefetch=1,
        grid=(N_EXPERTS, FFN_STEPS),
        in_specs=[
            pl.BlockSpec(memory_space=pl.ANY),
            pl.BlockSpec(memory_space=pl.ANY),
            pl.BlockSpec((None, None, D_MODEL, FF_TILE), lambda e, f, idx: (layer, e, 0, f)),
            pl.BlockSpec((None, None, D_MODEL, FF_TILE), lambda e, f, idx: (layer, e, 0, f)),
            pl.BlockSpec((None, None, FF_TILE, D_MODEL), lambda e, f, idx: (layer, e, f, 0)),
            pl.BlockSpec((None, CAP, 1), lambda e, f, idx: (e, 0, 0)),
        ],
        out_specs=pl.BlockSpec(memory_space=pl.ANY),
        scratch_shapes=[
            pltpu.VMEM((2, CAP * TOK_TILE, LANES), F32),
            pltpu.VMEM((CAP, D_MODEL), F32),
            pltpu.VMEM((CAP * TOK_TILE, LANES), F32),
            pltpu.SemaphoreType.DMA((2,)), pltpu.SemaphoreType.DMA((1,)), pltpu.SemaphoreType.DMA((1,)),
        ],
    )
    tiles = (N_TOK, TOK_TILE, LANES)
    f = pl.pallas_call(
        _ffn_kernel,
        out_shape=jax.ShapeDtypeStruct(tiles, F32),
        grid_spec=grid_spec,
        input_output_aliases={2: 0},
        compiler_params=_params(2),
        name="expert_ffn",
    )(idx, h2.reshape(tiles), f_zero.reshape(tiles), w_gate, w_up, w_down, gate)
    return f.reshape(N_TOK * TOK_TILE, LANES)


def _post_moe_kernel(x_ref, f_ref, g2_ref, lng_ref, lnb_ref, *o_refs):
    f = _load_token_tiles(f_ref, TM)
    y = _layer_norm(ALPHA * x_ref[...] + g2_ref[...] * f, lng_ref[...], lnb_ref[...])
    if len(o_refs) == 1:
        o_refs[0][...] = y
    else:
        @pl.when(pl.program_id(0) < PROMPT_TILES)
        def _():
            o_refs[0][...] = y

        @pl.when(pl.program_id(0) >= PROMPT_TILES)
        def _():
            o_refs[1][...] = y


def _post_moe(x1, f, mods5, layer, ln_g, ln_b, split_streams):
    if split_streams:
        out_shape = (jax.ShapeDtypeStruct((N_PROMPT, D_MODEL), F32), jax.ShapeDtypeStruct((N_SAMPLE, D_MODEL), F32))
        out_specs = tuple(_stream_specs(D_MODEL))
    else:
        out_shape = jax.ShapeDtypeStruct((N_TOK, D_MODEL), F32)
        out_specs = _row_spec(D_MODEL)
    return pl.pallas_call(
        _post_moe_kernel,
        out_shape=out_shape,
        grid=(TILES,),
        in_specs=[_row_spec(D_MODEL), _tok_tile_spec(), _mod_spec(layer, 5),
                  _full_spec((1, D_MODEL)), _full_spec((1, D_MODEL))],
        out_specs=out_specs,
        compiler_params=_params(1),
        name="post_moe",
    )(x1, f, mods5, ln_g, ln_b)


def _axial_tables(dim):
    n_rows = DEC_SEQ // GRID_W
    rows = jnp.repeat(jnp.arange(n_rows, dtype=F32), GRID_W)
    cols = jnp.tile(jnp.arange(GRID_W, dtype=F32), n_rows)
    n_freq = dim // 4
    inv_freq = ROPE_THETA ** (-jnp.arange(n_freq, dtype=F32) / n_freq)
    ang = jnp.concatenate([rows[:, None] * inv_freq, cols[:, None] * inv_freq], -1)
    cos, sin = jnp.cos(ang), jnp.sin(ang)
    return jnp.concatenate([cos, cos], -1), jnp.concatenate([-sin, sin], -1)


def _with_identity_rows(cos_l, sin_l):
    return (jnp.concatenate([jnp.ones((TM, LANES), F32), cos_l], 0),
            jnp.concatenate([jnp.zeros((TM, LANES), F32), sin_l], 0))


def _gqa_rope_tables():
    cos, sin = _axial_tables(HEAD_DIM)
    return _with_identity_rows(jnp.tile(cos, (1, 2)), jnp.tile(sin, (1, 2)))


def _mla_rope_tables():
    cos, sin = _axial_tables(MLA_ROPE_DIM)
    pad_hi = LANES - MLA_ROPE_LO - MLA_ROPE_DIM
    cos_l = jnp.concatenate([jnp.ones((DEC_SEQ, MLA_ROPE_LO), F32), cos, jnp.ones((DEC_SEQ, pad_hi), F32)], 1)
    sin_l = jnp.pad(sin, ((0, 0), (MLA_ROPE_LO, pad_hi)))
    return _with_identity_rows(cos_l, sin_l)


def _moe(h2, f_zero, aff_t, w_gate, w_up, w_down, layer):
    gate_p, idx_p = lax.top_k(aff_t[:, :N_PROMPT], CAP_P)
    gate_s, idx_s = lax.top_k(aff_t[:, N_PROMPT:], CAP_S)
    idx = jnp.concatenate([idx_p, idx_s + N_PROMPT], axis=1).reshape(-1)
    gate = jnp.concatenate([gate_p, gate_s], axis=1)[..., None]
    return _expert_ffn(idx, h2, f_zero, gate, w_gate, w_up, w_down, layer)


def kernel(x_prompt, x_sample, cache_gqa_k, cache_gqa_v, cache_mla_ckv, cache_mla_kpe, c, c_ctx,
           ada_w, ada_b, ln_g, ln_b, a_w_qkv, a_q_norm, a_k_norm, a_w_o, b_w_qkv, b_sink, b_w_o,
           mla_w_down, mla_q_norm, mla_w_uq, mla_kv_norm, mla_w_ukv, mla_w_o,
           moe_w_router, moe_w_gate, moe_w_up, moe_w_down):
    x = (x_prompt.reshape(N_PROMPT, D_MODEL), x_sample.reshape(N_SAMPLE, D_MODEL))
    cond = jnp.concatenate([c_ctx[None], c, jnp.zeros((N_COND - 1 - DEC_BATCH, D_MODEL), F32)], 0)
    mods5 = _adaln(cond, ada_w, ada_b).reshape(DEPTH, 6, N_COND, 1, D_MODEL)
    gqa_cos, gqa_sin = _gqa_rope_tables()
    mla_cos, mla_sin = _mla_rope_tables()

    new_k, new_v, new_ckv, new_kpe = [], [], [], []
    gqa_slot = 0
    for i in range(DEPTH):
        kind, j = i % 3, i // 3
        if kind in (0, 1):
            if kind == 0:
                w_qkv, w_o, sink = a_w_qkv[j], a_w_o[j], None
                gain = jnp.concatenate([jnp.tile(a_q_norm[j], N_HEADS), jnp.tile(a_k_norm[j], N_KV_HEADS)])[None]
            else:
                w_qkv, w_o, sink = b_w_qkv[j], b_w_o[j], b_sink[j]
                gain = jnp.ones((1, QK_WIDTH), F32)
            q, kh, vh, k32, v32 = _gqa_proj(x, mods5, i, w_qkv.astype(BF16), gain, gqa_cos, gqa_sin,
                                            use_norm=(kind == 0))
            o_p = _gqa_attention(q, kh, vh, None, None, gqa_slot, sink, latent=False, window=False)
            o_s = _gqa_attention(q, kh, vh, cache_gqa_k, cache_gqa_v, gqa_slot, sink,
                                 latent=True, window=(kind == 1))
            new_k.append(k32)
            new_v.append(v32)
            gqa_slot += 1
        else:
            w_down = mla_w_down[j]
            rope_lo = Q_RANK + KV_RANK
            wd_ext = jnp.concatenate([
                w_down[:, :rope_lo], jnp.zeros((D_MODEL, MLA_ROPE_LO), F32), w_down[:, rope_lo:],
                jnp.zeros((D_MODEL, LANES - MLA_ROPE_LO - MLA_ROPE_DIM), F32)], 1).astype(BF16)
            wuq_pad = jnp.pad(mla_w_uq[j].reshape(Q_RANK, MLA_HEADS, MLA_QK_DIM),
                              ((0, 0), (0, 0), (0, LANES - MLA_QK_DIM))).reshape(Q_RANK, MLA_HEADS * LANES)
            w_ukv = mla_w_ukv[j].reshape(KV_RANK, MLA_HEADS, MLA_NOPE_DIM + MLA_V_DIM)
            wk_pad = jnp.pad(w_ukv[:, :, :MLA_NOPE_DIM], ((0, 0), (0, 0), (0, LANES - MLA_NOPE_DIM)))
            wk_pad = wk_pad.reshape(KV_RANK, MLA_HEADS * LANES).astype(BF16)
            wv = w_ukv[:, :, MLA_NOPE_DIM:].reshape(KV_RANK, MLA_HEADS * MLA_V_DIM).astype(BF16)
            q, ckv, kpe = _mla_proj(x, mods5, i, wd_ext, mla_q_norm[j][None], wuq_pad.astype(BF16),
                                    mla_kv_norm[j][None], mla_cos, mla_sin)
            c_all = jnp.concatenate([ckv, cache_mla_ckv[:, j].reshape(DEC_BATCH * PAST_LEN, KV_RANK)], 0)
            kpe_ctx = jnp.pad(cache_mla_kpe[:, j].reshape(DEC_BATCH * PAST_LEN, MLA_ROPE_DIM),
                              ((0, 0), (MLA_ROPE_LO, LANES - MLA_ROPE_LO - MLA_ROPE_DIM)))
            kpe_all = jnp.concatenate([kpe, kpe_ctx], 0)
            k_all, v_all = _mla_kv(c_all, kpe_all, wk_pad, wv)
            o_p = _mla_attention(q, k_all, v_all, latent=False)
            o_s = _mla_attention(q, k_all, v_all, latent=True)
            w_o = mla_w_o[j]
            new_ckv.append(ckv[:N_PROMPT].reshape(BATCH, SEQ, KV_RANK))
            new_kpe.append(kpe[:N_PROMPT, MLA_ROPE_LO:MLA_ROPE_LO + MLA_ROPE_DIM].reshape(BATCH, SEQ, MLA_ROPE_DIM))
        x1, h2, aff_t, f_zero = _post_attn(o_p, o_s, x, w_o.astype(BF16), mods5, i, ln_g[i, 0][None],
                                           ln_b[i, 0][None], moe_w_router[i].T)
        f = _moe(h2, f_zero, aff_t, moe_w_gate, moe_w_up, moe_w_down, i)
        x = _post_moe(x1, f, mods5, i, ln_g[i, 1][None], ln_b[i, 1][None], split_streams=(i == DEPTH - 1))

    y_prompt = x[0].reshape(BATCH, SEQ, D_MODEL)
    y_sample = x[1].reshape(DEC_BATCH, DEC_SEQ, D_MODEL)
    return (y_prompt, y_sample, jnp.stack(new_k, 1), jnp.stack(new_v, 1),
            jnp.stack(new_ckv, 1), jnp.stack(new_kpe, 1))
```

```python
import functools
import math

import jax
import jax.numpy as jnp
from jax import lax
from jax.experimental import pallas as pl
from jax.experimental.pallas import tpu as pltpu

F32 = jnp.float32
BF16 = jnp.bfloat16

D_MODEL = 1024
BATCH = 16
SEQ = 256
DEPTH = 4
DEC_BATCH = 4
DEC_SEQ = 2048
PAST_LEN = 512
GRID_W = 64
N_HEADS = 16
N_KV_HEADS = 4
GROUP = N_HEADS // N_KV_HEADS
HEAD_DIM = 64
WINDOW = 128
MLA_HEADS = 16
Q_RANK = 256
KV_RANK = 128
MLA_NOPE_DIM = 64
MLA_ROPE_DIM = 32
MLA_V_DIM = 64
MLA_QK_DIM = MLA_NOPE_DIM + MLA_ROPE_DIM
MLA_SCALE = MLA_QK_DIM ** -0.5
GQA_SCALE = HEAD_DIM ** -0.5
N_EXPERTS = 16
EXPERT_FF = 2048
CAPACITY_FACTOR = 2
ROPE_THETA = 10000.0
NORM_EPS = 1e-6
LN_EPS = 1e-5
ALPHA = (2 * DEPTH) ** 0.25
LOG2E = math.log2(math.e)
GQA_Q_SCALE = GQA_SCALE * LOG2E
MLA_Q_SCALE = MLA_SCALE * LOG2E

LANES = 128
N_PROMPT = BATCH * SEQ
N_SAMPLE = DEC_BATCH * DEC_SEQ
N_TOK = N_PROMPT + N_SAMPLE
TM = 512
ATT_TQ = SEQ
PROMPT_TILES = N_PROMPT // TM
TILES = N_TOK // TM
SAMPLE_TILES_PER_REQ = DEC_SEQ // TM
N_COND = 8
CAP_P = CAPACITY_FACTOR * N_PROMPT // N_EXPERTS
CAP_S = CAPACITY_FACTOR * N_SAMPLE // N_EXPERTS
CAP = CAP_P + CAP_S
FF_TILE = 512
KV_CHUNK = 1024
GQA_TQ = 512
MLA_TQ = 1024

_NT = (((1,), (1,)), ((), ()))
_VMEM_LIMIT = 56 * 1024 * 1024


def _params(n_axes, vmem=_VMEM_LIMIT):
    return pltpu.CompilerParams(dimension_semantics=("arbitrary",) * n_axes, vmem_limit_bytes=vmem)


def _cond_of_tile(i):
    return jnp.where(i < PROMPT_TILES, 0, 1 + (i - PROMPT_TILES) // SAMPLE_TILES_PER_REQ)


def _pos_block_of_tile(i):
    return jnp.where(i < PROMPT_TILES, 0, 1 + (i - PROMPT_TILES) % SAMPLE_TILES_PER_REQ)


def _mod_spec(layer, k):
    return pl.BlockSpec((None, None, None, 1, D_MODEL), lambda i: (layer, k, _cond_of_tile(i), 0, 0))


def _row_spec(width):
    return pl.BlockSpec((TM, width), lambda i: (i, 0))


def _full_spec(shape):
    nd = len(shape)
    return pl.BlockSpec(shape, lambda *_: (0,) * nd)


def _stream_specs(width):
    return [pl.BlockSpec((TM, width), lambda i: (jnp.minimum(i, PROMPT_TILES - 1), 0)),
            pl.BlockSpec((TM, width), lambda i: (jnp.maximum(i - PROMPT_TILES, 0), 0))]


def _read_rows(refs):
    if len(refs) == 1:
        return refs[0][...]
    return jnp.where(pl.program_id(0) < PROMPT_TILES, refs[0][...], refs[1][...])


def _row_inputs(x):
    if isinstance(x, tuple):
        return _stream_specs(x[0].shape[1]), list(x)
    return [_row_spec(x.shape[1])], [x]


TOK_TILE = 8


def _tok_tile_spec():
    return pl.BlockSpec((TM * TOK_TILE, LANES), lambda i: (i, 0))


def _token_tiles(ref):
    if len(ref.shape) == 3:
        return ref
    return ref.reshape(ref.shape[0] // TOK_TILE, TOK_TILE, LANES)


def _lane_block_rows(j, n_tokens, t0=0):
    return pl.ds(t0 * TOK_TILE + j, n_tokens, stride=TOK_TILE)


def _store_token_tiles(ref, val):
    for j in range(TOK_TILE):
        ref[_lane_block_rows(j, val.shape[0]), :] = val[:, j * LANES:(j + 1) * LANES]


def _load_token_tiles(ref, n_tokens, t0=0):
    return jnp.concatenate([ref[_lane_block_rows(j, n_tokens, t0), :] for j in range(TOK_TILE)], axis=-1)


def _split_bf16(a):
    hi = a.astype(BF16)
    lo = (a - hi.astype(F32)).astype(BF16)
    return hi, lo


def _dot3(a, b, dims=(((1,), (0,)), ((), ()))):
    a_hi, a_lo = _split_bf16(a)
    b_hi, b_lo = _split_bf16(b)
    dg = functools.partial(lax.dot_general, dimension_numbers=dims, preferred_element_type=F32)
    return dg(a_hi, b_hi) + dg(a_lo, b_hi) + dg(a_hi, b_lo)


def _layer_norm(y, g, b):
    mu = jnp.mean(y, axis=-1, keepdims=True)
    d = y - mu
    var = jnp.mean(d * d, axis=-1, keepdims=True)
    return d * lax.rsqrt(var + LN_EPS) * g + b


def _rope_lanes(t, cos, sin, half):
    lane = lax.broadcasted_iota(jnp.int32, t.shape, 1)
    first = (lane & (2 * half - 1)) < half
    partner = jnp.where(first, pltpu.roll(t, LANES - half, 1), pltpu.roll(t, half, 1))
    return t * cos + partner * sin


def _pair_rms(t, gain):
    sq = t * t
    lane = lax.broadcasted_iota(jnp.int32, t.shape, 1)
    lo = lane < HEAD_DIM
    s_lo = jnp.sum(jnp.where(lo, sq, 0.0), axis=-1, keepdims=True)
    s_hi = jnp.sum(jnp.where(lo, 0.0, sq), axis=-1, keepdims=True)
    ms = jnp.where(lo, s_lo, s_hi) * (1.0 / HEAD_DIM)
    return t * lax.rsqrt(ms + NORM_EPS) * gain


def _adaln_kernel(c_ref, w_ref, b_ref, o_ref):
    c = c_ref[...]
    a = c / (1.0 + jnp.exp(-c))
    o_ref[...] = _dot3(a, w_ref[...]) + b_ref[...]


def _adaln(cond, ada_w, ada_b):
    return pl.pallas_call(
        _adaln_kernel,
        out_shape=jax.ShapeDtypeStruct((DEPTH, 6, N_COND, D_MODEL), F32),
        grid=(DEPTH, 6),
        in_specs=[
            pl.BlockSpec((N_COND, D_MODEL), lambda l, k: (0, 0)),
            pl.BlockSpec((None, D_MODEL, D_MODEL), lambda l, k: (l, 0, k)),
            pl.BlockSpec((None, None, 1, D_MODEL), lambda l, k: (l, k, 0, 0)),
        ],
        out_specs=pl.BlockSpec((None, None, N_COND, D_MODEL), lambda l, k: (l, k, 0, 0)),
        compiler_params=_params(2),
        name="adaln",
    )(cond, ada_w, ada_b.reshape(DEPTH, 6, 1, D_MODEL))


QK_WIDTH = (N_HEADS + N_KV_HEADS) * HEAD_DIM
QKV_WIDTH = (N_HEADS + 2 * N_KV_HEADS) * HEAD_DIM
Q_WIDTH = N_HEADS * HEAD_DIM


def _gqa_proj_kernel(*refs, use_norm, n_x):
    x_refs, refs = refs[:n_x], refs[n_x:]
    sh_ref, sc_ref, w_ref, gain_ref, cos_ref, sin_ref, q_ref, kh_ref, vh_ref, k32_ref, v32_ref = refs
    h = (_read_rows(x_refs) * (1.0 + sc_ref[...]) + sh_ref[...]).astype(BF16)
    acc = jnp.dot(h, w_ref[...], preferred_element_type=F32)
    cos = cos_ref[...]
    sin = sin_ref[...]
    k_heads, v_heads = [], []
    for j in range(QK_WIDTH // LANES):
        t = acc[:, j * LANES:(j + 1) * LANES]
        if use_norm:
            t = _pair_rms(t, gain_ref[:, j * LANES:(j + 1) * LANES])
        t = _rope_lanes(t, cos, sin, HEAD_DIM // 2)
        if j < Q_WIDTH // LANES:
            q_ref[:, j * LANES:(j + 1) * LANES] = (t * GQA_Q_SCALE).astype(BF16)
        else:
            for half in range(2):
                th = t[:, half * HEAD_DIM:(half + 1) * HEAD_DIM]
                kh_ref[len(k_heads)] = th.astype(BF16)
                k_heads.append(th)
    for kv_head in range(N_KV_HEADS):
        lo = QK_WIDTH + kv_head * HEAD_DIM
        tv = acc[:, lo:lo + HEAD_DIM]
        vh_ref[kv_head] = tv.astype(BF16)
        v_heads.append(tv)

    @pl.when(pl.program_id(0) < PROMPT_TILES)
    def _():
        for req in range(TM // SEQ):
            for kv_head in range(N_KV_HEADS):
                k32_ref[req, kv_head] = k_heads[kv_head][req * SEQ:(req + 1) * SEQ, :]
                v32_ref[req, kv_head] = v_heads[kv_head][req * SEQ:(req + 1) * SEQ, :]


def _gqa_proj(x, mods5, layer, w_qkv, gain, cos_t, sin_t, use_norm):
    head_spec = pl.BlockSpec((N_KV_HEADS, TM, HEAD_DIM), lambda i: (0, i, 0))
    pos_spec = pl.BlockSpec((TM, LANES), lambda i: (_pos_block_of_tile(i), 0))
    cache_spec = pl.BlockSpec((TM // SEQ, N_KV_HEADS, SEQ, HEAD_DIM),
                              lambda i: (jnp.minimum(i, PROMPT_TILES - 1), 0, 0, 0))
    x_specs, x_args = _row_inputs(x)
    return pl.pallas_call(
        functools.partial(_gqa_proj_kernel, use_norm=use_norm, n_x=len(x_args)),
        out_shape=(
            jax.ShapeDtypeStruct((N_TOK, Q_WIDTH), BF16),
            jax.ShapeDtypeStruct((N_KV_HEADS, N_TOK, HEAD_DIM), BF16),
            jax.ShapeDtypeStruct((N_KV_HEADS, N_TOK, HEAD_DIM), BF16),
            jax.ShapeDtypeStruct((BATCH, N_KV_HEADS, SEQ, HEAD_DIM), F32),
            jax.ShapeDtypeStruct((BATCH, N_KV_HEADS, SEQ, HEAD_DIM), F32),
        ),
        grid=(TILES,),
        in_specs=x_specs + [
            _mod_spec(layer, 0), _mod_spec(layer, 1),
            _full_spec((D_MODEL, QKV_WIDTH)), _full_spec((1, QK_WIDTH)),
            pos_spec, pos_spec,
        ],
        out_specs=(_row_spec(Q_WIDTH), head_spec, head_spec, cache_spec, cache_spec),
        compiler_params=_params(1),
        name="gqa_proj",
    )(*x_args, mods5, mods5, w_qkv, gain, cos_t, sin_t)


def _attend(qs, chunks, sink_col):
    m = den = acc = None
    for k_fn, v_fn, bias, _ in chunks:
        s = lax.dot_general(qs, k_fn(), _NT, preferred_element_type=F32)
        if bias is not None:
            s = s + bias
        mc = jnp.max(s, axis=-1, keepdims=True)
        m_new = mc if m is None else jnp.maximum(m, mc)
        p = jnp.exp2(s - m_new)
        ps = jnp.sum(p, axis=-1, keepdims=True)
        pv = jnp.dot(p.astype(BF16), v_fn(), preferred_element_type=F32)
        if m is None:
            den, acc = ps, pv
        else:
            alpha = jnp.exp2(m - m_new)
            den = den * alpha + ps
            acc = acc * alpha + pv
        m = m_new
    if sink_col is not None:
        den = den + jnp.exp2(sink_col - m)
    return acc / den


def _chunk_list(k_ref, v_ref, length, cast):
    out = []
    width = min(KV_CHUNK, length)
    for c0 in range(0, length, width):
        if cast:
            k_fn = lambda c0=c0: k_ref[c0:c0 + width, :].astype(BF16)
            v_fn = lambda c0=c0: v_ref[c0:c0 + width, :].astype(BF16)
        else:
            k_fn = lambda c0=c0: k_ref[c0:c0 + width, :]
            v_fn = lambda c0=c0: v_ref[c0:c0 + width, :]
        out.append((k_fn, v_fn, None, width))
    return out


def _gqa_attn_kernel(*refs, tq, n_keys, kv_groups, has_ctx, use_sink, window):
    refs = list(refs)
    q_ref, k_ref, v_ref = refs[:3]
    pos = 3
    if has_ctx:
        kc_ref, vc_ref = refs[pos:pos + 2]
        pos += 2
    if use_sink:
        sink_ref = refs[pos]
        pos += 1
    o_ref = refs[pos]
    qi = pl.program_id(2)
    width = GROUP * HEAD_DIM

    for gi in range(kv_groups):
        g = pl.program_id(1) * kv_groups + gi
        qf = q_ref[:, gi * width:(gi + 1) * width].astype(F32)
        qs = jnp.concatenate(
            [qf[:, j * HEAD_DIM:(j + 1) * HEAD_DIM] for j in range(GROUP)], axis=0).astype(BF16)
        kg_ref, vg_ref = k_ref.at[gi], v_ref.at[gi]

        if window:
            span = tq + 2 * WINDOW
            kstart = pl.multiple_of(jnp.clip(qi * tq - WINDOW, 0, n_keys - span), LANES)
            qpos = qi * tq + lax.broadcasted_iota(jnp.int32, (tq, span), 0)
            kpos = kstart + lax.broadcasted_iota(jnp.int32, (tq, span), 1)
            band = jnp.where(jnp.abs(kpos - qpos) <= WINDOW, 0.0, -jnp.inf).astype(F32)
            bias = jnp.concatenate([band] * GROUP, axis=0)
            chunks = [(lambda: kg_ref[pl.ds(kstart, span), :], lambda: vg_ref[pl.ds(kstart, span), :], bias, span)]
        else:
            chunks = _chunk_list(kg_ref, vg_ref, n_keys, cast=False)
        if has_ctx:
            chunks += _chunk_list(kc_ref.at[gi], vc_ref.at[gi], PAST_LEN, cast=True)

        sink_col = None
        if use_sink:
            row = lax.broadcasted_iota(jnp.int32, (GROUP * tq, 1), 0)
            sink_col = jnp.full((GROUP * tq, 1), sink_ref[g * GROUP + GROUP - 1], F32)
            for j in reversed(range(GROUP - 1)):
                sink_col = jnp.where(row < (j + 1) * tq, sink_ref[g * GROUP + j], sink_col)
            sink_col = sink_col * LOG2E

        o = _attend(qs, chunks, sink_col)
        o_ref[:, gi * width:(gi + 1) * width] = jnp.concatenate(
            [o[j * tq:(j + 1) * tq, :] for j in range(GROUP)], axis=-1).astype(BF16)


def _gqa_attention(q, kh, vh, ctx_k, ctx_v, slot, sink, *, latent, window):
    tq = GQA_TQ if latent and not window else ATT_TQ
    if latent:
        n_b, n_keys, n_qt = DEC_BATCH, DEC_SEQ, DEC_SEQ // tq
        row0 = N_PROMPT // tq
        kv_blk0 = N_PROMPT // DEC_SEQ
        kv_groups = 2
    else:
        n_b, n_keys, n_qt = BATCH, SEQ, 1
        row0 = 0
        kv_blk0 = 0
        kv_groups = N_KV_HEADS
    width = kv_groups * GROUP * HEAD_DIM
    q_spec = pl.BlockSpec((tq, width), lambda b, g, t: (row0 + b * n_qt + t, g))
    kv_spec = pl.BlockSpec((kv_groups, n_keys, HEAD_DIM), lambda b, g, t: (g, kv_blk0 + b, 0))
    in_specs = [q_spec, kv_spec, kv_spec]
    args = [q, kh, vh]
    if latent:
        ctx_spec = pl.BlockSpec((None, None, kv_groups, PAST_LEN, HEAD_DIM), lambda b, g, t: (b, slot, g, 0, 0))
        in_specs += [ctx_spec, ctx_spec]
        args += [ctx_k, ctx_v]
    use_sink = sink is not None
    if use_sink:
        in_specs.append(pl.BlockSpec(memory_space=pltpu.SMEM))
        args.append(sink)
    return pl.pallas_call(
        functools.partial(_gqa_attn_kernel, tq=tq, n_keys=n_keys, kv_groups=kv_groups, has_ctx=latent,
                          use_sink=use_sink, window=window),
        out_shape=jax.ShapeDtypeStruct((n_b * n_keys, Q_WIDTH), BF16),
        grid=(n_b, N_KV_HEADS // kv_groups, n_qt),
        in_specs=in_specs,
        out_specs=pl.BlockSpec((tq, width), lambda b, g, t: (b * n_qt + t, g)),
        compiler_params=_params(3),
        name="gqa_attn_latent" if latent else "gqa_attn_context",
    )(*args)


MLA_DOWN_EXT = Q_RANK + KV_RANK + LANES
MLA_HEAD_LANES = LANES
MLA_ROPE_LO = MLA_NOPE_DIM


def _mla_proj_kernel(x_ref, sh_ref, sc_ref, wd_ref, qg_ref, wuq_ref, kvg_ref, cos_ref, sin_ref,
                     q_ref, ckv_ref, kpe_ref):
    h = (x_ref[...] * (1.0 + sc_ref[...]) + sh_ref[...]).astype(BF16)
    acc = jnp.dot(h, wd_ref[...], preferred_element_type=F32)
    cos = cos_ref[...]
    sin = sin_ref[...]
    qa = acc[:, :Q_RANK]
    qn = qa * lax.rsqrt(jnp.mean(qa * qa, axis=-1, keepdims=True) + NORM_EPS) * qg_ref[...]
    q = jnp.dot(qn.astype(BF16), wuq_ref[...], preferred_element_type=F32)
    for hd in range(MLA_HEADS):
        t = q[:, hd * LANES:(hd + 1) * LANES]
        q_ref[:, hd * LANES:(hd + 1) * LANES] = (_rope_lanes(t, cos, sin, MLA_ROPE_DIM // 2) * MLA_Q_SCALE).astype(BF16)
    ckv = acc[:, Q_RANK:Q_RANK + KV_RANK]
    ckv_ref[...] = ckv * lax.rsqrt(jnp.mean(ckv * ckv, axis=-1, keepdims=True) + NORM_EPS) * kvg_ref[...]
    kpe_ref[...] = _rope_lanes(acc[:, Q_RANK + KV_RANK:], cos, sin, MLA_ROPE_DIM // 2)


def _mla_proj(x, mods5, layer, wd_ext, q_gain, wuq_pad, kv_gain, cos_t, sin_t):
    pos_spec = pl.BlockSpec((TM, LANES), lambda i: (_pos_block_of_tile(i), 0))
    return pl.pallas_call(
        _mla_proj_kernel,
        out_shape=(
            jax.ShapeDtypeStruct((N_TOK, MLA_HEADS * LANES), BF16),
            jax.ShapeDtypeStruct((N_TOK, KV_RANK), F32),
            jax.ShapeDtypeStruct((N_TOK, LANES), F32),
        ),
        grid=(TILES,),
        in_specs=[
            _row_spec(D_MODEL), _mod_spec(layer, 0), _mod_spec(layer, 1),
            _full_spec((D_MODEL, MLA_DOWN_EXT)), _full_spec((1, Q_RANK)),
            _full_spec((Q_RANK, MLA_HEADS * LANES)), _full_spec((1, KV_RANK)),
            pos_spec, pos_spec,
        ],
        out_specs=(_row_spec(MLA_HEADS * LANES), _row_spec(KV_RANK), _row_spec(LANES)),
        compiler_params=_params(1),
        name="mla_proj",
    )(x, mods5, mods5, wd_ext, q_gain, wuq_pad, kv_gain, cos_t, sin_t)


def _mla_kv_kernel(c_ref, kpe_ref, wk_ref, wv_ref, k_ref, v_ref):
    c = c_ref[...].astype(BF16)
    kk = jnp.dot(c, wk_ref[...], preferred_element_type=F32)
    vv = jnp.dot(c, wv_ref[...], preferred_element_type=F32)
    kpe = kpe_ref[...]
    for hd in range(MLA_HEADS):
        k_ref[hd] = (kk[:, hd * LANES:(hd + 1) * LANES] + kpe).astype(BF16)
        v_ref[hd] = vv[:, hd * MLA_V_DIM:(hd + 1) * MLA_V_DIM].astype(BF16)


def _mla_kv(c_all, kpe_all, wk_pad, wv):
    n_rows = c_all.shape[0]
    return pl.pallas_call(
        _mla_kv_kernel,
        out_shape=(
            jax.ShapeDtypeStruct((MLA_HEADS, n_rows, LANES), BF16),
            jax.ShapeDtypeStruct((MLA_HEADS, n_rows, MLA_V_DIM), BF16),
        ),
        grid=(n_rows // TM,),
        in_specs=[
            _row_spec(KV_RANK), _row_spec(LANES),
            _full_spec((KV_RANK, MLA_HEADS * LANES)), _full_spec((KV_RANK, MLA_HEADS * MLA_V_DIM)),
        ],
        out_specs=(
            pl.BlockSpec((MLA_HEADS, TM, LANES), lambda i: (0, i, 0)),
            pl.BlockSpec((MLA_HEADS, TM, MLA_V_DIM), lambda i: (0, i, 0)),
        ),
        compiler_params=_params(1),
        name="mla_kv",
    )(c_all, kpe_all, wk_pad, wv)


def _mla_attn_kernel(*refs, tq, n_keys, heads, has_ctx):
    refs = list(refs)
    q_ref, k_ref, v_ref = refs[:3]
    pos = 3
    if has_ctx:
        kc_ref, vc_ref = refs[pos:pos + 2]
        pos += 2
    o_ref = refs[pos]
    outs = []
    for j in range(heads):
        qs = q_ref[:, j * LANES:(j + 1) * LANES]
        chunks = _chunk_list(k_ref.at[j], v_ref.at[j], n_keys, cast=False)
        if has_ctx:
            chunks += _chunk_list(kc_ref.at[j], vc_ref.at[j], PAST_LEN, cast=False)
        outs.append(_attend(qs, chunks, None))
    o_ref[...] = jnp.concatenate(outs, axis=-1).astype(BF16)


def _mla_attention(q, k_all, v_all, *, latent):
    if latent:
        tq = MLA_TQ
        n_b, n_keys, n_qt = DEC_BATCH, DEC_SEQ, DEC_SEQ // tq
        row0 = N_PROMPT // tq
        kv_blk0 = N_PROMPT // DEC_SEQ
        heads = 4
    else:
        tq = ATT_TQ
        n_b, n_keys, n_qt = BATCH, SEQ, 1
        row0 = 0
        kv_blk0 = 0
        heads = 8
    q_spec = pl.BlockSpec((tq, heads * LANES), lambda b, hp, t: (row0 + b * n_qt + t, hp))
    k_spec = pl.BlockSpec((heads, n_keys, LANES), lambda b, hp, t: (hp, kv_blk0 + b, 0))
    v_spec = pl.BlockSpec((heads, n_keys, MLA_V_DIM), lambda b, hp, t: (hp, kv_blk0 + b, 0))
    in_specs = [q_spec, k_spec, v_spec]
    args = [q, k_all, v_all]
    if latent:
        ctx0 = N_TOK // PAST_LEN
        in_specs += [
            pl.BlockSpec((heads, PAST_LEN, LANES), lambda b, hp, t: (hp, ctx0 + b, 0)),
            pl.BlockSpec((heads, PAST_LEN, MLA_V_DIM), lambda b, hp, t: (hp, ctx0 + b, 0)),
        ]
        args += [k_all, v_all]
    return pl.pallas_call(
        functools.partial(_mla_attn_kernel, tq=tq, n_keys=n_keys, heads=heads, has_ctx=latent),
        out_shape=jax.ShapeDtypeStruct((n_b * n_keys, MLA_HEADS * MLA_V_DIM), BF16),
        grid=(n_b, MLA_HEADS // heads, n_qt),
        in_specs=in_specs,
        out_specs=pl.BlockSpec((tq, heads * MLA_V_DIM), lambda b, hp, t: (b * n_qt + t, hp)),
        compiler_params=_params(3),
        name="mla_attn_latent" if latent else "mla_attn_context",
    )(*args)


def _post_attn_kernel(*refs, n_x):
    (op_ref, os_ref), x_refs, refs = refs[:2], refs[2:2 + n_x], refs[2 + n_x:]
    wo_ref, g1_ref, sh2_ref, sc2_ref, lng_ref, lnb_ref, wr_ref, x1_ref, h2_ref, aff_ref, fz_ref = refs
    proj = jnp.dot(_read_rows((op_ref, os_ref)), wo_ref[...], preferred_element_type=F32)
    x1 = _layer_norm(ALPHA * _read_rows(x_refs) + g1_ref[...] * proj, lng_ref[...], lnb_ref[...])
    x1_ref[...] = x1
    h2 = x1 * (1.0 + sc2_ref[...]) + sh2_ref[...]
    _store_token_tiles(h2_ref, h2)
    logits_t = _dot3(wr_ref[...], h2, _NT)
    e = jnp.exp(logits_t - jnp.max(logits_t, axis=0, keepdims=True))
    aff_ref[...] = e / jnp.sum(e, axis=0, keepdims=True)
    fz_ref[...] = jnp.zeros_like(fz_ref)


def _post_attn(o_p, o_s, x, w_o, mods5, layer, ln_g, ln_b, w_router_t):
    x_specs, x_args = _row_inputs(x)
    return pl.pallas_call(
        functools.partial(_post_attn_kernel, n_x=len(x_args)),
        out_shape=(
            jax.ShapeDtypeStruct((N_TOK, D_MODEL), F32),
            jax.ShapeDtypeStruct((N_TOK * TOK_TILE, LANES), F32),
            jax.ShapeDtypeStruct((N_EXPERTS, N_TOK), F32),
            jax.ShapeDtypeStruct((N_TOK * TOK_TILE, LANES), F32),
        ),
        grid=(TILES,),
        in_specs=_stream_specs(D_MODEL) + x_specs + [
            _full_spec((D_MODEL, D_MODEL)),
            _mod_spec(layer, 2), _mod_spec(layer, 3), _mod_spec(layer, 4),
            _full_spec((1, D_MODEL)), _full_spec((1, D_MODEL)), _full_spec((N_EXPERTS, D_MODEL)),
        ],
        out_specs=(_row_spec(D_MODEL), _tok_tile_spec(), pl.BlockSpec((N_EXPERTS, TM), lambda i: (0, i)),
                   _tok_tile_spec()),
        compiler_params=_params(1),
        name="post_attn",
    )(o_p, o_s, *x_args, w_o, mods5, mods5, mods5, ln_g, ln_b, w_router_t)


FFN_ROWS = 512
FFN_STEPS = EXPERT_FF // FF_TILE
ROW_CHUNKS = CAP // FFN_ROWS
GATHER_PER_CHUNK = CAP // (FFN_STEPS * ROW_CHUNKS)


N_DMA_THREADS = 2


def _token_copy(src, src_tok, dst, dst_tok, sem):
    return pltpu.make_async_copy(_token_tiles(src).at[src_tok], _token_tiles(dst).at[dst_tok], sem)


def _wait_buffer(buf_ref, sem):
    pltpu.make_async_copy(buf_ref, buf_ref, sem).wait()


def _ffn_kernel(idx_ref, h_hbm, f_in_hbm, wg_ref, wu_ref, wd_ref, gate_ref, f_hbm,
                xbuf, acc, fbuf, gsem, rsem, wsem):
    del f_in_hbm
    e = pl.program_id(0)
    f = pl.program_id(1)
    buf = e % 2
    nxt = jnp.minimum(e + 1, N_EXPERTS - 1)

    def gather(expert, s, b):
        return _token_copy(h_hbm, idx_ref[expert * CAP + s], xbuf.at[b], s, gsem.at[b])

    @pl.when((e == 0) & (f == 0))
    def _():
        def issue(i, carry):
            for thread in range(N_DMA_THREADS):
                gather(0, i * N_DMA_THREADS + thread, 0).start(priority=thread)
            return carry
        lax.fori_loop(0, CAP // N_DMA_THREADS, issue, 0, unroll=8)

    @pl.when(f == 0)
    def _():
        _wait_buffer(xbuf.at[buf], gsem.at[buf])

    @pl.when((f == 0) & (e > 0))
    def _():
        _wait_buffer(fbuf, wsem.at[0])

    def write_rows(r_lo, n_rows):
        def body(i, carry):
            for thread in range(N_DMA_THREADS):
                r = r_lo + i * N_DMA_THREADS + thread
                _token_copy(fbuf, r, f_hbm, idx_ref[e * CAP + r], wsem.at[0]).start(priority=thread)
            return carry
        lax.fori_loop(0, n_rows // N_DMA_THREADS, body, 0, unroll=16)

    wg = wg_ref[...].astype(BF16)
    wu = wu_ref[...].astype(BF16)
    wd = wd_ref[...].astype(BF16)
    for c in range(ROW_CHUNKS):
        r0 = c * FFN_ROWS
        x = _load_token_tiles(xbuf.at[buf], FFN_ROWS, r0).astype(BF16)
        a = jnp.dot(x, wg, preferred_element_type=F32)
        u = jnp.dot(x, wu, preferred_element_type=F32)
        hid = (a / (1.0 + jnp.exp(-a)) * u).astype(BF16)
        part = jnp.dot(hid, wd, preferred_element_type=F32)
        for j in range(GATHER_PER_CHUNK):
            s = (f * ROW_CHUNKS + c) * GATHER_PER_CHUNK + j
            gather(nxt, s, 1 - buf).start(priority=j % N_DMA_THREADS)
            _token_copy(f_hbm, idx_ref[e * CAP + s], fbuf, s, rsem.at[0]).start(priority=(j + 1) % N_DMA_THREADS)
        acc[r0:r0 + FFN_ROWS, :] = jnp.where(f == 0, 0.0, acc[r0:r0 + FFN_ROWS, :]) + part

    @pl.when(f == FFN_STEPS - 1)
    def _():
        _wait_buffer(fbuf, rsem.at[0])
        for k in range(ROW_CHUNKS):
            r0 = k * FFN_ROWS
            g = gate_ref[r0:r0 + FFN_ROWS, :]
            for j in range(TOK_TILE):
                rows = _lane_block_rows(j, FFN_ROWS, r0)
                fbuf[rows, :] = fbuf[rows, :] + acc[r0:r0 + FFN_ROWS, j * LANES:(j + 1) * LANES] * g
            write_rows(r0, FFN_ROWS)

    @pl.when((e == N_EXPERTS - 1) & (f == FFN_STEPS - 1))
    def _():
        _wait_buffer(fbuf, wsem.at[0])
        _wait_buffer(xbuf.at[1 - buf], gsem.at[1 - buf])


def _expert_ffn(idx, h2, f_zero, gate, w_gate, w_up, w_down, layer):
    grid_spec = pltpu.PrefetchScalarGridSpec(
        num_scalar_prefetch=1,
        grid=(N_EXPERTS, FFN_STEPS),
        in_specs=[
            pl.BlockSpec(memory_space=pl.ANY),
            pl.BlockSpec(memory_space=pl.ANY),
            pl.BlockSpec((None, None, D_MODEL, FF_TILE), lambda e, f, idx: (layer, e, 0, f)),
            pl.BlockSpec((None, None, D_MODEL, FF_TILE), lambda e, f, idx: (layer, e, 0, f)),
            pl.BlockSpec((None, None, FF_TILE, D_MODEL), lambda e, f, idx: (layer, e, f, 0)),
            pl.BlockSpec((None, CAP, 1), lambda e, f, idx: (e, 0, 0)),
        ],
        out_specs=pl.BlockSpec(memory_space=pl.ANY),
        scratch_shapes=[
            pltpu.VMEM((2, CAP * TOK_TILE, LANES), F32),
            pltpu.VMEM((CAP, D_MODEL), F32),
            pltpu.VMEM((CAP * TOK_TILE, LANES), F32),
            pltpu.SemaphoreType.DMA((2,)), pltpu.SemaphoreType.DMA((1,)), pltpu.SemaphoreType.DMA((1,)),
        ],
    )
    tiles = (N_TOK, TOK_TILE, LANES)
    f = pl.pallas_call(
        _ffn_kernel,
        out_shape=jax.ShapeDtypeStruct(tiles, F32),
        grid_spec=grid_spec,
        input_output_aliases={2: 0},
        compiler_params=_params(2),
        name="expert_ffn",
    )(idx, h2.reshape(tiles), f_zero.reshape(tiles), w_gate, w_up, w_down, gate)
    return f.reshape(N_TOK * TOK_TILE, LANES)


def _post_moe_kernel(x_ref, f_ref, g2_ref, lng_ref, lnb_ref, *o_refs):
    f = _load_token_tiles(f_ref, TM)
    y = _layer_norm(ALPHA * x_ref[...] + g2_ref[...] * f, lng_ref[...], lnb_ref[...])
    if len(o_refs) == 1:
        o_refs[0][...] = y
    else:
        @pl.when(pl.program_id(0) < PROMPT_TILES)
        def _():
            o_refs[0][...] = y

        @pl.when(pl.program_id(0) >= PROMPT_TILES)
        def _():
            o_refs[1][...] = y


def _post_moe(x1, f, mods5, layer, ln_g, ln_b, split_streams):
    if split_streams:
        out_shape = (jax.ShapeDtypeStruct((N_PROMPT, D_MODEL), F32), jax.ShapeDtypeStruct((N_SAMPLE, D_MODEL), F32))
        out_specs = tuple(_stream_specs(D_MODEL))
    else:
        out_shape = jax.ShapeDtypeStruct((N_TOK, D_MODEL), F32)
        out_specs = _row_spec(D_MODEL)
    return pl.pallas_call(
        _post_moe_kernel,
        out_shape=out_shape,
        grid=(TILES,),
        in_specs=[_row_spec(D_MODEL), _tok_tile_spec(), _mod_spec(layer, 5),
                  _full_spec((1, D_MODEL)), _full_spec((1, D_MODEL))],
        out_specs=out_specs,
        compiler_params=_params(1),
        name="post_moe",
    )(x1, f, mods5, ln_g, ln_b)


def _axial_tables(dim):
    n_rows = DEC_SEQ // GRID_W
    rows = jnp.repeat(jnp.arange(n_rows, dtype=F32), GRID_W)
    cols = jnp.tile(jnp.arange(GRID_W, dtype=F32), n_rows)
    n_freq = dim // 4
    inv_freq = ROPE_THETA ** (-jnp.arange(n_freq, dtype=F32) / n_freq)
    ang = jnp.concatenate([rows[:, None] * inv_freq, cols[:, None] * inv_freq], -1)
    cos, sin = jnp.cos(ang), jnp.sin(ang)
    return jnp.concatenate([cos, cos], -1), jnp.concatenate([-sin, sin], -1)


def _with_identity_rows(cos_l, sin_l):
    return (jnp.concatenate([jnp.ones((TM, LANES), F32), cos_l], 0),
            jnp.concatenate([jnp.zeros((TM, LANES), F32), sin_l], 0))


def _gqa_rope_tables():
    cos, sin = _axial_tables(HEAD_DIM)
    return _with_identity_rows(jnp.tile(cos, (1, 2)), jnp.tile(sin, (1, 2)))


def _mla_rope_tables():
    cos, sin = _axial_tables(MLA_ROPE_DIM)
    pad_hi = LANES - MLA_ROPE_LO - MLA_ROPE_DIM
    cos_l = jnp.concatenate([jnp.ones((DEC_SEQ, MLA_ROPE_LO), F32), cos, jnp.ones((DEC_SEQ, pad_hi), F32)], 1)
    sin_l = jnp.pad(sin, ((0, 0), (MLA_ROPE_LO, pad_hi)))
    return _with_identity_rows(cos_l, sin_l)


def _moe(h2, f_zero, aff_t, w_gate, w_up, w_down, layer):
    gate_p, idx_p = lax.top_k(aff_t[:, :N_PROMPT], CAP_P)
    gate_s, idx_s = lax.top_k(aff_t[:, N_PROMPT:], CAP_S)
    idx = jnp.concatenate([idx_p, idx_s + N_PROMPT], axis=1).reshape(-1)
    gate = jnp.concatenate([gate_p, gate_s], axis=1)[..., None]
    return _expert_ffn(idx, h2, f_zero, gate, w_gate, w_up, w_down, layer)


def kernel(x_prompt, x_sample, cache_gqa_k, cache_gqa_v, cache_mla_ckv, cache_mla_kpe, c, c_ctx,
           ada_w, ada_b, ln_g, ln_b, a_w_qkv, a_q_norm, a_k_norm, a_w_o, b_w_qkv, b_sink, b_w_o,
           mla_w_down, mla_q_norm, mla_w_uq, mla_kv_norm, mla_w_ukv, mla_w_o,
           moe_w_router, moe_w_gate, moe_w_up, moe_w_down):
    x = (x_prompt.reshape(N_PROMPT, D_MODEL), x_sample.reshape(N_SAMPLE, D_MODEL))
    cond = jnp.concatenate([c_ctx[None], c, jnp.zeros((N_COND - 1 - DEC_BATCH, D_MODEL), F32)], 0)
    mods5 = _adaln(cond, ada_w, ada_b).reshape(DEPTH, 6, N_COND, 1, D_MODEL)
    gqa_cos, gqa_sin = _gqa_rope_tables()
    mla_cos, mla_sin = _mla_rope_tables()

    new_k, new_v, new_ckv, new_kpe = [], [], [], []
    gqa_slot = 0
    for i in range(DEPTH):
        kind, j = i % 3, i // 3
        if kind in (0, 1):
            if kind == 0:
                w_qkv, w_o, sink = a_w_qkv[j], a_w_o[j], None
                gain = jnp.concatenate([jnp.tile(a_q_norm[j], N_HEADS), jnp.tile(a_k_norm[j], N_KV_HEADS)])[None]
            else:
                w_qkv, w_o, sink = b_w_qkv[j], b_w_o[j], b_sink[j]
                gain = jnp.ones((1, QK_WIDTH), F32)
            q, kh, vh, k32, v32 = _gqa_proj(x, mods5, i, w_qkv.astype(BF16), gain, gqa_cos, gqa_sin,
                                            use_norm=(kind == 0))
            o_p = _gqa_attention(q, kh, vh, None, None, gqa_slot, sink, latent=False, window=False)
            o_s = _gqa_attention(q, kh, vh, cache_gqa_k, cache_gqa_v, gqa_slot, sink,
                                 latent=True, window=(kind == 1))
            new_k.append(k32)
            new_v.append(v32)
            gqa_slot += 1
        else:
            w_down = mla_w_down[j]
            rope_lo = Q_RANK + KV_RANK
            wd_ext = jnp.concatenate([
                w_down[:, :rope_lo], jnp.zeros((D_MODEL, MLA_ROPE_LO), F32), w_down[:, rope_lo:],
                jnp.zeros((D_MODEL, LANES - MLA_ROPE_LO - MLA_ROPE_DIM), F32)], 1).astype(BF16)
            wuq_pad = jnp.pad(mla_w_uq[j].reshape(Q_RANK, MLA_HEADS, MLA_QK_DIM),
                              ((0, 0), (0, 0), (0, LANES - MLA_QK_DIM))).reshape(Q_RANK, MLA_HEADS * LANES)
            w_ukv = mla_w_ukv[j].reshape(KV_RANK, MLA_HEADS, MLA_NOPE_DIM + MLA_V_DIM)
            wk_pad = jnp.pad(w_ukv[:, :, :MLA_NOPE_DIM], ((0, 0), (0, 0), (0, LANES - MLA_NOPE_DIM)))
            wk_pad = wk_pad.reshape(KV_RANK, MLA_HEADS * LANES).astype(BF16)
            wv = w_ukv[:, :, MLA_NOPE_DIM:].reshape(KV_RANK, MLA_HEADS * MLA_V_DIM).astype(BF16)
            q, ckv, kpe = _mla_proj(x, mods5, i, wd_ext, mla_q_norm[j][None], wuq_pad.astype(BF16),
                                    mla_kv_norm[j][None], mla_cos, mla_sin)
            c_all = jnp.concatenate([ckv, cache_mla_ckv[:, j].reshape(DEC_BATCH * PAST_LEN, KV_RANK)], 0)
            kpe_ctx = jnp.pad(cache_mla_kpe[:, j].reshape(DEC_BATCH * PAST_LEN, MLA_ROPE_DIM),
                              ((0, 0), (MLA_ROPE_LO, LANES - MLA_ROPE_LO - MLA_ROPE_DIM)))
            kpe_all = jnp.concatenate([kpe, kpe_ctx], 0)
            k_all, v_all = _mla_kv(c_all, kpe_all, wk_pad, wv)
            o_p = _mla_attention(q, k_all, v_all, latent=False)
            o_s = _mla_attention(q, k_all, v_all, latent=True)
            w_o = mla_w_o[j]
            new_ckv.append(ckv[:N_PROMPT].reshape(BATCH, SEQ, KV_RANK))
            new_kpe.append(kpe[:N_PROMPT, MLA_ROPE_LO:MLA_ROPE_LO + MLA_ROPE_DIM].reshape(BATCH, SEQ, MLA_ROPE_DIM))
        x1, h2, aff_t, f_zero = _post_attn(o_p, o_s, x, w_o.astype(BF16), mods5, i, ln_g[i, 0][None],
                                           ln_b[i, 0][None], moe_w_router[i].T)
        f = _moe(h2, f_zero, aff_t, moe_w_gate, moe_w_up, moe_w_down, i)
        x = _post_moe(x1, f, mods5, i, ln_g[i, 1][None], ln_b[i, 1][None], split_streams=(i == DEPTH - 1))

    y_prompt = x[0].reshape(BATCH, SEQ, D_MODEL)
    y_sample = x[1].reshape(DEC_BATCH, DEC_SEQ, D_MODEL)
    return (y_prompt, y_sample, jnp.stack(new_k, 1), jnp.stack(new_v, 1),
            jnp.stack(new_ckv, 1), jnp.stack(new_kpe, 1))
```

```python
import functools
import math

import jax
import jax.numpy as jnp
from jax import lax
from jax.experimental import pallas as pl
from jax.experimental.pallas import tpu as pltpu

F32 = jnp.float32
BF16 = jnp.bfloat16

D_MODEL = 1024
BATCH = 16
SEQ = 256
DEPTH = 4
DEC_BATCH = 4
DEC_SEQ = 2048
PAST_LEN = 512
GRID_W = 64
N_HEADS = 16
N_KV_HEADS = 4
GROUP = N_HEADS // N_KV_HEADS
HEAD_DIM = 64
WINDOW = 128
MLA_HEADS = 16
Q_RANK = 256
KV_RANK = 128
MLA_NOPE_DIM = 64
MLA_ROPE_DIM = 32
MLA_V_DIM = 64
MLA_QK_DIM = MLA_NOPE_DIM + MLA_ROPE_DIM
MLA_SCALE = MLA_QK_DIM ** -0.5
GQA_SCALE = HEAD_DIM ** -0.5
N_EXPERTS = 16
EXPERT_FF = 2048
CAPACITY_FACTOR = 2
ROPE_THETA = 10000.0
NORM_EPS = 1e-6
LN_EPS = 1e-5
ALPHA = (2 * DEPTH) ** 0.25
LOG2E = math.log2(math.e)
GQA_Q_SCALE = GQA_SCALE * LOG2E
MLA_Q_SCALE = MLA_SCALE * LOG2E

LANES = 128
N_PROMPT = BATCH * SEQ
N_SAMPLE = DEC_BATCH * DEC_SEQ
N_TOK = N_PROMPT + N_SAMPLE
TM = 512
ATT_TQ = SEQ
PROMPT_TILES = N_PROMPT // TM
TILES = N_TOK // TM
SAMPLE_TILES_PER_REQ = DEC_SEQ // TM
N_COND = 8
CAP_P = CAPACITY_FACTOR * N_PROMPT // N_EXPERTS
CAP_S = CAPACITY_FACTOR * N_SAMPLE // N_EXPERTS
CAP = CAP_P + CAP_S
FF_TILE = 512
KV_CHUNK = 1024
GQA_TQ = 512
MLA_TQ = 1024

_NT = (((1,), (1,)), ((), ()))
_VMEM_LIMIT = 56 * 1024 * 1024


def _params(n_axes, vmem=_VMEM_LIMIT):
    return pltpu.CompilerParams(dimension_semantics=("arbitrary",) * n_axes, vmem_limit_bytes=vmem)


def _cond_of_tile(i):
    return jnp.where(i < PROMPT_TILES, 0, 1 + (i - PROMPT_TILES) // SAMPLE_TILES_PER_REQ)


def _pos_block_of_tile(i):
    return jnp.where(i < PROMPT_TILES, 0, 1 + (i - PROMPT_TILES) % SAMPLE_TILES_PER_REQ)


def _mod_spec(layer, k):
    return pl.BlockSpec((None, None, None, 1, D_MODEL), lambda i: (layer, k, _cond_of_tile(i), 0, 0))


def _row_spec(width):
    return pl.BlockSpec((TM, width), lambda i: (i, 0))


def _full_spec(shape):
    nd = len(shape)
    return pl.BlockSpec(shape, lambda *_: (0,) * nd)


def _stream_specs(width):
    return [pl.BlockSpec((TM, width), lambda i: (jnp.minimum(i, PROMPT_TILES - 1), 0)),
            pl.BlockSpec((TM, width), lambda i: (jnp.maximum(i - PROMPT_TILES, 0), 0))]


def _read_rows(refs):
    if len(refs) == 1:
        return refs[0][...]
    return jnp.where(pl.program_id(0) < PROMPT_TILES, refs[0][...], refs[1][...])


def _row_inputs(x):
    if isinstance(x, tuple):
        return _stream_specs(x[0].shape[1]), list(x)
    return [_row_spec(x.shape[1])], [x]


TOK_TILE = 8


def _tok_tile_spec():
    return pl.BlockSpec((TM * TOK_TILE, LANES), lambda i: (i, 0))


def _token_tiles(ref):
    if len(ref.shape) == 3:
        return ref
    return ref.reshape(ref.shape[0] // TOK_TILE, TOK_TILE, LANES)


def _lane_block_rows(j, n_tokens, t0=0):
    return pl.ds(t0 * TOK_TILE + j, n_tokens, stride=TOK_TILE)


def _store_token_tiles(ref, val):
    for j in range(TOK_TILE):
        ref[_lane_block_rows(j, val.shape[0]), :] = val[:, j * LANES:(j + 1) * LANES]


def _load_token_tiles(ref, n_tokens, t0=0):
    return jnp.concatenate([ref[_lane_block_rows(j, n_tokens, t0), :] for j in range(TOK_TILE)], axis=-1)


def _split_bf16(a):
    hi = a.astype(BF16)
    lo = (a - hi.astype(F32)).astype(BF16)
    return hi, lo


def _dot3(a, b, dims=(((1,), (0,)), ((), ()))):
    a_hi, a_lo = _split_bf16(a)
    b_hi, b_lo = _split_bf16(b)
    dg = functools.partial(lax.dot_general, dimension_numbers=dims, preferred_element_type=F32)
    return dg(a_hi, b_hi) + dg(a_lo, b_hi) + dg(a_hi, b_lo)


def _layer_norm(y, g, b):
    mu = jnp.mean(y, axis=-1, keepdims=True)
    d = y - mu
    var = jnp.mean(d * d, axis=-1, keepdims=True)
    return d * lax.rsqrt(var + LN_EPS) * g + b


def _rope_lanes(t, cos, sin, half):
    lane = lax.broadcasted_iota(jnp.int32, t.shape, 1)
    first = (lane & (2 * half - 1)) < half
    partner = jnp.where(first, pltpu.roll(t, LANES - half, 1), pltpu.roll(t, half, 1))
    return t * cos + partner * sin


def _pair_rms(t, gain):
    sq = t * t
    lane = lax.broadcasted_iota(jnp.int32, t.shape, 1)
    lo = lane < HEAD_DIM
    s_lo = jnp.sum(jnp.where(lo, sq, 0.0), axis=-1, keepdims=True)
    s_hi = jnp.sum(jnp.where(lo, 0.0, sq), axis=-1, keepdims=True)
    ms = jnp.where(lo, s_lo, s_hi) * (1.0 / HEAD_DIM)
    return t * lax.rsqrt(ms + NORM_EPS) * gain


def _adaln_kernel(c_ref, w_ref, b_ref, o_ref):
    c = c_ref[...]
    a = c / (1.0 + jnp.exp(-c))
    o_ref[...] = _dot3(a, w_ref[...]) + b_ref[...]


def _adaln(cond, ada_w, ada_b):
    return pl.pallas_call(
        _adaln_kernel,
        out_shape=jax.ShapeDtypeStruct((DEPTH, 6, N_COND, D_MODEL), F32),
        grid=(DEPTH, 6),
        in_specs=[
            pl.BlockSpec((N_COND, D_MODEL), lambda l, k: (0, 0)),
            pl.BlockSpec((None, D_MODEL, D_MODEL), lambda l, k: (l, 0, k)),
            pl.BlockSpec((None, None, 1, D_MODEL), lambda l, k: (l, k, 0, 0)),
        ],
        out_specs=pl.BlockSpec((None, None, N_COND, D_MODEL), lambda l, k: (l, k, 0, 0)),
        compiler_params=_params(2),
        name="adaln",
    )(cond, ada_w, ada_b.reshape(DEPTH, 6, 1, D_MODEL))


QK_WIDTH = (N_HEADS + N_KV_HEADS) * HEAD_DIM
QKV_WIDTH = (N_HEADS + 2 * N_KV_HEADS) * HEAD_DIM
Q_WIDTH = N_HEADS * HEAD_DIM


def _gqa_proj_kernel(*refs, use_norm, n_x):
    x_refs, refs = refs[:n_x], refs[n_x:]
    sh_ref, sc_ref, w_ref, gain_ref, cos_ref, sin_ref, q_ref, kh_ref, vh_ref, k32_ref, v32_ref = refs
    h = (_read_rows(x_refs) * (1.0 + sc_ref[...]) + sh_ref[...]).astype(BF16)
    acc = jnp.dot(h, w_ref[...], preferred_element_type=F32)
    cos = cos_ref[...]
    sin = sin_ref[...]
    k_heads, v_heads = [], []
    for j in range(QK_WIDTH // LANES):
        t = acc[:, j * LANES:(j + 1) * LANES]
        if use_norm:
            t = _pair_rms(t, gain_ref[:, j * LANES:(j + 1) * LANES])
        t = _rope_lanes(t, cos, sin, HEAD_DIM // 2)
        if j < Q_WIDTH // LANES:
            q_ref[:, j * LANES:(j + 1) * LANES] = (t * GQA_Q_SCALE).astype(BF16)
        else:
            for half in range(2):
                th = t[:, half * HEAD_DIM:(half + 1) * HEAD_DIM]
                kh_ref[len(k_heads)] = th.astype(BF16)
                k_heads.append(th)
    for kv_head in range(N_KV_HEADS):
        lo = QK_WIDTH + kv_head * HEAD_DIM
        tv = acc[:, lo:lo + HEAD_DIM]
        vh_ref[kv_head] = tv.astype(BF16)
        v_heads.append(tv)

    @pl.when(pl.program_id(0) < PROMPT_TILES)
    def _():
        for req in range(TM // SEQ):
            for kv_head in range(N_KV_HEADS):
                k32_ref[req, kv_head] = k_heads[kv_head][req * SEQ:(req + 1) * SEQ, :]
                v32_ref[req, kv_head] = v_heads[kv_head][req * SEQ:(req + 1) * SEQ, :]


def _gqa_proj(x, mods5, layer, w_qkv, gain, cos_t, sin_t, use_norm):
    head_spec = pl.BlockSpec((N_KV_HEADS, TM, HEAD_DIM), lambda i: (0, i, 0))
    pos_spec = pl.BlockSpec((TM, LANES), lambda i: (_pos_block_of_tile(i), 0))
    cache_spec = pl.BlockSpec((TM // SEQ, N_KV_HEADS, SEQ, HEAD_DIM),
                              lambda i: (jnp.minimum(i, PROMPT_TILES - 1), 0, 0, 0))
    x_specs, x_args = _row_inputs(x)
    return pl.pallas_call(
        functools.partial(_gqa_proj_kernel, use_norm=use_norm, n_x=len(x_args)),
        out_shape=(
            jax.ShapeDtypeStruct((N_TOK, Q_WIDTH), BF16),
            jax.ShapeDtypeStruct((N_KV_HEADS, N_TOK, HEAD_DIM), BF16),
            jax.ShapeDtypeStruct((N_KV_HEADS, N_TOK, HEAD_DIM), BF16),
            jax.ShapeDtypeStruct((BATCH, N_KV_HEADS, SEQ, HEAD_DIM), F32),
            jax.ShapeDtypeStruct((BATCH, N_KV_HEADS, SEQ, HEAD_DIM), F32),
        ),
        grid=(TILES,),
        in_specs=x_specs + [
            _mod_spec(layer, 0), _mod_spec(layer, 1),
            _full_spec((D_MODEL, QKV_WIDTH)), _full_spec((1, QK_WIDTH)),
            pos_spec, pos_spec,
        ],
        out_specs=(_row_spec(Q_WIDTH), head_spec, head_spec, cache_spec, cache_spec),
        compiler_params=_params(1),
        name="gqa_proj",
    )(*x_args, mods5, mods5, w_qkv, gain, cos_t, sin_t)


def _attend(qs, chunks, sink_col):
    m = den = acc = None
    for k_fn, v_fn, bias, _ in chunks:
        s = lax.dot_general(qs, k_fn(), _NT, preferred_element_type=F32)
        if bias is not None:
            s = s + bias
        mc = jnp.max(s, axis=-1, keepdims=True)
        m_new = mc if m is None else jnp.maximum(m, mc)
        p = jnp.exp2(s - m_new)
        ps = jnp.sum(p, axis=-1, keepdims=True)
        pv = jnp.dot(p.astype(BF16), v_fn(), preferred_element_type=F32)
        if m is None:
            den, acc = ps, pv
        else:
            alpha = jnp.exp2(m - m_new)
            den = den * alpha + ps
            acc = acc * alpha + pv
        m = m_new
    if sink_col is not None:
        den = den + jnp.exp2(sink_col - m)
    return acc / den


def _chunk_list(k_ref, v_ref, length, cast):
    out = []
    width = min(KV_CHUNK, length)
    for c0 in range(0, length, width):
        if cast:
            k_fn = lambda c0=c0: k_ref[c0:c0 + width, :].astype(BF16)
            v_fn = lambda c0=c0: v_ref[c0:c0 + width, :].astype(BF16)
        else:
            k_fn = lambda c0=c0: k_ref[c0:c0 + width, :]
            v_fn = lambda c0=c0: v_ref[c0:c0 + width, :]
        out.append((k_fn, v_fn, None, width))
    return out


def _gqa_attn_kernel(*refs, tq, n_keys, kv_groups, has_ctx, use_sink, window):
    refs = list(refs)
    q_ref, k_ref, v_ref = refs[:3]
    pos = 3
    if has_ctx:
        kc_ref, vc_ref = refs[pos:pos + 2]
        pos += 2
    if use_sink:
        sink_ref = refs[pos]
        pos += 1
    o_ref = refs[pos]
    qi = pl.program_id(2)
    width = GROUP * HEAD_DIM

    for gi in range(kv_groups):
        g = pl.program_id(1) * kv_groups + gi
        qf = q_ref[:, gi * width:(gi + 1) * width].astype(F32)
        qs = jnp.concatenate(
            [qf[:, j * HEAD_DIM:(j + 1) * HEAD_DIM] for j in range(GROUP)], axis=0).astype(BF16)
        kg_ref, vg_ref = k_ref.at[gi], v_ref.at[gi]

        if window:
            span = tq + 2 * WINDOW
            kstart = pl.multiple_of(jnp.clip(qi * tq - WINDOW, 0, n_keys - span), LANES)
            qpos = qi * tq + lax.broadcasted_iota(jnp.int32, (tq, span), 0)
            kpos = kstart + lax.broadcasted_iota(jnp.int32, (tq, span), 1)
            band = jnp.where(jnp.abs(kpos - qpos) <= WINDOW, 0.0, -jnp.inf).astype(F32)
            bias = jnp.concatenate([band] * GROUP, axis=0)
            chunks = [(lambda: kg_ref[pl.ds(kstart, span), :], lambda: vg_ref[pl.ds(kstart, span), :], bias, span)]
        else:
            chunks = _chunk_list(kg_ref, vg_ref, n_keys, cast=False)
        if has_ctx:
            chunks += _chunk_list(kc_ref.at[gi], vc_ref.at[gi], PAST_LEN, cast=True)

        sink_col = None
        if use_sink:
            row = lax.broadcasted_iota(jnp.int32, (GROUP * tq, 1), 0)
            sink_col = jnp.full((GROUP * tq, 1), sink_ref[g * GROUP + GROUP - 1], F32)
            for j in reversed(range(GROUP - 1)):
                sink_col = jnp.where(row < (j + 1) * tq, sink_ref[g * GROUP + j], sink_col)
            sink_col = sink_col * LOG2E

        o = _attend(qs, chunks, sink_col)
        o_ref[:, gi * width:(gi + 1) * width] = jnp.concatenate(
            [o[j * tq:(j + 1) * tq, :] for j in range(GROUP)], axis=-1).astype(BF16)


def _gqa_attention(q, kh, vh, ctx_k, ctx_v, slot, sink, *, latent, window):
    tq = GQA_TQ if latent and not window else ATT_TQ
    if latent:
        n_b, n_keys, n_qt = DEC_BATCH, DEC_SEQ, DEC_SEQ // tq
        row0 = N_PROMPT // tq
        kv_blk0 = N_PROMPT // DEC_SEQ
        kv_groups = 2
    else:
        n_b, n_keys, n_qt = BATCH, SEQ, 1
        row0 = 0
        kv_blk0 = 0
        kv_groups = N_KV_HEADS
    width = kv_groups * GROUP * HEAD_DIM
    q_spec = pl.BlockSpec((tq, width), lambda b, g, t: (row0 + b * n_qt + t, g))
    kv_spec = pl.BlockSpec((kv_groups, n_keys, HEAD_DIM), lambda b, g, t: (g, kv_blk0 + b, 0))
    in_specs = [q_spec, kv_spec, kv_spec]
    args = [q, kh, vh]
    if latent:
        ctx_spec = pl.BlockSpec((None, None, kv_groups, PAST_LEN, HEAD_DIM), lambda b, g, t: (b, slot, g, 0, 0))
        in_specs += [ctx_spec, ctx_spec]
        args += [ctx_k, ctx_v]
    use_sink = sink is not None
    if use_sink:
        in_specs.append(pl.BlockSpec(memory_space=pltpu.SMEM))
        args.append(sink)
    return pl.pallas_call(
        functools.partial(_gqa_attn_kernel, tq=tq, n_keys=n_keys, kv_groups=kv_groups, has_ctx=latent,
                          use_sink=use_sink, window=window),
        out_shape=jax.ShapeDtypeStruct((n_b * n_keys, Q_WIDTH), BF16),
        grid=(n_b, N_KV_HEADS // kv_groups, n_qt),
        in_specs=in_specs,
        out_specs=pl.BlockSpec((tq, width), lambda b, g, t: (b * n_qt + t, g)),
        compiler_params=_params(3),
        name="gqa_attn_latent" if latent else "gqa_attn_context",
    )(*args)


MLA_DOWN_EXT = Q_RANK + KV_RANK + LANES
MLA_HEAD_LANES = LANES
MLA_ROPE_LO = MLA_NOPE_DIM


def _mla_proj_kernel(x_ref, sh_ref, sc_ref, wd_ref, qg_ref, wuq_ref, kvg_ref, cos_ref, sin_ref,
                     q_ref, ckv_ref, kpe_ref):
    h = (x_ref[...] * (1.0 + sc_ref[...]) + sh_ref[...]).astype(BF16)
    acc = jnp.dot(h, wd_ref[...], preferred_element_type=F32)
    cos = cos_ref[...]
    sin = sin_ref[...]
    qa = acc[:, :Q_RANK]
    qn = qa * lax.rsqrt(jnp.mean(qa * qa, axis=-1, keepdims=True) + NORM_EPS) * qg_ref[...]
    q = jnp.dot(qn.astype(BF16), wuq_ref[...], preferred_element_type=F32)
    for hd in range(MLA_HEADS):
        t = q[:, hd * LANES:(hd + 1) * LANES]
        q_ref[:, hd * LANES:(hd + 1) * LANES] = (_rope_lanes(t, cos, sin, MLA_ROPE_DIM // 2) * MLA_Q_SCALE).astype(BF16)
    ckv = acc[:, Q_RANK:Q_RANK + KV_RANK]
    ckv_ref[...] = ckv * lax.rsqrt(jnp.mean(ckv * ckv, axis=-1, keepdims=True) + NORM_EPS) * kvg_ref[...]
    kpe_ref[...] = _rope_lanes(acc[:, Q_RANK + KV_RANK:], cos, sin, MLA_ROPE_DIM // 2)


def _mla_proj(x, mods5, layer, wd_ext, q_gain, wuq_pad, kv_gain, cos_t, sin_t):
    pos_spec = pl.BlockSpec((TM, LANES), lambda i: (_pos_block_of_tile(i), 0))
    return pl.pallas_call(
        _mla_proj_kernel,
        out_shape=(
            jax.ShapeDtypeStruct((N_TOK, MLA_HEADS * LANES), BF16),
            jax.ShapeDtypeStruct((N_TOK, KV_RANK), F32),
            jax.ShapeDtypeStruct((N_TOK, LANES), F32),
        ),
        grid=(TILES,),
        in_specs=[
            _row_spec(D_MODEL), _mod_spec(layer, 0), _mod_spec(layer, 1),
            _full_spec((D_MODEL, MLA_DOWN_EXT)), _full_spec((1, Q_RANK)),
            _full_spec((Q_RANK, MLA_HEADS * LANES)), _full_spec((1, KV_RANK)),
            pos_spec, pos_spec,
        ],
        out_specs=(_row_spec(MLA_HEADS * LANES), _row_spec(KV_RANK), _row_spec(LANES)),
        compiler_params=_params(1),
        name="mla_proj",
    )(x, mods5, mods5, wd_ext, q_gain, wuq_pad, kv_gain, cos_t, sin_t)


def _mla_kv_kernel(c_ref, kpe_ref, wk_ref, wv_ref, k_ref, v_ref):
    c = c_ref[...].astype(BF16)
    kk = jnp.dot(c, wk_ref[...], preferred_element_type=F32)
    vv = jnp.dot(c, wv_ref[...], preferred_element_type=F32)
    kpe = kpe_ref[...]
    for hd in range(MLA_HEADS):
        k_ref[hd] = (kk[:, hd * LANES:(hd + 1) * LANES] + kpe).astype(BF16)
        v_ref[hd] = vv[:, hd * MLA_V_DIM:(hd + 1) * MLA_V_DIM].astype(BF16)


def _mla_kv(c_all, kpe_all, wk_pad, wv):
    n_rows = c_all.shape[0]
    return pl.pallas_call(
        _mla_kv_kernel,
        out_shape=(
            jax.ShapeDtypeStruct((MLA_HEADS, n_rows, LANES), BF16),
            jax.ShapeDtypeStruct((MLA_HEADS, n_rows, MLA_V_DIM), BF16),
        ),
        grid=(n_rows // TM,),
        in_specs=[
            _row_spec(KV_RANK), _row_spec(LANES),
            _full_spec((KV_RANK, MLA_HEADS * LANES)), _full_spec((KV_RANK, MLA_HEADS * MLA_V_DIM)),
        ],
        out_specs=(
            pl.BlockSpec((MLA_HEADS, TM, LANES), lambda i: (0, i, 0)),
            pl.BlockSpec((MLA_HEADS, TM, MLA_V_DIM), lambda i: (0, i, 0)),
        ),
        compiler_params=_params(1),
        name="mla_kv",
    )(c_all, kpe_all, wk_pad, wv)


def _mla_attn_kernel(*refs, tq, n_keys, heads, has_ctx):
    refs = list(refs)
    q_ref, k_ref, v_ref = refs[:3]
    pos = 3
    if has_ctx:
        kc_ref, vc_ref = refs[pos:pos + 2]
        pos += 2
    o_ref = refs[pos]
    outs = []
    for j in range(heads):
        qs = q_ref[:, j * LANES:(j + 1) * LANES]
        chunks = _chunk_list(k_ref.at[j], v_ref.at[j], n_keys, cast=False)
        if has_ctx:
            chunks += _chunk_list(kc_ref.at[j], vc_ref.at[j], PAST_LEN, cast=False)
        outs.append(_attend(qs, chunks, None))
    o_ref[...] = jnp.concatenate(outs, axis=-1).astype(BF16)


def _mla_attention(q, k_all, v_all, *, latent):
    if latent:
        tq = MLA_TQ
        n_b, n_keys, n_qt = DEC_BATCH, DEC_SEQ, DEC_SEQ // tq
        row0 = N_PROMPT // tq
        kv_blk0 = N_PROMPT // DEC_SEQ
        heads = 4
    else:
        tq = ATT_TQ
        n_b, n_keys, n_qt = BATCH, SEQ, 1
        row0 = 0
        kv_blk0 = 0
        heads = 8
    q_spec = pl.BlockSpec((tq, heads * LANES), lambda b, hp, t: (row0 + b * n_qt + t, hp))
    k_spec = pl.BlockSpec((heads, n_keys, LANES), lambda b, hp, t: (hp, kv_blk0 + b, 0))
    v_spec = pl.BlockSpec((heads, n_keys, MLA_V_DIM), lambda b, hp, t: (hp, kv_blk0 + b, 0))
    in_specs = [q_spec, k_spec, v_spec]
    args = [q, k_all, v_all]
    if latent:
        ctx0 = N_TOK // PAST_LEN
        in_specs += [
            pl.BlockSpec((heads, PAST_LEN, LANES), lambda b, hp, t: (hp, ctx0 + b, 0)),
            pl.BlockSpec((heads, PAST_LEN, MLA_V_DIM), lambda b, hp, t: (hp, ctx0 + b, 0)),
        ]
        args += [k_all, v_all]
    return pl.pallas_call(
        functools.partial(_mla_attn_kernel, tq=tq, n_keys=n_keys, heads=heads, has_ctx=latent),
        out_shape=jax.ShapeDtypeStruct((n_b * n_keys, MLA_HEADS * MLA_V_DIM), BF16),
        grid=(n_b, MLA_HEADS // heads, n_qt),
        in_specs=in_specs,
        out_specs=pl.BlockSpec((tq, heads * MLA_V_DIM), lambda b, hp, t: (b * n_qt + t, hp)),
        compiler_params=_params(3),
        name="mla_attn_latent" if latent else "mla_attn_context",
    )(*args)


def _post_attn_kernel(*refs, n_x):
    (op_ref, os_ref), x_refs, refs = refs[:2], refs[2:2 + n_x], refs[2 + n_x:]
    wo_ref, g1_ref, sh2_ref, sc2_ref, lng_ref, lnb_ref, wr_ref, x1_ref, h2_ref, aff_ref, fz_ref = refs
    proj = jnp.dot(_read_rows((op_ref, os_ref)), wo_ref[...], preferred_element_type=F32)
    x1 = _layer_norm(ALPHA * _read_rows(x_refs) + g1_ref[...] * proj, lng_ref[...], lnb_ref[...])
    x1_ref[...] = x1
    h2 = x1 * (1.0 + sc2_ref[...]) + sh2_ref[...]
    _store_token_tiles(h2_ref, h2)
    logits_t = _dot3(wr_ref[...], h2, _NT)
    e = jnp.exp(logits_t - jnp.max(logits_t, axis=0, keepdims=True))
    aff_ref[...] = e / jnp.sum(e, axis=0, keepdims=True)
    fz_ref[...] = jnp.zeros_like(fz_ref)


def _post_attn(o_p, o_s, x, w_o, mods5, layer, ln_g, ln_b, w_router_t):
    x_specs, x_args = _row_inputs(x)
    return pl.pallas_call(
        functools.partial(_post_attn_kernel, n_x=len(x_args)),
        out_shape=(
            jax.ShapeDtypeStruct((N_TOK, D_MODEL), F32),
            jax.ShapeDtypeStruct((N_TOK * TOK_TILE, LANES), F32),
            jax.ShapeDtypeStruct((N_EXPERTS, N_TOK), F32),
            jax.ShapeDtypeStruct((N_TOK * TOK_TILE, LANES), F32),
        ),
        grid=(TILES,),
        in_specs=_stream_specs(D_MODEL) + x_specs + [
            _full_spec((D_MODEL, D_MODEL)),
            _mod_spec(layer, 2), _mod_spec(layer, 3), _mod_spec(layer, 4),
            _full_spec((1, D_MODEL)), _full_spec((1, D_MODEL)), _full_spec((N_EXPERTS, D_MODEL)),
        ],
        out_specs=(_row_spec(D_MODEL), _tok_tile_spec(), pl.BlockSpec((N_EXPERTS, TM), lambda i: (0, i)),
                   _tok_tile_spec()),
        compiler_params=_params(1),
        name="post_attn",
    )(o_p, o_s, *x_args, w_o, mods5, mods5, mods5, ln_g, ln_b, w_router_t)


FFN_ROWS = 512
FFN_STEPS = EXPERT_FF // FF_TILE
ROW_CHUNKS = CAP // FFN_ROWS
GATHER_PER_CHUNK = CAP // (FFN_STEPS * ROW_CHUNKS)


N_DMA_THREADS = 2


def _token_copy(src, src_tok, dst, dst_tok, sem):
    return pltpu.make_async_copy(_token_tiles(src).at[src_tok], _token_tiles(dst).at[dst_tok], sem)


def _wait_buffer(buf_ref, sem):
    pltpu.make_async_copy(buf_ref, buf_ref, sem).wait()


def _ffn_kernel(idx_ref, h_hbm, f_in_hbm, wg_ref, wu_ref, wd_ref, gate_ref, f_hbm,
                xbuf, acc, fbuf, gsem, rsem, wsem):
    del f_in_hbm
    e = pl.program_id(0)
    f = pl.program_id(1)
    buf = e % 2
    nxt = jnp.minimum(e + 1, N_EXPERTS - 1)

    def gather(expert, s, b):
        return _token_copy(h_hbm, idx_ref[expert * CAP + s], xbuf.at[b], s, gsem.at[b])

    @pl.when((e == 0) & (f == 0))
    def _():
        def issue(i, carry):
            for thread in range(N_DMA_THREADS):
                gather(0, i * N_DMA_THREADS + thread, 0).start(priority=thread)
            return carry
        lax.fori_loop(0, CAP // N_DMA_THREADS, issue, 0, unroll=8)

    @pl.when(f == 0)
    def _():
        _wait_buffer(xbuf.at[buf], gsem.at[buf])

    @pl.when((f == 0) & (e > 0))
    def _():
        _wait_buffer(fbuf, wsem.at[0])

    def write_rows(r_lo, n_rows):
        def body(i, carry):
            for thread in range(N_DMA_THREADS):
                r = r_lo + i * N_DMA_THREADS + thread
                _token_copy(fbuf, r, f_hbm, idx_ref[e * CAP + r], wsem.at[0]).start(priority=thread)
            return carry
        lax.fori_loop(0, n_rows // N_DMA_THREADS, body, 0, unroll=16)

    wg = wg_ref[...].astype(BF16)
    wu = wu_ref[...].astype(BF16)
    wd = wd_ref[...].astype(BF16)
    for c in range(ROW_CHUNKS):
        r0 = c * FFN_ROWS
        x = _load_token_tiles(xbuf.at[buf], FFN_ROWS, r0).astype(BF16)
        a = jnp.dot(x, wg, preferred_element_type=F32)
        u = jnp.dot(x, wu, preferred_element_type=F32)
        hid = (a / (1.0 + jnp.exp(-a)) * u).astype(BF16)
        part = jnp.dot(hid, wd, preferred_element_type=F32)
        for j in range(GATHER_PER_CHUNK):
            s = (f * ROW_CHUNKS + c) * GATHER_PER_CHUNK + j
            gather(nxt, s, 1 - buf).start(priority=j % N_DMA_THREADS)
            _token_copy(f_hbm, idx_ref[e * CAP + s], fbuf, s, rsem.at[0]).start(priority=(j + 1) % N_DMA_THREADS)
        acc[r0:r0 + FFN_ROWS, :] = jnp.where(f == 0, 0.0, acc[r0:r0 + FFN_ROWS, :]) + part

    @pl.when(f == FFN_STEPS - 1)
    def _():
        _wait_buffer(fbuf, rsem.at[0])
        for k in range(ROW_CHUNKS):
            r0 = k * FFN_ROWS
            g = gate_ref[r0:r0 + FFN_ROWS, :]
            for j in range(TOK_TILE):
                rows = _lane_block_rows(j, FFN_ROWS, r0)
                fbuf[rows, :] = fbuf[rows, :] + acc[r0:r0 + FFN_ROWS, j * LANES:(j + 1) * LANES] * g
            write_rows(r0, FFN_ROWS)

    @pl.when((e == N_EXPERTS - 1) & (f == FFN_STEPS - 1))
    def _():
        _wait_buffer(fbuf, wsem.at[0])
        _wait_buffer(xbuf.at[1 - buf], gsem.at[1 - buf])


def _expert_ffn(idx, h2, f_zero, gate, w_gate, w_up, w_down, layer):
    grid_spec = pltpu.PrefetchScalarGridSpec(
        num_scalar_prefetch=1,
        grid=(N_EXPERTS, FFN_STEPS),
        in_specs=[
            pl.BlockSpec(memory_space=pl.ANY),
            pl.BlockSpec(memory_space=pl.ANY),
            pl.BlockSpec((None, None, D_MODEL, FF_TILE), lambda e, f, idx: (layer, e, 0, f)),
            pl.BlockSpec((None, None, D_MODEL, FF_TILE), lambda e, f, idx: (layer, e, 0, f)),
            pl.BlockSpec((None, None, FF_TILE, D_MODEL), lambda e, f, idx: (layer, e, f, 0)),
            pl.BlockSpec((None, CAP, 1), lambda e, f, idx: (e, 0, 0)),
        ],
        out_specs=pl.BlockSpec(memory_space=pl.ANY),
        scratch_shapes=[
            pltpu.VMEM((2, CAP * TOK_TILE, LANES), F32),
            pltpu.VMEM((CAP, D_MODEL), F32),
            pltpu.VMEM((CAP * TOK_TILE, LANES), F32),
            pltpu.SemaphoreType.DMA((2,)), pltpu.SemaphoreType.DMA((1,)), pltpu.SemaphoreType.DMA((1,)),
        ],
    )
    tiles = (N_TOK, TOK_TILE, LANES)
    f = pl.pallas_call(
        _ffn_kernel,
        out_shape=jax.ShapeDtypeStruct(tiles, F32),
        grid_spec=grid_spec,
        input_output_aliases={2: 0},
        compiler_params=_params(2),
        name="expert_ffn",
    )(idx, h2.reshape(tiles), f_zero.reshape(tiles), w_gate, w_up, w_down, gate)
    return f.reshape(N_TOK * TOK_TILE, LANES)


def _post_moe_kernel(x_ref, f_ref, g2_ref, lng_ref, lnb_ref, *o_refs):
    f = _load_token_tiles(f_ref, TM)
    y = _layer_norm(ALPHA * x_ref[...] + g2_ref[...] * f, lng_ref[...], lnb_ref[...])
    if len(o_refs) == 1:
        o_refs[0][...] = y
    else:
        @pl.when(pl.program_id(0) < PROMPT_TILES)
        def _():
            o_refs[0][...] = y

        @pl.when(pl.program_id(0) >= PROMPT_TILES)
        def _():
            o_refs[1][...] = y


IN_BUFFERS = 3
OUT_BUFFERS = 2


def _post_moe_stream_kernel(x_hbm, f_hbm, g2_ref, lng_ref, lnb_ref, o_hbm, xbuf, fbuf, obuf, isem, osem):
    def in_copies(i, slot):
        return (pltpu.make_async_copy(x_hbm.at[pl.ds(i * TM, TM), :], xbuf.at[slot], isem.at[0, slot]),
                pltpu.make_async_copy(f_hbm.at[pl.ds(i * TM * TOK_TILE, TM * TOK_TILE), :], fbuf.at[slot],
                                      isem.at[1, slot]))

    def out_copy(i, slot):
        return pltpu.make_async_copy(obuf.at[slot], o_hbm.at[pl.ds(i * TM, TM), :], osem.at[slot])

    for i in range(IN_BUFFERS - 1):
        for cp in in_copies(i, i):
            cp.start()

    def body(i, carry):
        slot = i % IN_BUFFERS
        ahead = i + IN_BUFFERS - 1

        @pl.when(ahead < TILES)
        def _():
            for cp in in_copies(ahead, ahead % IN_BUFFERS):
                cp.start()

        for cp in in_copies(i, slot):
            cp.wait()
        oslot = i % OUT_BUFFERS

        @pl.when(i >= OUT_BUFFERS)
        def _():
            out_copy(i - OUT_BUFFERS, oslot).wait()

        f = _load_token_tiles(fbuf.at[slot], TM)
        g2 = g2_ref[_cond_of_tile(i)]
        obuf[oslot] = _layer_norm(ALPHA * xbuf[slot] + g2 * f, lng_ref[...], lnb_ref[...])
        out_copy(i, oslot).start()
        return carry

    lax.fori_loop(0, TILES, body, 0)
    for i in range(TILES - OUT_BUFFERS, TILES):
        out_copy(i, i % OUT_BUFFERS).wait()


def _post_moe_stream(x1, f, mods5, layer, ln_g, ln_b):
    return pl.pallas_call(
        _post_moe_stream_kernel,
        out_shape=jax.ShapeDtypeStruct((N_TOK, D_MODEL), F32),
        grid=(1,),
        in_specs=[pl.BlockSpec(memory_space=pl.ANY), pl.BlockSpec(memory_space=pl.ANY),
                  pl.BlockSpec((None, None, N_COND, 1, D_MODEL), lambda i: (layer, 5, 0, 0, 0)),
                  _full_spec((1, D_MODEL)), _full_spec((1, D_MODEL))],
        out_specs=pl.BlockSpec(memory_space=pl.ANY),
        scratch_shapes=[
            pltpu.VMEM((IN_BUFFERS, TM, D_MODEL), F32),
            pltpu.VMEM((IN_BUFFERS, TM * TOK_TILE, LANES), F32),
            pltpu.VMEM((OUT_BUFFERS, TM, D_MODEL), F32),
            pltpu.SemaphoreType.DMA((2, IN_BUFFERS)), pltpu.SemaphoreType.DMA((OUT_BUFFERS,)),
        ],
        compiler_params=_params(1),
        name="post_moe_stream",
    )(x1, f, mods5, ln_g, ln_b)


def _post_moe(x1, f, mods5, layer, ln_g, ln_b, split_streams):
    if split_streams:
        out_shape = (jax.ShapeDtypeStruct((N_PROMPT, D_MODEL), F32), jax.ShapeDtypeStruct((N_SAMPLE, D_MODEL), F32))
        out_specs = tuple(_stream_specs(D_MODEL))
    else:
        return _post_moe_stream(x1, f, mods5, layer, ln_g, ln_b)
    return pl.pallas_call(
        _post_moe_kernel,
        out_shape=out_shape,
        grid=(TILES,),
        in_specs=[_row_spec(D_MODEL), _tok_tile_spec(), _mod_spec(layer, 5),
                  _full_spec((1, D_MODEL)), _full_spec((1, D_MODEL))],
        out_specs=out_specs,
        compiler_params=_params(1),
        name="post_moe",
    )(x1, f, mods5, ln_g, ln_b)


def _axial_tables(dim):
    n_rows = DEC_SEQ // GRID_W
    rows = jnp.repeat(jnp.arange(n_rows, dtype=F32), GRID_W)
    cols = jnp.tile(jnp.arange(GRID_W, dtype=F32), n_rows)
    n_freq = dim // 4
    inv_freq = ROPE_THETA ** (-jnp.arange(n_freq, dtype=F32) / n_freq)
    ang = jnp.concatenate([rows[:, None] * inv_freq, cols[:, None] * inv_freq], -1)
    cos, sin = jnp.cos(ang), jnp.sin(ang)
    return jnp.concatenate([cos, cos], -1), jnp.concatenate([-sin, sin], -1)


def _with_identity_rows(cos_l, sin_l):
    return (jnp.concatenate([jnp.ones((TM, LANES), F32), cos_l], 0),
            jnp.concatenate([jnp.zeros((TM, LANES), F32), sin_l], 0))


def _gqa_rope_tables():
    cos, sin = _axial_tables(HEAD_DIM)
    return _with_identity_rows(jnp.tile(cos, (1, 2)), jnp.tile(sin, (1, 2)))


def _mla_rope_tables():
    cos, sin = _axial_tables(MLA_ROPE_DIM)
    pad_hi = LANES - MLA_ROPE_LO - MLA_ROPE_DIM
    cos_l = jnp.concatenate([jnp.ones((DEC_SEQ, MLA_ROPE_LO), F32), cos, jnp.ones((DEC_SEQ, pad_hi), F32)], 1)
    sin_l = jnp.pad(sin, ((0, 0), (MLA_ROPE_LO, pad_hi)))
    return _with_identity_rows(cos_l, sin_l)


def _moe(h2, f_zero, aff_t, w_gate, w_up, w_down, layer):
    gate_p, idx_p = lax.top_k(aff_t[:, :N_PROMPT], CAP_P)
    gate_s, idx_s = lax.top_k(aff_t[:, N_PROMPT:], CAP_S)
    idx = jnp.concatenate([idx_p, idx_s + N_PROMPT], axis=1).reshape(-1)
    gate = jnp.concatenate([gate_p, gate_s], axis=1)[..., None]
    return _expert_ffn(idx, h2, f_zero, gate, w_gate, w_up, w_down, layer)


def kernel(x_prompt, x_sample, cache_gqa_k, cache_gqa_v, cache_mla_ckv, cache_mla_kpe, c, c_ctx,
           ada_w, ada_b, ln_g, ln_b, a_w_qkv, a_q_norm, a_k_norm, a_w_o, b_w_qkv, b_sink, b_w_o,
           mla_w_down, mla_q_norm, mla_w_uq, mla_kv_norm, mla_w_ukv, mla_w_o,
           moe_w_router, moe_w_gate, moe_w_up, moe_w_down):
    x = (x_prompt.reshape(N_PROMPT, D_MODEL), x_sample.reshape(N_SAMPLE, D_MODEL))
    cond = jnp.concatenate([c_ctx[None], c, jnp.zeros((N_COND - 1 - DEC_BATCH, D_MODEL), F32)], 0)
    mods5 = _adaln(cond, ada_w, ada_b).reshape(DEPTH, 6, N_COND, 1, D_MODEL)
    gqa_cos, gqa_sin = _gqa_rope_tables()
    mla_cos, mla_sin = _mla_rope_tables()

    new_k, new_v, new_ckv, new_kpe = [], [], [], []
    gqa_slot = 0
    for i in range(DEPTH):
        kind, j = i % 3, i // 3
        if kind in (0, 1):
            if kind == 0:
                w_qkv, w_o, sink = a_w_qkv[j], a_w_o[j], None
                gain = jnp.concatenate([jnp.tile(a_q_norm[j], N_HEADS), jnp.tile(a_k_norm[j], N_KV_HEADS)])[None]
            else:
                w_qkv, w_o, sink = b_w_qkv[j], b_w_o[j], b_sink[j]
                gain = jnp.ones((1, QK_WIDTH), F32)
            q, kh, vh, k32, v32 = _gqa_proj(x, mods5, i, w_qkv.astype(BF16), gain, gqa_cos, gqa_sin,
                                            use_norm=(kind == 0))
            o_p = _gqa_attention(q, kh, vh, None, None, gqa_slot, sink, latent=False, window=False)
            o_s = _gqa_attention(q, kh, vh, cache_gqa_k, cache_gqa_v, gqa_slot, sink,
                                 latent=True, window=(kind == 1))
            new_k.append(k32)
            new_v.append(v32)
            gqa_slot += 1
        else:
            w_down = mla_w_down[j]
            rope_lo = Q_RANK + KV_RANK
            wd_ext = jnp.concatenate([
                w_down[:, :rope_lo], jnp.zeros((D_MODEL, MLA_ROPE_LO), F32), w_down[:, rope_lo:],
                jnp.zeros((D_MODEL, LANES - MLA_ROPE_LO - MLA_ROPE_DIM), F32)], 1).astype(BF16)
            wuq_pad = jnp.pad(mla_w_uq[j].reshape(Q_RANK, MLA_HEADS, MLA_QK_DIM),
                              ((0, 0), (0, 0), (0, LANES - MLA_QK_DIM))).reshape(Q_RANK, MLA_HEADS * LANES)
            w_ukv = mla_w_ukv[j].reshape(KV_RANK, MLA_HEADS, MLA_NOPE_DIM + MLA_V_DIM)
            wk_pad = jnp.pad(w_ukv[:, :, :MLA_NOPE_DIM], ((0, 0), (0, 0), (0, LANES - MLA_NOPE_DIM)))
            wk_pad = wk_pad.reshape(KV_RANK, MLA_HEADS * LANES).astype(BF16)
            wv = w_ukv[:, :, MLA_NOPE_DIM:].reshape(KV_RANK, MLA_HEADS * MLA_V_DIM).astype(BF16)
            q, ckv, kpe = _mla_proj(x, mods5, i, wd_ext, mla_q_norm[j][None], wuq_pad.astype(BF16),
                                    mla_kv_norm[j][None], mla_cos, mla_sin)
            c_all = jnp.concatenate([ckv, cache_mla_ckv[:, j].reshape(DEC_BATCH * PAST_LEN, KV_RANK)], 0)
            kpe_ctx = jnp.pad(cache_mla_kpe[:, j].reshape(DEC_BATCH * PAST_LEN, MLA_ROPE_DIM),
                              ((0, 0), (MLA_ROPE_LO, LANES - MLA_ROPE_LO - MLA_ROPE_DIM)))
            kpe_all = jnp.concatenate([kpe, kpe_ctx], 0)
            k_all, v_all = _mla_kv(c_all, kpe_all, wk_pad, wv)
            o_p = _mla_attention(q, k_all, v_all, latent=False)
            o_s = _mla_attention(q, k_all, v_all, latent=True)
            w_o = mla_w_o[j]
            new_ckv.append(ckv[:N_PROMPT].reshape(BATCH, SEQ, KV_RANK))
            new_kpe.append(kpe[:N_PROMPT, MLA_ROPE_LO:MLA_ROPE_LO + MLA_ROPE_DIM].reshape(BATCH, SEQ, MLA_ROPE_DIM))
        x1, h2, aff_t, f_zero = _post_attn(o_p, o_s, x, w_o.astype(BF16), mods5, i, ln_g[i, 0][None],
                                           ln_b[i, 0][None], moe_w_router[i].T)
        f = _moe(h2, f_zero, aff_t, moe_w_gate, moe_w_up, moe_w_down, i)
        x = _post_moe(x1, f, mods5, i, ln_g[i, 1][None], ln_b[i, 1][None], split_streams=(i == DEPTH - 1))

    y_prompt = x[0].reshape(BATCH, SEQ, D_MODEL)
    y_sample = x[1].reshape(DEC_BATCH, DEC_SEQ, D_MODEL)
    return (y_prompt, y_sample, jnp.stack(new_k, 1), jnp.stack(new_v, 1),
            jnp.stack(new_ckv, 1), jnp.stack(new_kpe, 1))
```
